```python
import math
import jax, jax.numpy as jnp
from jax import lax
import numpy as np

D_MODEL = 2048
BATCH = 2
SEQ = 16384
DEPTH = 1

ATTN_HEAD_DIM = 64
ATTN_WIDTH = D_MODEL // 2
ATTN_HEADS = ATTN_WIDTH // ATTN_HEAD_DIM
ATTN_KV_HEADS = ATTN_HEADS // 4
ATTN_GROUP = ATTN_HEADS // ATTN_KV_HEADS
ATTN_KV_COLS = ATTN_KV_HEADS * ATTN_HEAD_DIM
WINDOW = 128
BLOCK = 128
HALO = -(-WINDOW // BLOCK)
N_BUCKETS = 32
MAX_DISTANCE = 128

RET_WIDTH = D_MODEL - ATTN_WIDTH
RET_HEADS = 8
RET_VAL_DIM = RET_WIDTH // RET_HEADS
RET_KEY_DIM = RET_VAL_DIM // 2
RET_QK_COLS = RET_HEADS * RET_KEY_DIM
CHUNK = 128
ROPE_BASE = 10000.0

MIX_WIDTH = ATTN_WIDTH + RET_WIDTH
SPLIT_SIZES = (ATTN_WIDTH, ATTN_KV_COLS, ATTN_KV_COLS, ATTN_WIDTH,
               RET_QK_COLS, RET_QK_COLS, RET_WIDTH, RET_WIDTH)
IN_WIDTH = sum(SPLIT_SIZES)
EPS = 1e-6

kernel_name = 'hymba_swa_retention_block'


def rmsnorm(x, w):
    x32 = x.astype(jnp.float32)
    x32 = x32 * lax.rsqrt(jnp.mean(x32 * x32, axis=-1, keepdims=True) + EPS)
    return x32.astype(x.dtype) * w


def t5_bucket(rel):
    nb = N_BUCKETS // 2
    max_exact = nb // 2
    ret = jnp.where(rel > 0, nb, 0)
    n = jnp.abs(rel)
    nf = jnp.maximum(n, 1).astype(jnp.float32)
    large = max_exact + (jnp.log(nf / max_exact) / math.log(MAX_DISTANCE / max_exact)
                         * (nb - max_exact)).astype(jnp.int32)
    large = jnp.minimum(large, nb - 1)
    return ret + jnp.where(n < max_exact, n, large)


def windowed_attention(q, k, v, sink, rel_bias):
    B, S = q.shape[0], q.shape[1]
    nb = S // BLOCK
    KW = (2 * HALO + 1) * BLOCK
    qb = q.reshape(B, nb, BLOCK, ATTN_KV_HEADS, ATTN_GROUP, ATTN_HEAD_DIM)
    pad = ((0, 0), (HALO * BLOCK, HALO * BLOCK), (0, 0), (0, 0))
    kp = jnp.pad(k, pad).reshape(B, nb + 2 * HALO, BLOCK, ATTN_KV_HEADS, ATTN_HEAD_DIM)
    vp = jnp.pad(v, pad).reshape(B, nb + 2 * HALO, BLOCK, ATTN_KV_HEADS, ATTN_HEAD_DIM)
    kb = jnp.concatenate([kp[:, o:o + nb] for o in range(2 * HALO + 1)], axis=2)
    vb = jnp.concatenate([vp[:, o:o + nb] for o in range(2 * HALO + 1)], axis=2)

    scale = ATTN_HEAD_DIM ** -0.5
    s = jnp.einsum('bnqhgd,bnkhd->bhgnqk', qb, kb).astype(jnp.float32) * scale

    qi = jnp.arange(BLOCK, dtype=jnp.int32)[:, None]
    kt = jnp.arange(KW, dtype=jnp.int32)[None, :]
    rel = kt - HALO * BLOCK - qi
    bias = rel_bias.astype(jnp.float32)[t5_bucket(rel)]
    bias = jnp.moveaxis(bias, -1, 0).reshape(ATTN_KV_HEADS, ATTN_GROUP, 1, BLOCK, KW)
    key_pos = (jnp.arange(nb, dtype=jnp.int32)[:, None, None] * BLOCK
               + kt[None] - HALO * BLOCK)
    valid = (jnp.abs(rel)[None] <= WINDOW) & (key_pos >= 0) & (key_pos < S)
    s = jnp.where(valid[None, None, None], s + bias[None], -1e30)

    sink_b = sink.astype(jnp.float32).reshape(1, ATTN_KV_HEADS, ATTN_GROUP, 1, 1, 1)
    m = jnp.maximum(jnp.max(s, axis=-1, keepdims=True), sink_b)
    p = jnp.exp(s - m)
    p = p / (jnp.sum(p, axis=-1, keepdims=True) + jnp.exp(sink_b - m))
    out = jnp.einsum('bhgnqk,bnkhd->bnqhgd', p.astype(v.dtype), vb)
    return out.reshape(B, S, ATTN_WIDTH)


def rotary(x):
    S, D = x.shape[1], x.shape[-1]
    inv = ROPE_BASE ** (-jnp.arange(0, D, 2, dtype=jnp.float32) / D)
    ang = jnp.arange(S, dtype=jnp.float32)[:, None] * inv[None, :]
    cos = jnp.cos(ang)[None, :, None, :]
    sin = jnp.sin(ang)[None, :, None, :]
    xr = x.astype(jnp.float32).reshape(x.shape[:-1] + (D // 2, 2))
    x1, x2 = xr[..., 0], xr[..., 1]
    return jnp.stack([x1 * cos - x2 * sin, x1 * sin + x2 * cos], axis=-1).reshape(x.shape)


def retention_chunkwise(q, k, v, log_gamma, strict):
    B, S, H, Dk = q.shape
    Dv = v.shape[-1]
    nc = S // CHUNK
    qc = q.reshape(B, nc, CHUNK, H, Dk)
    kc = k.reshape(B, nc, CHUNK, H, Dk)
    vc = v.reshape(B, nc, CHUNK, H, Dv)
    idx = jnp.arange(CHUNK, dtype=jnp.float32)
    diff = idx[:, None] - idx[None, :]
    keep = diff > 0 if strict else diff >= 0
    dmask = jnp.where(keep[None], jnp.exp(log_gamma[:, None, None] * jnp.maximum(diff, 0.0)[None]), 0.0)
    s = jnp.einsum('bnihd,bnjhd->bhnij', qc, kc) * dmask[None, :, None]
    intra = jnp.einsum('bhnij,bnjhe->bnihe', s, vc)
    kdec = jnp.exp(log_gamma[None, :] * (CHUNK - 1 - idx)[:, None])
    kv = jnp.einsum('bnjhd,bnjhe->nbhde', kc * kdec[None, None, :, :, None], vc)
    chunk_decay = jnp.exp(log_gamma * CHUNK)[None, :, None, None]

    def step(state, kv_c):
        return state * chunk_decay + kv_c, state

    _, prev = lax.scan(step, jnp.zeros((B, H, Dk, Dv), jnp.float32), kv)
    qdec = jnp.exp(log_gamma[None, :] * (idx + 1.0)[:, None])
    cross = jnp.einsum('bnihd,nbhde->bnihe', qc * qdec[None, None, :, :, None], prev)
    return (intra + cross).reshape(B, S, H, Dv)


def bidirectional_retention(q, k, v, decay_fwd, decay_bwd):
    lg_f = jax.nn.log_sigmoid(decay_fwd.astype(jnp.float32))
    lg_b = jax.nn.log_sigmoid(decay_bwd.astype(jnp.float32))
    fwd = retention_chunkwise(q, k, v, lg_f, strict=False)
    flip = lambda t: jnp.flip(t, axis=1)
    bwd = flip(retention_chunkwise(flip(q), flip(k), flip(v), lg_b, strict=True))
    return fwd + bwd


def head_groupnorm(o, w):
    mu = jnp.mean(o, axis=-1, keepdims=True)
    var = jnp.mean((o - mu) ** 2, axis=-1, keepdims=True)
    o = (o - mu) * lax.rsqrt(var + EPS)
    return o.reshape(o.shape[0], o.shape[1], -1) * w.astype(jnp.float32)


def setup_inputs(seed: int = 0) -> dict:
    key = jax.random.key(seed)
    ks = jax.random.split(key, 12)
    f32 = jnp.float32
    x = jax.random.normal(ks[0], (BATCH, SEQ, D_MODEL), f32)
    norm_w = 1.0 + 0.02 * jax.random.normal(ks[1], (DEPTH, D_MODEL), f32)
    w_in = jax.random.normal(ks[2], (DEPTH, D_MODEL, IN_WIDTH), f32) * D_MODEL ** -0.5
    attn_sink = 0.5 * jax.random.normal(ks[3], (DEPTH, ATTN_HEADS), f32)
    rel_bias = 0.1 * jax.random.normal(ks[4], (N_BUCKETS, ATTN_HEADS), f32)
    attn_out_norm_w = 1.0 + 0.02 * jax.random.normal(ks[5], (DEPTH, ATTN_WIDTH), f32)
    scales = jnp.log(2.0 ** (5.0 + jnp.arange(RET_HEADS, dtype=f32)) - 1.0)
    ret_decay_fwd = scales[None] + 0.05 * jax.random.normal(ks[6], (DEPTH, RET_HEADS), f32)
    ret_decay_bwd = scales[None] + 0.05 * jax.random.normal(ks[7], (DEPTH, RET_HEADS), f32)
    ret_gn_w = 1.0 + 0.02 * jax.random.normal(ks[8], (DEPTH, RET_WIDTH), f32)
    w_out = jax.random.normal(ks[9], (DEPTH, MIX_WIDTH, D_MODEL), f32) * MIX_WIDTH ** -0.5
    final_norm_w = 1.0 + 0.02 * jax.random.normal(ks[10], (D_MODEL,), f32)
    return {'x': x, 'norm_w': norm_w, 'w_in': w_in, 'attn_sink': attn_sink,
            'rel_bias': rel_bias, 'attn_out_norm_w': attn_out_norm_w,
            'ret_decay_fwd': ret_decay_fwd, 'ret_decay_bwd': ret_decay_bwd,
            'ret_gn_w': ret_gn_w, 'w_out': w_out, 'final_norm_w': final_norm_w}


def reference(x, norm_w, w_in, attn_sink, rel_bias, attn_out_norm_w,
              ret_decay_fwd, ret_decay_bwd, ret_gn_w, w_out, final_norm_w):
    B, S = x.shape[0], x.shape[1]
    points = [int(p) for p in np.cumsum(SPLIT_SIZES)[:-1]]
    for l in range(DEPTH):
        h = rmsnorm(x, norm_w[l])
        proj = jnp.einsum('bsd,de->bse', h, w_in[l])
        q_a, k_a, v_a, g_a, q_r, k_r, v_r, g_r = jnp.split(proj, points, axis=-1)

        attn = windowed_attention(q_a.reshape(B, S, ATTN_HEADS, ATTN_HEAD_DIM),
                                  k_a.reshape(B, S, ATTN_KV_HEADS, ATTN_HEAD_DIM),
                                  v_a.reshape(B, S, ATTN_KV_HEADS, ATTN_HEAD_DIM),
                                  attn_sink[l], rel_bias)
        attn = rmsnorm(attn, attn_out_norm_w[l]) * jax.nn.silu(g_a)

        qr = rotary(q_r.reshape(B, S, RET_HEADS, RET_KEY_DIM))
        kr = rotary(k_r.reshape(B, S, RET_HEADS, RET_KEY_DIM)) * RET_KEY_DIM ** -0.5
        vr = v_r.reshape(B, S, RET_HEADS, RET_VAL_DIM).astype(jnp.float32)
        ret = bidirectional_retention(qr, kr, vr, ret_decay_fwd[l], ret_decay_bwd[l])
        ret = head_groupnorm(ret, ret_gn_w[l]).astype(x.dtype) * jax.nn.silu(g_r)

        mixed = jnp.concatenate([attn.astype(x.dtype), ret], axis=-1)
        x = x + jnp.einsum('bse,ed->bsd', mixed, w_out[l])
    return rmsnorm(x, final_norm_w)
```

```python
import math

import numpy as np
import jax
import jax.numpy as jnp
from jax import lax
from jax.experimental import pallas as pl
from jax.experimental.pallas import tpu as pltpu

D_MODEL = 2048
ATTN_HEAD_DIM = 64
ATTN_WIDTH = 1024
ATTN_HEADS = 16
ATTN_KV_HEADS = 4
ATTN_GROUP = 4
ATTN_KV_COLS = 256
WINDOW = 128
BLOCK = 128
N_BUCKETS = 32
MAX_DISTANCE = 128
RET_WIDTH = 1024
RET_HEADS = 8
RET_VAL_DIM = 128
RET_KEY_DIM = 64
RET_QK_COLS = 512
CHUNK = 128
ROPE_BASE = 10000.0
EPS = 1e-6
NEG = -1e30

LANES = 128
N_PAIRS = RET_HEADS // 2
KW = 3 * BLOCK

OFF_QA, OFF_GA, OFF_VR, OFF_GR = 0, 1024, 2048, 3072
OFF_KVA, OFF_QR, OFF_KR = 4096, 4608, 5120
IN_WIDTH = 5632

VMEM_LIMIT = 56 * 1024 * 1024

F32 = jnp.float32
BF16 = jnp.bfloat16


def _inproj_kernel(x_ref, nw_ref, w_ref, o_ref, h_ref):
    @pl.when(pl.program_id(1) == 0)
    def _():
        x = x_ref[...]
        ms = jnp.mean(x * x, axis=-1, keepdims=True)
        h_ref[...] = (x * lax.rsqrt(ms + EPS) * nw_ref[...]).astype(BF16)

    o_ref[...] = jnp.dot(h_ref[...], w_ref[...],
                         preferred_element_type=F32).astype(o_ref.dtype)


def _inproj(x2, norm_w, w_in_p, tm=1024, tn=512):
    m = x2.shape[0]
    return pl.pallas_call(
        _inproj_kernel,
        grid=(m // tm, IN_WIDTH // tn),
        in_specs=[
            pl.BlockSpec((tm, D_MODEL), lambda i, j: (i, 0)),
            pl.BlockSpec((1, D_MODEL), lambda i, j: (0, 0)),
            pl.BlockSpec((D_MODEL, tn), lambda i, j: (0, j)),
        ],
        out_specs=pl.BlockSpec((tm, tn), lambda i, j: (i, j)),
        out_shape=jax.ShapeDtypeStruct((m, IN_WIDTH), BF16),
        scratch_shapes=[pltpu.VMEM((tm, D_MODEL), BF16)],
        compiler_params=pltpu.CompilerParams(
            dimension_semantics=("arbitrary", "arbitrary"),
            vmem_limit_bytes=VMEM_LIMIT),
        name="inproj",
    )(x2, norm_w, w_in_p)


def _silu(g):
    return g / (1.0 + jnp.exp(-g))


def _attn_kernel(sink_ref, q_ref, g_ref, kvp_ref, kvc_ref, kvn_ref, bias_ref,
                 nw_ref, o_ref):
    i = pl.program_id(1)
    nb = pl.num_programs(1)
    k_lo = jnp.where(i > 0, 0, BLOCK)
    k_hi = jnp.where(i < nb - 1, KW, 2 * BLOCK)
    kv = jnp.concatenate([kvp_ref[...], kvc_ref[...], kvn_ref[...]], axis=0)

    lane = lax.broadcasted_iota(jnp.int32, (KW, LANES), 1)
    low_half = lane < ATTN_HEAD_DIM
    qi = lax.broadcasted_iota(jnp.int32, (BLOCK, KW), 0)
    kt = lax.broadcasted_iota(jnp.int32, (BLOCK, KW), 1)
    rel = kt - BLOCK - qi
    valid = (jnp.abs(rel) <= WINDOW) & (kt >= k_lo) & (kt < k_hi)

    cols = []
    for c in range(ATTN_WIDTH // LANES):
        j = c // ATTN_GROUP
        kj = kv[:, LANES * j:LANES * (j + 1)]
        vj = kv[:, ATTN_KV_COLS + LANES * j:ATTN_KV_COLS + LANES * (j + 1)]
        q = q_ref[:, LANES * c:LANES * (c + 1)]
        acc = None
        for half in range(2):
            sel = low_half if half == 0 else jnp.logical_not(low_half)
            kh = jnp.where(sel, kj, jnp.zeros_like(kj))
            vh = jnp.where(sel, vj, jnp.zeros_like(vj))
            hn = 2 * c + half
            s = lax.dot_general(q, kh, (((1,), (1,)), ((), ())),
                                preferred_element_type=F32)
            t = jnp.where(valid, s + bias_ref[hn], NEG)
            sk = sink_ref[hn]
            m = jnp.maximum(jnp.max(t, axis=-1, keepdims=True), sk)
            p = jnp.exp(t - m)
            denom = jnp.sum(p, axis=-1, keepdims=True) + jnp.exp(sk - m)
            o = jnp.dot(p.astype(BF16), vh, preferred_element_type=F32)
            o = o / denom
            acc = o if acc is None else acc + o
        cols.append(acc)
    attn = jnp.concatenate(cols, axis=-1)
    ms = jnp.mean(attn * attn, axis=-1, keepdims=True)
    a = attn * lax.rsqrt(ms + EPS) * nw_ref[...]
    g = g_ref[...].astype(F32)
    o_ref[...] = (a * _silu(g)).astype(o_ref.dtype)


def _attention(proj, sink_p, bias_t, attn_nw_p, batch, seq):
    nb = seq // BLOCK
    row = lambda b, i: b * nb + i
    kvblk = OFF_KVA // 512
    return pl.pallas_call(
        _attn_kernel,
        grid=(batch, nb),
        in_specs=[
            pl.BlockSpec(memory_space=pltpu.SMEM),
            pl.BlockSpec((BLOCK, ATTN_WIDTH), lambda b, i: (row(b, i), OFF_QA // 1024)),
            pl.BlockSpec((BLOCK, ATTN_WIDTH), lambda b, i: (row(b, i), OFF_GA // 1024)),
            pl.BlockSpec((BLOCK, 512), lambda b, i: (row(b, jnp.maximum(i - 1, 0)), kvblk)),
            pl.BlockSpec((BLOCK, 512), lambda b, i: (row(b, i), kvblk)),
            pl.BlockSpec((BLOCK, 512), lambda b, i: (row(b, jnp.minimum(i + 1, nb - 1)), kvblk)),
            pl.BlockSpec((ATTN_HEADS, BLOCK, KW), lambda b, i: (0, 0, 0)),
            pl.BlockSpec((1, ATTN_WIDTH), lambda b, i: (0, 0)),
        ],
        out_specs=pl.BlockSpec((BLOCK, ATTN_WIDTH), lambda b, i: (row(b, i), 0)),
        out_shape=jax.ShapeDtypeStruct((batch * seq, ATTN_WIDTH), BF16),
        compiler_params=pltpu.CompilerParams(
            dimension_semantics=("arbitrary", "arbitrary"),
            vmem_limit_bytes=VMEM_LIMIT),
        name="attn",
    )(sink_p, proj, proj, proj, proj, proj, bias_t, attn_nw_p)


def _rotary(x, cos, sin_signed):
    lane = lax.broadcasted_iota(jnp.int32, x.shape, 1)
    first = (lane % RET_KEY_DIM) < (RET_KEY_DIM // 2)
    partner = jnp.where(first,
                        pltpu.roll(x, LANES - RET_KEY_DIM // 2, axis=1),
                        pltpu.roll(x, RET_KEY_DIM // 2, axis=1))
    return x * cos + partner * sin_signed


def _state_mask():
    r = lax.broadcasted_iota(jnp.int32, (LANES, 2 * RET_VAL_DIM), 0)
    m = lax.broadcasted_iota(jnp.int32, (LANES, 2 * RET_VAL_DIM), 1)
    return (r // RET_KEY_DIM) == (m // RET_VAL_DIM)


def _state_update(st_ref, c, k_rot, kdec, gdec, vpair):
    kd = (k_rot * kdec).astype(BF16)
    upd = lax.dot_general(kd, vpair, (((0,), (0,)), ((), ())),
                          preferred_element_type=F32)
    st_ref[c] = st_ref[c] * gdec + jnp.where(_state_mask(), upd, 0.0)


def _retstate_kernel(k_ref, v_ref, cos_ref, sin_ref, kdb_ref, gb_ref, pb_ref, st_ref):
    @pl.when(pl.program_id(1) == 0)
    def _():
        st_ref[...] = jnp.zeros_like(st_ref)

    pb_ref[0] = st_ref[...].astype(BF16)
    cos = cos_ref[...]
    sin = sin_ref[...]
    for c in range(N_PAIRS):
        k = _rotary(k_ref[:, LANES * c:LANES * (c + 1)].astype(F32), cos, sin)
        vpair = v_ref[:, 2 * RET_VAL_DIM * c:2 * RET_VAL_DIM * (c + 1)]
        _state_update(st_ref, c, k, kdb_ref[c], gb_ref[c], vpair)


def _retention_states(proj, cos_t, sin_t, kdec_b, g_b, batch, seq):
    nc = seq // CHUNK
    rrow = lambda b, i: b * nc + (nc - 1 - i)
    return pl.pallas_call(
        _retstate_kernel,
        grid=(batch, nc),
        in_specs=[
            pl.BlockSpec((CHUNK, RET_QK_COLS), lambda b, i: (rrow(b, i), OFF_KR // 512)),
            pl.BlockSpec((CHUNK, RET_WIDTH), lambda b, i: (rrow(b, i), OFF_VR // 1024)),
            pl.BlockSpec((CHUNK, LANES), lambda b, i: (nc - 1 - i, 0)),
            pl.BlockSpec((CHUNK, LANES), lambda b, i: (nc - 1 - i, 0)),
            pl.BlockSpec((N_PAIRS, CHUNK, LANES), lambda b, i: (0, 0, 0)),
            pl.BlockSpec((N_PAIRS, LANES, 2 * RET_VAL_DIM), lambda b, i: (0, 0, 0)),
        ],
        out_specs=pl.BlockSpec((1, N_PAIRS, LANES, 2 * RET_VAL_DIM),
                               lambda b, i: (rrow(b, i), 0, 0, 0)),
        out_shape=jax.ShapeDtypeStruct((batch * nc, N_PAIRS, LANES, 2 * RET_VAL_DIM), BF16),
        scratch_shapes=[pltpu.VMEM((N_PAIRS, LANES, 2 * RET_VAL_DIM), F32)],
        compiler_params=pltpu.CompilerParams(
            dimension_semantics=("arbitrary", "arbitrary"),
            vmem_limit_bytes=VMEM_LIMIT),
        name="ret_states",
    )(proj, proj, cos_t, sin_t, kdec_b, g_b)


def _ret_kernel(q_ref, k_ref, v_ref, g_ref, pb_ref, cos_ref, sin_ref, d_ref,
                qdf_ref, qdb_ref, kdf_ref, gf_ref, gnw_ref, o_ref, st_ref):
    @pl.when(pl.program_id(1) == 0)
    def _():
        st_ref[...] = jnp.zeros_like(st_ref)

    cos = cos_ref[...]
    sin = sin_ref[...]
    lane = lax.broadcasted_iota(jnp.int32, (CHUNK, LANES), 1)
    low_half = lane < RET_KEY_DIM
    vcol = lax.broadcasted_iota(jnp.int32, (CHUNK, 2 * RET_VAL_DIM), 1)
    first_v = vcol < RET_VAL_DIM
    for c in range(N_PAIRS):
        q = _rotary(q_ref[:, LANES * c:LANES * (c + 1)].astype(F32), cos, sin)
        k = _rotary(k_ref[:, LANES * c:LANES * (c + 1)].astype(F32), cos, sin)
        kb = k.astype(BF16)
        vpair = v_ref[:, 2 * RET_VAL_DIM * c:2 * RET_VAL_DIM * (c + 1)]
        parts = []
        for half in range(2):
            sel = low_half if half == 0 else jnp.logical_not(low_half)
            qh = jnp.where(sel, q, 0.0).astype(BF16)
            s = lax.dot_general(qh, kb, (((1,), (1,)), ((), ())),
                                preferred_element_type=F32)
            parts.append((s * d_ref[2 * c + half]).astype(BF16))
        parts.append((q * qdf_ref[c]).astype(BF16))
        parts.append((q * qdb_ref[c]).astype(BF16))
        lhs = jnp.concatenate(parts, axis=1)
        zero_v = jnp.zeros_like(vpair)
        rhs = jnp.concatenate([
            jnp.where(first_v, vpair, zero_v),
            jnp.where(first_v, zero_v, vpair),
            st_ref[c].astype(BF16),
            pb_ref[0, c],
        ], axis=0)
        o = jnp.dot(lhs, rhs, preferred_element_type=F32)
        _state_update(st_ref, c, k, kdf_ref[c], gf_ref[c], vpair)
        for half in range(2):
            lo_c = RET_VAL_DIM * (2 * c + half)
            oh = o[:, RET_VAL_DIM * half:RET_VAL_DIM * (half + 1)]
            mu = jnp.mean(oh, axis=-1, keepdims=True)
            dev = oh - mu
            var = jnp.mean(dev * dev, axis=-1, keepdims=True)
            y = dev * lax.rsqrt(var + EPS) * gnw_ref[:, lo_c:lo_c + RET_VAL_DIM]
            g = g_ref[:, lo_c:lo_c + RET_VAL_DIM].astype(F32)
            o_ref[:, lo_c:lo_c + RET_VAL_DIM] = (y * _silu(g)).astype(o_ref.dtype)


def _retention(proj, pb, cos_t, sin_t, d_t, qdec_f, qdec_b, kdec_f, g_f, gn_w, batch, seq):
    nc = seq // CHUNK
    row = lambda b, i: b * nc + i
    const3 = lambda b, i: (0, 0, 0)
    return pl.pallas_call(
        _ret_kernel,
        grid=(batch, nc),
        in_specs=[
            pl.BlockSpec((CHUNK, RET_QK_COLS), lambda b, i: (row(b, i), OFF_QR // 512)),
            pl.BlockSpec((CHUNK, RET_QK_COLS), lambda b, i: (row(b, i), OFF_KR // 512)),
            pl.BlockSpec((CHUNK, RET_WIDTH), lambda b, i: (row(b, i), OFF_VR // 1024)),
            pl.BlockSpec((CHUNK, RET_WIDTH), lambda b, i: (row(b, i), OFF_GR // 1024)),
            pl.BlockSpec((1, N_PAIRS, LANES, 2 * RET_VAL_DIM), lambda b, i: (row(b, i), 0, 0, 0)),
            pl.BlockSpec((CHUNK, LANES), lambda b, i: (i, 0)),
            pl.BlockSpec((CHUNK, LANES), lambda b, i: (i, 0)),
            pl.BlockSpec((RET_HEADS, CHUNK, CHUNK), const3),
            pl.BlockSpec((N_PAIRS, CHUNK, LANES), const3),
            pl.BlockSpec((N_PAIRS, CHUNK, LANES), const3),
            pl.BlockSpec((N_PAIRS, CHUNK, LANES), const3),
            pl.BlockSpec((N_PAIRS, LANES, 2 * RET_VAL_DIM), const3),
            pl.BlockSpec((1, RET_WIDTH), lambda b, i: (0, 0)),
        ],
        out_specs=pl.BlockSpec((CHUNK, RET_WIDTH), lambda b, i: (row(b, i), 0)),
        out_shape=jax.ShapeDtypeStruct((batch * seq, RET_WIDTH), BF16),
        scratch_shapes=[pltpu.VMEM((N_PAIRS, LANES, 2 * RET_VAL_DIM), F32)],
        compiler_params=pltpu.CompilerParams(
            dimension_semantics=("arbitrary", "arbitrary"),
            vmem_limit_bytes=VMEM_LIMIT),
        name="retention",
    )(proj, proj, proj, proj, pb, cos_t, sin_t, d_t, qdec_f, qdec_b, kdec_f, g_f, gn_w)


def _outproj_kernel(a_ref, r_ref, x_ref, wa_ref, wr_ref, fw_ref, o_ref):
    y = (x_ref[...]
         + jnp.dot(a_ref[...], wa_ref[...], preferred_element_type=F32)
         + jnp.dot(r_ref[...], wr_ref[...], preferred_element_type=F32))
    ms = jnp.mean(y * y, axis=-1, keepdims=True)
    o_ref[...] = y * lax.rsqrt(ms + EPS) * fw_ref[...]


def _outproj(attn_o, ret_o, x2, w_a, w_r, final_w, tm=512):
    m = x2.shape[0]
    return pl.pallas_call(
        _outproj_kernel,
        grid=(m // tm,),
        in_specs=[
            pl.BlockSpec((tm, ATTN_WIDTH), lambda i: (i, 0)),
            pl.BlockSpec((tm, RET_WIDTH), lambda i: (i, 0)),
            pl.BlockSpec((tm, D_MODEL), lambda i: (i, 0)),
            pl.BlockSpec((ATTN_WIDTH, D_MODEL), lambda i: (0, 0)),
            pl.BlockSpec((RET_WIDTH, D_MODEL), lambda i: (0, 0)),
            pl.BlockSpec((1, D_MODEL), lambda i: (0, 0)),
        ],
        out_specs=pl.BlockSpec((tm, D_MODEL), lambda i: (i, 0)),
        out_shape=jax.ShapeDtypeStruct((m, D_MODEL), F32),
        compiler_params=pltpu.CompilerParams(
            dimension_semantics=("arbitrary",),
            vmem_limit_bytes=VMEM_LIMIT),
        name="outproj",
    )(attn_o, ret_o, x2, w_a, w_r, final_w)


def _attn_head_order():
    order = []
    for c in range(ATTN_WIDTH // LANES):
        for half in range(2):
            g = 2 * (c // ATTN_GROUP) + half
            order.append(ATTN_GROUP * g + c % ATTN_GROUP)
    return np.array(order, dtype=np.int32)


def _attn_col_perm():
    heads = _attn_head_order()
    return (heads[:, None] * ATTN_HEAD_DIM + np.arange(ATTN_HEAD_DIM)[None, :]).reshape(-1)


def _ret_col_perm():
    half = RET_KEY_DIM // 2
    h = np.arange(RET_HEADS)[:, None, None]
    p = np.arange(2)[None, :, None]
    t = np.arange(half)[None, None, :]
    return (h * RET_KEY_DIM + 2 * t + p).reshape(-1).astype(np.int32)


def _t5_bucket(rel):
    nb = N_BUCKETS // 2
    max_exact = nb // 2
    ret = jnp.where(rel > 0, nb, 0)
    n = jnp.abs(rel)
    nf = jnp.maximum(n, 1).astype(F32)
    large = max_exact + (jnp.log(nf / max_exact) / math.log(MAX_DISTANCE / max_exact)
                         * (nb - max_exact)).astype(jnp.int32)
    large = jnp.minimum(large, nb - 1)
    return ret + jnp.where(n < max_exact, n, large)


def kernel(x, norm_w, w_in, attn_sink, rel_bias, attn_out_norm_w, ret_decay_fwd,
           ret_decay_bwd, ret_gn_w, w_out, final_norm_w):
    batch, seq, _ = x.shape
    assert norm_w.shape[0] == 1 and seq % BLOCK == 0
    x2 = x.reshape(batch * seq, D_MODEL)

    qa_perm = _attn_col_perm()
    r_perm = _ret_col_perm()
    w = w_in[0]
    o = np.cumsum([0, ATTN_WIDTH, ATTN_KV_COLS, ATTN_KV_COLS, ATTN_WIDTH,
                   RET_QK_COLS, RET_QK_COLS, RET_WIDTH, RET_WIDTH])
    w_qa, w_ka, w_va, w_ga, w_qr, w_kr, w_vr, w_gr = [w[:, o[t]:o[t + 1]] for t in range(8)]
    w_in_p = jnp.concatenate([
        w_qa[:, qa_perm] * (ATTN_HEAD_DIM ** -0.5),
        w_ga[:, qa_perm], w_vr, w_gr, w_ka, w_va,
        w_qr[:, r_perm],
        w_kr[:, r_perm] * (RET_KEY_DIM ** -0.5),
    ], axis=1).astype(BF16)
    w_o = w_out[0]
    w_o_a = w_o[:ATTN_WIDTH][qa_perm].astype(BF16)
    w_o_r = w_o[ATTN_WIDTH:].astype(BF16)
    attn_nw_p = attn_out_norm_w[0][qa_perm].reshape(1, ATTN_WIDTH)

    heads = _attn_head_order()
    qi = jnp.arange(BLOCK, dtype=jnp.int32)[:, None]
    kt = jnp.arange(KW, dtype=jnp.int32)[None, :]
    bias = rel_bias.astype(F32)[_t5_bucket(kt - BLOCK - qi)]
    bias_t = jnp.moveaxis(bias, -1, 0)[heads]
    sink_p = attn_sink[0].astype(F32)[heads]

    half = RET_KEY_DIM // 2
    inv = ROPE_BASE ** (-jnp.arange(0, RET_KEY_DIM, 2, dtype=F32) / RET_KEY_DIM)
    ang = jnp.arange(seq, dtype=F32)[:, None] * inv[None, :]
    cos_t = jnp.tile(jnp.cos(ang), (1, LANES // half))
    sgn = jnp.tile(jnp.concatenate([-jnp.ones((half,), F32), jnp.ones((half,), F32)]),
                   LANES // RET_KEY_DIM)
    sin_t = jnp.tile(jnp.sin(ang), (1, LANES // half)) * sgn[None, :]

    lg_f = jax.nn.log_sigmoid(ret_decay_fwd[0].astype(F32))
    lg_b = jax.nn.log_sigmoid(ret_decay_bwd[0].astype(F32))
    idx = jnp.arange(CHUNK, dtype=F32)
    diff = idx[:, None] - idx[None, :]
    d_f = jnp.where((diff >= 0)[None], jnp.exp(lg_f[:, None, None] * jnp.maximum(diff, 0.0)[None]), 0.0)
    d_b = jnp.where((diff < 0)[None], jnp.exp(lg_b[:, None, None] * jnp.maximum(-diff, 0.0)[None]), 0.0)
    d_t = d_f + d_b

    def lane_table(lg, expo):
        per_head = jnp.exp(lg[None, :] * expo[:, None])
        t = jnp.repeat(per_head, RET_KEY_DIM, axis=1)
        return t.reshape(CHUNK, N_PAIRS, LANES).transpose(1, 0, 2)

    qdec_f = lane_table(lg_f, idx + 1.0)
    qdec_b = lane_table(lg_b, CHUNK - idx)
    kdec_f = lane_table(lg_f, CHUNK - 1.0 - idx)
    kdec_b = lane_table(lg_b, idx)

    def state_decay(lg):
        cd = jnp.exp(lg * CHUNK).reshape(N_PAIRS, 2)
        rows = jnp.repeat(cd, RET_KEY_DIM, axis=1)
        r = np.arange(LANES)[:, None] // RET_KEY_DIM
        m = np.arange(2 * RET_VAL_DIM)[None, :] // RET_VAL_DIM
        return jnp.where(jnp.asarray(r == m)[None], rows[:, :, None], 0.0)

    g_f = state_decay(lg_f)
    g_b = state_decay(lg_b)

    proj = _inproj(x2, norm_w[0].reshape(1, D_MODEL), w_in_p)
    attn_o = _attention(proj, sink_p, bias_t, attn_nw_p, batch, seq)
    pb = _retention_states(proj, cos_t, sin_t, kdec_b, g_b, batch, seq)
    ret_o = _retention(proj, pb, cos_t, sin_t, d_t, qdec_f, qdec_b, kdec_f, g_f,
                       ret_gn_w[0].reshape(1, RET_WIDTH), batch, seq)
    out = _outproj(attn_o, ret_o, x2, w_o_a, w_o_r, final_norm_w.reshape(1, D_MODEL))
    return out.reshape(batch, seq, D_MODEL)
```

```python
import math

import numpy as np
import jax
import jax.numpy as jnp
from jax import lax
from jax.experimental import pallas as pl
from jax.experimental.pallas import tpu as pltpu

D_MODEL = 2048
ATTN_HEAD_DIM = 64
ATTN_WIDTH = 1024
ATTN_HEADS = 16
ATTN_KV_HEADS = 4
ATTN_GROUP = 4
ATTN_KV_COLS = 256
WINDOW = 128
BLOCK = 128
N_BUCKETS = 32
MAX_DISTANCE = 128
RET_WIDTH = 1024
RET_HEADS = 8
RET_VAL_DIM = 128
RET_KEY_DIM = 64
RET_QK_COLS = 512
CHUNK = 128
ROPE_BASE = 10000.0
EPS = 1e-6
NEG = -1e30

LANES = 128
N_PAIRS = RET_HEADS // 2
KW = 3 * BLOCK

OFF_QA, OFF_GA, OFF_VR, OFF_GR = 0, 1024, 2048, 3072
OFF_KVA, OFF_QR, OFF_KR = 4096, 4608, 5120
IN_WIDTH = 5632

VMEM_LIMIT = 56 * 1024 * 1024

F32 = jnp.float32
BF16 = jnp.bfloat16


def _inproj_kernel(x_ref, nw_ref, w_ref, o_ref, h_ref):
    @pl.when(pl.program_id(1) == 0)
    def _():
        x = x_ref[...]
        ms = jnp.mean(x * x, axis=-1, keepdims=True)
        h_ref[...] = (x * lax.rsqrt(ms + EPS) * nw_ref[...]).astype(BF16)

    o_ref[...] = jnp.dot(h_ref[...], w_ref[...],
                         preferred_element_type=F32).astype(o_ref.dtype)


def _inproj(x2, norm_w, w_in_p, tm=1024, tn=512):
    m = x2.shape[0]
    return pl.pallas_call(
        _inproj_kernel,
        grid=(m // tm, IN_WIDTH // tn),
        in_specs=[
            pl.BlockSpec((tm, D_MODEL), lambda i, j: (i, 0)),
            pl.BlockSpec((1, D_MODEL), lambda i, j: (0, 0)),
            pl.BlockSpec((D_MODEL, tn), lambda i, j: (0, j)),
        ],
        out_specs=pl.BlockSpec((tm, tn), lambda i, j: (i, j)),
        out_shape=jax.ShapeDtypeStruct((m, IN_WIDTH), BF16),
        scratch_shapes=[pltpu.VMEM((tm, D_MODEL), BF16)],
        compiler_params=pltpu.CompilerParams(
            dimension_semantics=("arbitrary", "arbitrary"),
            vmem_limit_bytes=VMEM_LIMIT),
        name="inproj",
    )(x2, norm_w, w_in_p)


def _silu(g):
    return g / (1.0 + jnp.exp(-g))


def _attn_kernel(q_ref, kvp_ref, kvc_ref, kvn_ref, bias_ref, sink_ref, o_ref):
    i = pl.program_id(1)
    nb = pl.num_programs(1)
    thr_l = jnp.where(i > 0, 0.5 * NEG, -NEG).astype(F32)
    thr_r = jnp.where(i < nb - 1, 0.5 * NEG, -NEG).astype(F32)
    kv = jnp.concatenate([kvp_ref[...], kvc_ref[...], kvn_ref[...]], axis=0)
    lane = lax.broadcasted_iota(jnp.int32, (KW, LANES), 1)
    low_half = lane < ATTN_HEAD_DIM
    hd = ATTN_HEAD_DIM

    for j in range(ATTN_KV_COLS // LANES):
        kj = kv[:, LANES * j:LANES * (j + 1)]
        vj = kv[:, ATTN_KV_COLS + LANES * j:ATTN_KV_COLS + LANES * (j + 1)]
        qs = jnp.concatenate(
            [q_ref[:, LANES * c:LANES * (c + 1)]
             for c in range(ATTN_GROUP * j, ATTN_GROUP * (j + 1))], axis=0)
        halves = []
        for half in range(2):
            sel = low_half if half == 0 else jnp.logical_not(low_half)
            kh = jnp.where(sel, kj, jnp.zeros_like(kj))
            vh = jnp.where(sel, vj, jnp.ones_like(vj))
            st = lax.dot_general(kh, qs, (((1,), (1,)), ((), ())),
                                 preferred_element_type=F32)
            b = bias_ref[2 * j + half]
            t = jnp.concatenate([
                jnp.where(b[:BLOCK] > thr_l, st[:BLOCK] + b[:BLOCK], NEG),
                st[BLOCK:2 * BLOCK] + b[BLOCK:2 * BLOCK],
                jnp.where(b[2 * BLOCK:] > thr_r, st[2 * BLOCK:] + b[2 * BLOCK:], NEG),
            ], axis=0)
            sk = sink_ref[2 * j + half]
            m = jnp.maximum(jnp.max(t, axis=0, keepdims=True), sk)
            p = jnp.exp(t - m).astype(BF16)
            ot = lax.dot_general(vh, p, (((0,), (0,)), ((), ())),
                                 preferred_element_type=F32)
            if half == 0:
                num, den = ot[:hd], ot[hd:hd + 1]
            else:
                num, den = ot[hd:], ot[0:1]
            halves.append(num / (den + jnp.exp(sk - m)))
        out_t = jnp.concatenate(halves, axis=0)
        for r in range(ATTN_GROUP):
            c = ATTN_GROUP * j + r
            o_ref[:, LANES * c:LANES * (c + 1)] = (
                out_t[:, BLOCK * r:BLOCK * (r + 1)].T.astype(o_ref.dtype))


def _attention(proj, bias_t, sink_t, batch, seq):
    nb = seq // BLOCK
    row = lambda b, i: b * nb + i
    kvblk = OFF_KVA // 512
    n_kv = ATTN_KV_HEADS
    return pl.pallas_call(
        _attn_kernel,
        grid=(batch, nb),
        in_specs=[
            pl.BlockSpec((BLOCK, ATTN_WIDTH), lambda b, i: (row(b, i), OFF_QA // 1024)),
            pl.BlockSpec((BLOCK, 512), lambda b, i: (row(b, jnp.maximum(i - 1, 0)), kvblk)),
            pl.BlockSpec((BLOCK, 512), lambda b, i: (row(b, i), kvblk)),
            pl.BlockSpec((BLOCK, 512), lambda b, i: (row(b, jnp.minimum(i + 1, nb - 1)), kvblk)),
            pl.BlockSpec((n_kv, KW, ATTN_GROUP * BLOCK), lambda b, i: (0, 0, 0)),
            pl.BlockSpec((n_kv, 1, ATTN_GROUP * BLOCK), lambda b, i: (0, 0, 0)),
        ],
        out_specs=pl.BlockSpec((BLOCK, ATTN_WIDTH), lambda b, i: (row(b, i), 0)),
        out_shape=jax.ShapeDtypeStruct((batch * seq, ATTN_WIDTH), BF16),
        compiler_params=pltpu.CompilerParams(
            dimension_semantics=("arbitrary", "arbitrary"),
            vmem_limit_bytes=VMEM_LIMIT),
        name="attn",
    )(proj, proj, proj, proj, bias_t, sink_t)


def _rotary(x, cos, sin_signed):
    lane = lax.broadcasted_iota(jnp.int32, x.shape, 1)
    first = (lane % RET_KEY_DIM) < (RET_KEY_DIM // 2)
    partner = jnp.where(first,
                        pltpu.roll(x, LANES - RET_KEY_DIM // 2, axis=1),
                        pltpu.roll(x, RET_KEY_DIM // 2, axis=1))
    return x * cos + partner * sin_signed


def _state_mask():
    r = lax.broadcasted_iota(jnp.int32, (LANES, 2 * RET_VAL_DIM), 0)
    m = lax.broadcasted_iota(jnp.int32, (LANES, 2 * RET_VAL_DIM), 1)
    return (r // RET_KEY_DIM) == (m // RET_VAL_DIM)


def _state_update(st_ref, c, k_rot, kdec, gdec, vpair):
    kd = (k_rot * kdec).astype(BF16)
    upd = lax.dot_general(kd, vpair, (((0,), (0,)), ((), ())),
                          preferred_element_type=F32)
    st_ref[c] = st_ref[c] * gdec + jnp.where(_state_mask(), upd, 0.0)


def _retstate_kernel(k_ref, v_ref, cos_ref, sin_ref, kdb_ref, gb_ref, pb_ref, st_ref):
    @pl.when(pl.program_id(1) == 0)
    def _():
        st_ref[...] = jnp.zeros_like(st_ref)

    pb_ref[0] = st_ref[...].astype(BF16)
    cos = cos_ref[...]
    sin = sin_ref[...]
    for c in range(N_PAIRS):
        k = _rotary(k_ref[:, LANES * c:LANES * (c + 1)].astype(F32), cos, sin)
        vpair = v_ref[:, 2 * RET_VAL_DIM * c:2 * RET_VAL_DIM * (c + 1)]
        _state_update(st_ref, c, k, kdb_ref[c], gb_ref[c], vpair)


def _retention_states(proj, cos_t, sin_t, kdec_b, g_b, batch, seq):
    nc = seq // CHUNK
    rrow = lambda b, i: b * nc + (nc - 1 - i)
    return pl.pallas_call(
        _retstate_kernel,
        grid=(batch, nc),
        in_specs=[
            pl.BlockSpec((CHUNK, RET_QK_COLS), lambda b, i: (rrow(b, i), OFF_KR // 512)),
            pl.BlockSpec((CHUNK, RET_WIDTH), lambda b, i: (rrow(b, i), OFF_VR // 1024)),
            pl.BlockSpec((CHUNK, LANES), lambda b, i: (nc - 1 - i, 0)),
            pl.BlockSpec((CHUNK, LANES), lambda b, i: (nc - 1 - i, 0)),
            pl.BlockSpec((N_PAIRS, CHUNK, LANES), lambda b, i: (0, 0, 0)),
            pl.BlockSpec((N_PAIRS, LANES, 2 * RET_VAL_DIM), lambda b, i: (0, 0, 0)),
        ],
        out_specs=pl.BlockSpec((1, N_PAIRS, LANES, 2 * RET_VAL_DIM),
                               lambda b, i: (rrow(b, i), 0, 0, 0)),
        out_shape=jax.ShapeDtypeStruct((batch * nc, N_PAIRS, LANES, 2 * RET_VAL_DIM), BF16),
        scratch_shapes=[pltpu.VMEM((N_PAIRS, LANES, 2 * RET_VAL_DIM), F32)],
        compiler_params=pltpu.CompilerParams(
            dimension_semantics=("arbitrary", "arbitrary"),
            vmem_limit_bytes=VMEM_LIMIT),
        name="ret_states",
    )(proj, proj, cos_t, sin_t, kdec_b, g_b)


def _ret_kernel(q_ref, k_ref, v_ref, g_ref, pb_ref, cos_ref, sin_ref, d_ref,
                qdf_ref, qdb_ref, kdf_ref, gf_ref, gnw_ref, o_ref, st_ref):
    @pl.when(pl.program_id(1) == 0)
    def _():
        st_ref[...] = jnp.zeros_like(st_ref)

    cos = cos_ref[...]
    sin = sin_ref[...]
    lane = lax.broadcasted_iota(jnp.int32, (CHUNK, LANES), 1)
    low_half = lane < RET_KEY_DIM
    vcol = lax.broadcasted_iota(jnp.int32, (CHUNK, 2 * RET_VAL_DIM), 1)
    first_v = vcol < RET_VAL_DIM
    for c in range(N_PAIRS):
        q = _rotary(q_ref[:, LANES * c:LANES * (c + 1)].astype(F32), cos, sin)
        k = _rotary(k_ref[:, LANES * c:LANES * (c + 1)].astype(F32), cos, sin)
        kb = k.astype(BF16)
        vpair = v_ref[:, 2 * RET_VAL_DIM * c:2 * RET_VAL_DIM * (c + 1)]
        parts = []
        for half in range(2):
            sel = low_half if half == 0 else jnp.logical_not(low_half)
            qh = jnp.where(sel, q, 0.0).astype(BF16)
            s = lax.dot_general(qh, kb, (((1,), (1,)), ((), ())),
                                preferred_element_type=F32)
            parts.append((s * d_ref[2 * c + half]).astype(BF16))
        parts.append((q * qdf_ref[c]).astype(BF16))
        parts.append((q * qdb_ref[c]).astype(BF16))
        lhs = jnp.concatenate(parts, axis=1)
        zero_v = jnp.zeros_like(vpair)
        rhs = jnp.concatenate([
            jnp.where(first_v, vpair, zero_v),
            jnp.where(first_v, zero_v, vpair),
            st_ref[c].astype(BF16),
            pb_ref[0, c],
        ], axis=0)
        o = jnp.dot(lhs, rhs, preferred_element_type=F32)
        _state_update(st_ref, c, k, kdf_ref[c], gf_ref[c], vpair)
        for half in range(2):
            lo_c = RET_VAL_DIM * (2 * c + half)
            oh = o[:, RET_VAL_DIM * half:RET_VAL_DIM * (half + 1)]
            mu = jnp.mean(oh, axis=-1, keepdims=True)
            dev = oh - mu
            var = jnp.mean(dev * dev, axis=-1, keepdims=True)
            y = dev * lax.rsqrt(var + EPS) * gnw_ref[:, lo_c:lo_c + RET_VAL_DIM]
            g = g_ref[:, lo_c:lo_c + RET_VAL_DIM].astype(F32)
            o_ref[:, lo_c:lo_c + RET_VAL_DIM] = (y * _silu(g)).astype(o_ref.dtype)


def _retention(proj, pb, cos_t, sin_t, d_t, qdec_f, qdec_b, kdec_f, g_f, gn_w, batch, seq):
    nc = seq // CHUNK
    row = lambda b, i: b * nc + i
    const3 = lambda b, i: (0, 0, 0)
    return pl.pallas_call(
        _ret_kernel,
        grid=(batch, nc),
        in_specs=[
            pl.BlockSpec((CHUNK, RET_QK_COLS), lambda b, i: (row(b, i), OFF_QR // 512)),
            pl.BlockSpec((CHUNK, RET_QK_COLS), lambda b, i: (row(b, i), OFF_KR // 512)),
            pl.BlockSpec((CHUNK, RET_WIDTH), lambda b, i: (row(b, i), OFF_VR // 1024)),
            pl.BlockSpec((CHUNK, RET_WIDTH), lambda b, i: (row(b, i), OFF_GR // 1024)),
            pl.BlockSpec((1, N_PAIRS, LANES, 2 * RET_VAL_DIM), lambda b, i: (row(b, i), 0, 0, 0)),
            pl.BlockSpec((CHUNK, LANES), lambda b, i: (i, 0)),
            pl.BlockSpec((CHUNK, LANES), lambda b, i: (i, 0)),
            pl.BlockSpec((RET_HEADS, CHUNK, CHUNK), const3),
            pl.BlockSpec((N_PAIRS, CHUNK, LANES), const3),
            pl.BlockSpec((N_PAIRS, CHUNK, LANES), const3),
            pl.BlockSpec((N_PAIRS, CHUNK, LANES), const3),
            pl.BlockSpec((N_PAIRS, LANES, 2 * RET_VAL_DIM), const3),
            pl.BlockSpec((1, RET_WIDTH), lambda b, i: (0, 0)),
        ],
        out_specs=pl.BlockSpec((CHUNK, RET_WIDTH), lambda b, i: (row(b, i), 0)),
        out_shape=jax.ShapeDtypeStruct((batch * seq, RET_WIDTH), BF16),
        scratch_shapes=[pltpu.VMEM((N_PAIRS, LANES, 2 * RET_VAL_DIM), F32)],
        compiler_params=pltpu.CompilerParams(
            dimension_semantics=("arbitrary", "arbitrary"),
            vmem_limit_bytes=VMEM_LIMIT),
        name="retention",
    )(proj, proj, proj, proj, pb, cos_t, sin_t, d_t, qdec_f, qdec_b, kdec_f, g_f, gn_w)


def _outproj_kernel(a_ref, g_ref, r_ref, x_ref, anw_ref, wa_ref, wr_ref, fw_ref, o_ref):
    a = a_ref[...].astype(F32)
    ms = jnp.mean(a * a, axis=-1, keepdims=True)
    a = a * lax.rsqrt(ms + EPS) * anw_ref[...]
    am = (a * _silu(g_ref[...].astype(F32))).astype(BF16)
    y = (x_ref[...]
         + jnp.dot(am, wa_ref[...], preferred_element_type=F32)
         + jnp.dot(r_ref[...], wr_ref[...], preferred_element_type=F32))
    ms = jnp.mean(y * y, axis=-1, keepdims=True)
    o_ref[...] = y * lax.rsqrt(ms + EPS) * fw_ref[...]


def _outproj(attn_o, proj, ret_o, x2, attn_nw, w_a, w_r, final_w, tm=512):
    m = x2.shape[0]
    return pl.pallas_call(
        _outproj_kernel,
        grid=(m // tm,),
        in_specs=[
            pl.BlockSpec((tm, ATTN_WIDTH), lambda i: (i, 0)),
            pl.BlockSpec((tm, ATTN_WIDTH), lambda i: (i, OFF_GA // 1024)),
            pl.BlockSpec((tm, RET_WIDTH), lambda i: (i, 0)),
            pl.BlockSpec((tm, D_MODEL), lambda i: (i, 0)),
            pl.BlockSpec((1, ATTN_WIDTH), lambda i: (0, 0)),
            pl.BlockSpec((ATTN_WIDTH, D_MODEL), lambda i: (0, 0)),
            pl.BlockSpec((RET_WIDTH, D_MODEL), lambda i: (0, 0)),
            pl.BlockSpec((1, D_MODEL), lambda i: (0, 0)),
        ],
        out_specs=pl.BlockSpec((tm, D_MODEL), lambda i: (i, 0)),
        out_shape=jax.ShapeDtypeStruct((m, D_MODEL), F32),
        compiler_params=pltpu.CompilerParams(
            dimension_semantics=("arbitrary",),
            vmem_limit_bytes=VMEM_LIMIT),
        name="outproj",
    )(attn_o, proj, ret_o, x2, attn_nw, w_a, w_r, final_w)


def _pair_heads(a, axis):
    a = jnp.moveaxis(a, axis, -1)
    lead = a.shape[:-1]
    d = a.shape[-1] // ATTN_HEADS
    a = a.reshape(lead + (2, 2, ATTN_GROUP, d))
    a = jnp.swapaxes(a, -3, -2)
    a = a.reshape(lead + (ATTN_HEADS * d,))
    return jnp.moveaxis(a, -1, axis)


def _split_even_odd(w):
    k = w.shape[0]
    w = w.reshape(k, RET_HEADS, RET_KEY_DIM // 2, 2)
    return jnp.swapaxes(w, -1, -2).reshape(k, RET_QK_COLS)


def _t5_bucket(rel):
    nb = N_BUCKETS // 2
    max_exact = nb // 2
    ret = jnp.where(rel > 0, nb, 0)
    n = jnp.abs(rel)
    nf = jnp.maximum(n, 1).astype(F32)
    large = max_exact + (jnp.log(nf / max_exact) / math.log(MAX_DISTANCE / max_exact)
                         * (nb - max_exact)).astype(jnp.int32)
    large = jnp.minimum(large, nb - 1)
    return ret + jnp.where(n < max_exact, n, large)


def kernel(x, norm_w, w_in, attn_sink, rel_bias, attn_out_norm_w, ret_decay_fwd,
           ret_decay_bwd, ret_gn_w, w_out, final_norm_w):
    batch, seq, _ = x.shape
    assert norm_w.shape[0] == 1 and seq % BLOCK == 0
    x2 = x.reshape(batch * seq, D_MODEL)

    w = w_in[0]
    o = np.cumsum([0, ATTN_WIDTH, ATTN_KV_COLS, ATTN_KV_COLS, ATTN_WIDTH,
                   RET_QK_COLS, RET_QK_COLS, RET_WIDTH, RET_WIDTH])
    w_qa, w_ka, w_va, w_ga, w_qr, w_kr, w_vr, w_gr = [w[:, o[t]:o[t + 1]] for t in range(8)]
    w_in_p = jnp.concatenate([
        _pair_heads(w_qa, 1) * (ATTN_HEAD_DIM ** -0.5),
        _pair_heads(w_ga, 1), w_vr, w_gr, w_ka, w_va,
        _split_even_odd(w_qr),
        _split_even_odd(w_kr) * (RET_KEY_DIM ** -0.5),
    ], axis=1).astype(BF16)
    w_o = w_out[0]
    w_o_a = _pair_heads(w_o[:ATTN_WIDTH], 0).astype(BF16)
    w_o_r = w_o[ATTN_WIDTH:].astype(BF16)
    attn_nw_p = _pair_heads(attn_out_norm_w[0], 0).reshape(1, ATTN_WIDTH)

    qi = jnp.arange(BLOCK, dtype=jnp.int32)[:, None]
    kt = jnp.arange(KW, dtype=jnp.int32)[None, :]
    rel = kt - BLOCK - qi
    bias = rel_bias.astype(F32)[_t5_bucket(rel)]
    bias = jnp.where((jnp.abs(rel) <= WINDOW)[:, :, None], bias, NEG)
    bias = bias.reshape(BLOCK, KW, 2, 2, ATTN_GROUP)
    bias_t = bias.transpose(2, 3, 1, 4, 0).reshape(ATTN_KV_HEADS, KW, ATTN_GROUP * BLOCK)
    sink = attn_sink[0].astype(F32).reshape(ATTN_KV_HEADS, 1, ATTN_GROUP, 1)
    sink_t = jnp.broadcast_to(sink, (ATTN_KV_HEADS, 1, ATTN_GROUP, BLOCK)).reshape(
        ATTN_KV_HEADS, 1, ATTN_GROUP * BLOCK)

    half = RET_KEY_DIM // 2
    inv = ROPE_BASE ** (-jnp.arange(0, RET_KEY_DIM, 2, dtype=F32) / RET_KEY_DIM)
    ang = jnp.arange(seq, dtype=F32)[:, None] * inv[None, :]
    cos_t = jnp.tile(jnp.cos(ang), (1, LANES // half))
    sgn = jnp.tile(jnp.concatenate([-jnp.ones((half,), F32), jnp.ones((half,), F32)]),
                   LANES // RET_KEY_DIM)
    sin_t = jnp.tile(jnp.sin(ang), (1, LANES // half)) * sgn[None, :]

    lg_f = jax.nn.log_sigmoid(ret_decay_fwd[0].astype(F32))
    lg_b = jax.nn.log_sigmoid(ret_decay_bwd[0].astype(F32))
    idx = jnp.arange(CHUNK, dtype=F32)
    diff = idx[:, None] - idx[None, :]
    d_f = jnp.where((diff >= 0)[None], jnp.exp(lg_f[:, None, None] * jnp.maximum(diff, 0.0)[None]), 0.0)
    d_b = jnp.where((diff < 0)[None], jnp.exp(lg_b[:, None, None] * jnp.maximum(-diff, 0.0)[None]), 0.0)
    d_t = d_f + d_b

    def lane_table(lg, expo):
        per_head = jnp.exp(lg[None, :] * expo[:, None])
        t = jnp.repeat(per_head, RET_KEY_DIM, axis=1)
        return t.reshape(CHUNK, N_PAIRS, LANES).transpose(1, 0, 2)

    qdec_f = lane_table(lg_f, idx + 1.0)
    qdec_b = lane_table(lg_b, CHUNK - idx)
    kdec_f = lane_table(lg_f, CHUNK - 1.0 - idx)
    kdec_b = lane_table(lg_b, idx)

    def state_decay(lg):
        cd = jnp.exp(lg * CHUNK).reshape(N_PAIRS, 2)
        rows = jnp.repeat(cd, RET_KEY_DIM, axis=1)
        r = np.arange(LANES)[:, None] // RET_KEY_DIM
        m = np.arange(2 * RET_VAL_DIM)[None, :] // RET_VAL_DIM
        return jnp.where(jnp.asarray(r == m)[None], rows[:, :, None], 0.0)

    g_f = state_decay(lg_f)
    g_b = state_decay(lg_b)

    proj = _inproj(x2, norm_w[0].reshape(1, D_MODEL), w_in_p)
    attn_o = _attention(proj, bias_t, sink_t, batch, seq)
    pb = _retention_states(proj, cos_t, sin_t, kdec_b, g_b, batch, seq)
    ret_o = _retention(proj, pb, cos_t, sin_t, d_t, qdec_f, qdec_b, kdec_f, g_f,
                       ret_gn_w[0].reshape(1, RET_WIDTH), batch, seq)
    out = _outproj(attn_o, proj, ret_o, x2, attn_nw_p, w_o_a, w_o_r,
                   final_norm_w.reshape(1, D_MODEL))
    return out.reshape(batch, seq, D_MODEL)
```

```python
import functools
import math

import numpy as np
import jax
import jax.numpy as jnp
from jax import lax
from jax.experimental import pallas as pl
from jax.experimental.pallas import tpu as pltpu

D_MODEL = 2048
ATTN_HEAD_DIM = 64
ATTN_WIDTH = 1024
ATTN_HEADS = 16
ATTN_KV_HEADS = 4
ATTN_GROUP = 4
ATTN_KV_COLS = 256
WINDOW = 128
BLOCK = 128
N_BUCKETS = 32
MAX_DISTANCE = 128
RET_WIDTH = 1024
RET_HEADS = 8
RET_VAL_DIM = 128
RET_KEY_DIM = 64
RET_QK_COLS = 512
CHUNK = 128
ROPE_BASE = 10000.0
EPS = 1e-6
NEG = -1e30

LANES = 128
N_PAIRS = RET_HEADS // 2
KW = 3 * BLOCK

OFF_QA, OFF_GA, OFF_VR, OFF_GR = 0, 1024, 2048, 3072
OFF_KVA, OFF_QR, OFF_KR = 4096, 4608, 5120
IN_WIDTH = 5632

VMEM_LIMIT = 56 * 1024 * 1024

F32 = jnp.float32
BF16 = jnp.bfloat16


def _inproj_kernel(x_ref, nw_ref, w_ref, o_ref, *, row_chunks, tn):
    tm = x_ref.shape[0]
    rc = tm // row_chunks
    nw = nw_ref[...]
    for r in range(row_chunks):
        x = x_ref[rc * r:rc * (r + 1), :]
        ms = jnp.mean(x * x, axis=-1, keepdims=True)
        h = (x * lax.rsqrt(ms + EPS) * nw).astype(BF16)
        for n in range(IN_WIDTH // tn):
            o_ref[rc * r:rc * (r + 1), tn * n:tn * (n + 1)] = jnp.dot(
                h, w_ref[:, tn * n:tn * (n + 1)],
                preferred_element_type=F32).astype(o_ref.dtype)


def _inproj(x2, norm_w, w_in_p, tm=512, row_chunks=2, tn=512):
    m = x2.shape[0]
    return pl.pallas_call(
        functools.partial(_inproj_kernel, row_chunks=row_chunks, tn=tn),
        grid=(m // tm,),
        in_specs=[
            pl.BlockSpec((tm, D_MODEL), lambda i: (i, 0)),
            pl.BlockSpec((1, D_MODEL), lambda i: (0, 0)),
            pl.BlockSpec((D_MODEL, IN_WIDTH), lambda i: (0, 0)),
        ],
        out_specs=pl.BlockSpec((tm, IN_WIDTH), lambda i: (i, 0)),
        out_shape=jax.ShapeDtypeStruct((m, IN_WIDTH), BF16),
        compiler_params=pltpu.CompilerParams(
            dimension_semantics=("arbitrary",),
            vmem_limit_bytes=VMEM_LIMIT),
        name="inproj",
    )(x2, norm_w, w_in_p)


def _silu(g):
    return g / (1.0 + jnp.exp(-g))


def _attn_kernel(q_ref, kvp_ref, kvc_ref, kvn_ref, bias_ref, sink_ref, o_ref):
    i = pl.program_id(1)
    nt = pl.num_programs(1)
    n_blk = q_ref.shape[0] // BLOCK
    thr_in = jnp.float32(0.5 * NEG)
    thr_first = jnp.where(i > 0, 0.5 * NEG, -NEG).astype(F32)
    thr_last = jnp.where(i < nt - 1, 0.5 * NEG, -NEG).astype(F32)
    kv = jnp.concatenate([kvp_ref[...], kvc_ref[...], kvn_ref[...]], axis=0)
    lane = lax.broadcasted_iota(jnp.int32, (KW, LANES), 1)
    low_half = lane < ATTN_HEAD_DIM
    hd = ATTN_HEAD_DIM

    for blk in range(n_blk):
        thr_l = thr_first if blk == 0 else thr_in
        thr_r = thr_last if blk == n_blk - 1 else thr_in
        rows = slice(BLOCK * blk, BLOCK * (blk + 1))
        for j in range(ATTN_KV_COLS // LANES):
            kj = kv[BLOCK * blk:BLOCK * blk + KW, LANES * j:LANES * (j + 1)]
            vj = kv[BLOCK * blk:BLOCK * blk + KW,
                    ATTN_KV_COLS + LANES * j:ATTN_KV_COLS + LANES * (j + 1)]
            qs = jnp.concatenate(
                [q_ref[rows, LANES * c:LANES * (c + 1)]
                 for c in range(ATTN_GROUP * j, ATTN_GROUP * (j + 1))], axis=0)
            halves = []
            for half in range(2):
                sel = low_half if half == 0 else jnp.logical_not(low_half)
                kh = jnp.where(sel, kj, jnp.zeros_like(kj))
                vh = jnp.where(sel, vj, jnp.ones_like(vj))
                st = lax.dot_general(kh, qs, (((1,), (1,)), ((), ())),
                                     preferred_element_type=F32)
                b = bias_ref[2 * j + half]
                t = jnp.concatenate([
                    jnp.where(b[:BLOCK] > thr_l, st[:BLOCK] + b[:BLOCK], NEG),
                    st[BLOCK:2 * BLOCK] + b[BLOCK:2 * BLOCK],
                    jnp.where(b[2 * BLOCK:] > thr_r, st[2 * BLOCK:] + b[2 * BLOCK:], NEG),
                ], axis=0)
                sk = sink_ref[2 * j + half]
                m = jnp.maximum(jnp.max(t, axis=0, keepdims=True), sk)
                p = jnp.exp(t - m).astype(BF16)
                ot = lax.dot_general(vh, p, (((0,), (0,)), ((), ())),
                                     preferred_element_type=F32)
                if half == 0:
                    num, den = ot[:hd], ot[hd:hd + 1]
                else:
                    num, den = ot[hd:], ot[0:1]
                halves.append(num / (den + jnp.exp(sk - m)))
            out_t = jnp.concatenate(halves, axis=0)
            for r in range(ATTN_GROUP):
                c = ATTN_GROUP * j + r
                o_ref[rows, LANES * c:LANES * (c + 1)] = (
                    out_t[:, BLOCK * r:BLOCK * (r + 1)].T.astype(o_ref.dtype))


def _attention(proj, bias_t, sink_t, batch, seq, tq=512):
    nt = seq // tq
    bpt = tq // BLOCK
    nb = seq // BLOCK
    kvblk = OFF_KVA // 512
    n_kv = ATTN_KV_HEADS
    prev_blk = lambda b, i: (b * nb + jnp.maximum(bpt * i - 1, 0), kvblk)
    next_blk = lambda b, i: (b * nb + jnp.minimum(bpt * (i + 1), nb - 1), kvblk)
    return pl.pallas_call(
        _attn_kernel,
        grid=(batch, nt),
        in_specs=[
            pl.BlockSpec((tq, ATTN_WIDTH), lambda b, i: (b * nt + i, OFF_QA // 1024)),
            pl.BlockSpec((BLOCK, 512), prev_blk),
            pl.BlockSpec((tq, 512), lambda b, i: (b * nt + i, kvblk)),
            pl.BlockSpec((BLOCK, 512), next_blk),
            pl.BlockSpec((n_kv, KW, ATTN_GROUP * BLOCK), lambda b, i: (0, 0, 0)),
            pl.BlockSpec((n_kv, 1, ATTN_GROUP * BLOCK), lambda b, i: (0, 0, 0)),
        ],
        out_specs=pl.BlockSpec((tq, ATTN_WIDTH), lambda b, i: (b * nt + i, 0)),
        out_shape=jax.ShapeDtypeStruct((batch * seq, ATTN_WIDTH), BF16),
        compiler_params=pltpu.CompilerParams(
            dimension_semantics=("arbitrary", "arbitrary"),
            vmem_limit_bytes=VMEM_LIMIT),
        name="attn",
    )(proj, proj, proj, proj, bias_t, sink_t)


def _rotary(x, cos, sin_signed):
    lane = lax.broadcasted_iota(jnp.int32, x.shape, 1)
    first = (lane % RET_KEY_DIM) < (RET_KEY_DIM // 2)
    partner = jnp.where(first,
                        pltpu.roll(x, LANES - RET_KEY_DIM // 2, axis=1),
                        pltpu.roll(x, RET_KEY_DIM // 2, axis=1))
    return x * cos + partner * sin_signed


def _state_mask():
    r = lax.broadcasted_iota(jnp.int32, (LANES, 2 * RET_VAL_DIM), 0)
    m = lax.broadcasted_iota(jnp.int32, (LANES, 2 * RET_VAL_DIM), 1)
    return (r // RET_KEY_DIM) == (m // RET_VAL_DIM)


def _state_update(st_ref, c, k_rot, kdec, gdec, vpair):
    kd = (k_rot * kdec).astype(BF16)
    upd = lax.dot_general(kd, vpair, (((0,), (0,)), ((), ())),
                          preferred_element_type=F32)
    st_ref[c] = st_ref[c] * gdec + jnp.where(_state_mask(), upd, 0.0)


def _retstate_kernel(k_ref, v_ref, cos_ref, sin_ref, kdb_ref, gb_ref, pb_ref, st_ref):
    @pl.when(pl.program_id(1) == 0)
    def _():
        st_ref[...] = jnp.zeros_like(st_ref)

    n_ch = k_ref.shape[0] // CHUNK
    for ch in reversed(range(n_ch)):
        rows = slice(CHUNK * ch, CHUNK * (ch + 1))
        pb_ref[ch] = st_ref[...].astype(BF16)
        cos = cos_ref[rows, :]
        sin = sin_ref[rows, :]
        for c in range(N_PAIRS):
            k = _rotary(k_ref[rows, LANES * c:LANES * (c + 1)].astype(F32), cos, sin)
            vpair = v_ref[rows, 2 * RET_VAL_DIM * c:2 * RET_VAL_DIM * (c + 1)]
            _state_update(st_ref, c, k, kdb_ref[c], gb_ref[c], vpair)


def _retention_states(proj, cos_t, sin_t, kdec_b, g_b, batch, seq, tt=512):
    nt = seq // tt
    cpt = tt // CHUNK
    rrow = lambda b, i: b * nt + (nt - 1 - i)
    return pl.pallas_call(
        _retstate_kernel,
        grid=(batch, nt),
        in_specs=[
            pl.BlockSpec((tt, RET_QK_COLS), lambda b, i: (rrow(b, i), OFF_KR // 512)),
            pl.BlockSpec((tt, RET_WIDTH), lambda b, i: (rrow(b, i), OFF_VR // 1024)),
            pl.BlockSpec((tt, LANES), lambda b, i: (nt - 1 - i, 0)),
            pl.BlockSpec((tt, LANES), lambda b, i: (nt - 1 - i, 0)),
            pl.BlockSpec((N_PAIRS, CHUNK, LANES), lambda b, i: (0, 0, 0)),
            pl.BlockSpec((N_PAIRS, LANES, 2 * RET_VAL_DIM), lambda b, i: (0, 0, 0)),
        ],
        out_specs=pl.BlockSpec((cpt, N_PAIRS, LANES, 2 * RET_VAL_DIM),
                               lambda b, i: (rrow(b, i), 0, 0, 0)),
        out_shape=jax.ShapeDtypeStruct((batch * nt * cpt, N_PAIRS, LANES, 2 * RET_VAL_DIM), BF16),
        scratch_shapes=[pltpu.VMEM((N_PAIRS, LANES, 2 * RET_VAL_DIM), F32)],
        compiler_params=pltpu.CompilerParams(
            dimension_semantics=("arbitrary", "arbitrary"),
            vmem_limit_bytes=VMEM_LIMIT),
        name="ret_states",
    )(proj, proj, cos_t, sin_t, kdec_b, g_b)


def _ret_kernel(q_ref, k_ref, v_ref, g_ref, pb_ref, cos_ref, sin_ref, d_ref,
                qdf_ref, qdb_ref, kdf_ref, gf_ref, gnw_ref, o_ref, st_ref):
    @pl.when(pl.program_id(1) == 0)
    def _():
        st_ref[...] = jnp.zeros_like(st_ref)

    lane = lax.broadcasted_iota(jnp.int32, (CHUNK, LANES), 1)
    low_half = lane < RET_KEY_DIM
    vcol = lax.broadcasted_iota(jnp.int32, (CHUNK, 2 * RET_VAL_DIM), 1)
    first_v = vcol < RET_VAL_DIM
    for ch in range(q_ref.shape[0] // CHUNK):
        rows = slice(CHUNK * ch, CHUNK * (ch + 1))
        cos = cos_ref[rows, :]
        sin = sin_ref[rows, :]
        for c in range(N_PAIRS):
            q = _rotary(q_ref[rows, LANES * c:LANES * (c + 1)].astype(F32), cos, sin)
            k = _rotary(k_ref[rows, LANES * c:LANES * (c + 1)].astype(F32), cos, sin)
            kb = k.astype(BF16)
            vpair = v_ref[rows, 2 * RET_VAL_DIM * c:2 * RET_VAL_DIM * (c + 1)]
            parts = []
            for half in range(2):
                sel = low_half if half == 0 else jnp.logical_not(low_half)
                qh = jnp.where(sel, q, 0.0).astype(BF16)
                s = lax.dot_general(qh, kb, (((1,), (1,)), ((), ())),
                                    preferred_element_type=F32)
                parts.append((s * d_ref[2 * c + half]).astype(BF16))
            parts.append((q * qdf_ref[c]).astype(BF16))
            parts.append((q * qdb_ref[c]).astype(BF16))
            lhs = jnp.concatenate(parts, axis=1)
            zero_v = jnp.zeros_like(vpair)
            rhs = jnp.concatenate([
                jnp.where(first_v, vpair, zero_v),
                jnp.where(first_v, zero_v, vpair),
                st_ref[c].astype(BF16),
                pb_ref[ch, c],
            ], axis=0)
            o = jnp.dot(lhs, rhs, preferred_element_type=F32)
            _state_update(st_ref, c, k, kdf_ref[c], gf_ref[c], vpair)
            for half in range(2):
                lo_c = RET_VAL_DIM * (2 * c + half)
                oh = o[:, RET_VAL_DIM * half:RET_VAL_DIM * (half + 1)]
                mu = jnp.mean(oh, axis=-1, keepdims=True)
                dev = oh - mu
                var = jnp.mean(dev * dev, axis=-1, keepdims=True)
                y = dev * lax.rsqrt(var + EPS) * gnw_ref[:, lo_c:lo_c + RET_VAL_DIM]
                g = g_ref[rows, lo_c:lo_c + RET_VAL_DIM].astype(F32)
                o_ref[rows, lo_c:lo_c + RET_VAL_DIM] = (y * _silu(g)).astype(o_ref.dtype)


def _retention(proj, pb, cos_t, sin_t, d_t, qdec_f, qdec_b, kdec_f, g_f, gn_w, batch, seq,
               tt=512):
    nt = seq // tt
    cpt = tt // CHUNK
    row = lambda b, i: b * nt + i
    const3 = lambda b, i: (0, 0, 0)
    return pl.pallas_call(
        _ret_kernel,
        grid=(batch, nt),
        in_specs=[
            pl.BlockSpec((tt, RET_QK_COLS), lambda b, i: (row(b, i), OFF_QR // 512)),
            pl.BlockSpec((tt, RET_QK_COLS), lambda b, i: (row(b, i), OFF_KR // 512)),
            pl.BlockSpec((tt, RET_WIDTH), lambda b, i: (row(b, i), OFF_VR // 1024)),
            pl.BlockSpec((tt, RET_WIDTH), lambda b, i: (row(b, i), OFF_GR // 1024)),
            pl.BlockSpec((cpt, N_PAIRS, LANES, 2 * RET_VAL_DIM), lambda b, i: (row(b, i), 0, 0, 0)),
            pl.BlockSpec((tt, LANES), lambda b, i: (i, 0)),
            pl.BlockSpec((tt, LANES), lambda b, i: (i, 0)),
            pl.BlockSpec((RET_HEADS, CHUNK, CHUNK), const3),
            pl.BlockSpec((N_PAIRS, CHUNK, LANES), const3),
            pl.BlockSpec((N_PAIRS, CHUNK, LANES), const3),
            pl.BlockSpec((N_PAIRS, CHUNK, LANES), const3),
            pl.BlockSpec((N_PAIRS, LANES, 2 * RET_VAL_DIM), const3),
            pl.BlockSpec((1, RET_WIDTH), lambda b, i: (0, 0)),
        ],
        out_specs=pl.BlockSpec((tt, RET_WIDTH), lambda b, i: (row(b, i), 0)),
        out_shape=jax.ShapeDtypeStruct((batch * seq, RET_WIDTH), BF16),
        scratch_shapes=[pltpu.VMEM((N_PAIRS, LANES, 2 * RET_VAL_DIM), F32)],
        compiler_params=pltpu.CompilerParams(
            dimension_semantics=("arbitrary", "arbitrary"),
            vmem_limit_bytes=VMEM_LIMIT),
        name="retention",
    )(proj, proj, proj, proj, pb, cos_t, sin_t, d_t, qdec_f, qdec_b, kdec_f, g_f, gn_w)


def _outproj_kernel(a_ref, g_ref, r_ref, x_ref, anw_ref, wa_ref, wr_ref, fw_ref, o_ref):
    a = a_ref[...].astype(F32)
    ms = jnp.mean(a * a, axis=-1, keepdims=True)
    a = a * lax.rsqrt(ms + EPS) * anw_ref[...]
    am = (a * _silu(g_ref[...].astype(F32))).astype(BF16)
    y = (x_ref[...]
         + jnp.dot(am, wa_ref[...], preferred_element_type=F32)
         + jnp.dot(r_ref[...], wr_ref[...], preferred_element_type=F32))
    ms = jnp.mean(y * y, axis=-1, keepdims=True)
    o_ref[...] = y * lax.rsqrt(ms + EPS) * fw_ref[...]


def _outproj(attn_o, proj, ret_o, x2, attn_nw, w_a, w_r, final_w, tm=512):
    m = x2.shape[0]
    return pl.pallas_call(
        _outproj_kernel,
        grid=(m // tm,),
        in_specs=[
            pl.BlockSpec((tm, ATTN_WIDTH), lambda i: (i, 0)),
            pl.BlockSpec((tm, ATTN_WIDTH), lambda i: (i, OFF_GA // 1024)),
            pl.BlockSpec((tm, RET_WIDTH), lambda i: (i, 0)),
            pl.BlockSpec((tm, D_MODEL), lambda i: (i, 0)),
            pl.BlockSpec((1, ATTN_WIDTH), lambda i: (0, 0)),
            pl.BlockSpec((ATTN_WIDTH, D_MODEL), lambda i: (0, 0)),
            pl.BlockSpec((RET_WIDTH, D_MODEL), lambda i: (0, 0)),
            pl.BlockSpec((1, D_MODEL), lambda i: (0, 0)),
        ],
        out_specs=pl.BlockSpec((tm, D_MODEL), lambda i: (i, 0)),
        out_shape=jax.ShapeDtypeStruct((m, D_MODEL), F32),
        compiler_params=pltpu.CompilerParams(
            dimension_semantics=("arbitrary",),
            vmem_limit_bytes=VMEM_LIMIT),
        name="outproj",
    )(attn_o, proj, ret_o, x2, attn_nw, w_a, w_r, final_w)


def _pair_heads(a, axis):
    a = jnp.moveaxis(a, axis, -1)
    lead = a.shape[:-1]
    d = a.shape[-1] // ATTN_HEADS
    a = a.reshape(lead + (2, 2, ATTN_GROUP, d))
    a = jnp.swapaxes(a, -3, -2)
    a = a.reshape(lead + (ATTN_HEADS * d,))
    return jnp.moveaxis(a, -1, axis)


def _split_even_odd(w):
    k = w.shape[0]
    w = w.reshape(k, RET_HEADS, RET_KEY_DIM // 2, 2)
    return jnp.swapaxes(w, -1, -2).reshape(k, RET_QK_COLS)


def _t5_bucket(rel):
    nb = N_BUCKETS // 2
    max_exact = nb // 2
    ret = jnp.where(rel > 0, nb, 0)
    n = jnp.abs(rel)
    nf = jnp.maximum(n, 1).astype(F32)
    large = max_exact + (jnp.log(nf / max_exact) / math.log(MAX_DISTANCE / max_exact)
                         * (nb - max_exact)).astype(jnp.int32)
    large = jnp.minimum(large, nb - 1)
    return ret + jnp.where(n < max_exact, n, large)


def kernel(x, norm_w, w_in, attn_sink, rel_bias, attn_out_norm_w, ret_decay_fwd,
           ret_decay_bwd, ret_gn_w, w_out, final_norm_w):
    batch, seq, _ = x.shape
    assert norm_w.shape[0] == 1 and seq % BLOCK == 0
    x2 = x.reshape(batch * seq, D_MODEL)

    w = w_in[0]
    o = np.cumsum([0, ATTN_WIDTH, ATTN_KV_COLS, ATTN_KV_COLS, ATTN_WIDTH,
                   RET_QK_COLS, RET_QK_COLS, RET_WIDTH, RET_WIDTH])
    w_qa, w_ka, w_va, w_ga, w_qr, w_kr, w_vr, w_gr = [w[:, o[t]:o[t + 1]] for t in range(8)]
    w_in_p = jnp.concatenate([
        _pair_heads(w_qa, 1) * (ATTN_HEAD_DIM ** -0.5),
        _pair_heads(w_ga, 1), w_vr, w_gr, w_ka, w_va,
        _split_even_odd(w_qr),
        _split_even_odd(w_kr) * (RET_KEY_DIM ** -0.5),
    ], axis=1).astype(BF16)
    w_o = w_out[0]
    w_o_a = _pair_heads(w_o[:ATTN_WIDTH], 0).astype(BF16)
    w_o_r = w_o[ATTN_WIDTH:].astype(BF16)
    attn_nw_p = _pair_heads(attn_out_norm_w[0], 0).reshape(1, ATTN_WIDTH)

    qi = jnp.arange(BLOCK, dtype=jnp.int32)[:, None]
    kt = jnp.arange(KW, dtype=jnp.int32)[None, :]
    rel = kt - BLOCK - qi
    onehot = (_t5_bucket(rel)[:, :, None] == jnp.arange(N_BUCKETS)[None, None, :]).astype(F32)
    bias = jnp.einsum("qkn,nh->qkh", onehot, rel_bias.astype(F32),
                      precision=lax.Precision.HIGHEST)
    bias = jnp.where((jnp.abs(rel) <= WINDOW)[:, :, None], bias, NEG)
    bias = bias.reshape(BLOCK, KW, 2, 2, ATTN_GROUP)
    bias_t = bias.transpose(2, 3, 1, 4, 0).reshape(ATTN_KV_HEADS, KW, ATTN_GROUP * BLOCK)
    sink = attn_sink[0].astype(F32).reshape(ATTN_KV_HEADS, 1, ATTN_GROUP, 1)
    sink_t = jnp.broadcast_to(sink, (ATTN_KV_HEADS, 1, ATTN_GROUP, BLOCK)).reshape(
        ATTN_KV_HEADS, 1, ATTN_GROUP * BLOCK)

    half = RET_KEY_DIM // 2
    inv = ROPE_BASE ** (-jnp.arange(0, RET_KEY_DIM, 2, dtype=F32) / RET_KEY_DIM)
    ang = jnp.arange(seq, dtype=F32)[:, None] * inv[None, :]
    cos_t = jnp.tile(jnp.cos(ang), (1, LANES // half))
    sgn = jnp.tile(jnp.concatenate([-jnp.ones((half,), F32), jnp.ones((half,), F32)]),
                   LANES // RET_KEY_DIM)
    sin_t = jnp.tile(jnp.sin(ang), (1, LANES // half)) * sgn[None, :]

    lg_f = jax.nn.log_sigmoid(ret_decay_fwd[0].astype(F32))
    lg_b = jax.nn.log_sigmoid(ret_decay_bwd[0].astype(F32))
    idx = jnp.arange(CHUNK, dtype=F32)
    diff = idx[:, None] - idx[None, :]
    d_f = jnp.where((diff >= 0)[None], jnp.exp(lg_f[:, None, None] * jnp.maximum(diff, 0.0)[None]), 0.0)
    d_b = jnp.where((diff < 0)[None], jnp.exp(lg_b[:, None, None] * jnp.maximum(-diff, 0.0)[None]), 0.0)
    d_t = d_f + d_b

    def lane_table(lg, expo):
        per_head = jnp.exp(lg[None, :] * expo[:, None])
        t = jnp.repeat(per_head, RET_KEY_DIM, axis=1)
        return t.reshape(CHUNK, N_PAIRS, LANES).transpose(1, 0, 2)

    qdec_f = lane_table(lg_f, idx + 1.0)
    qdec_b = lane_table(lg_b, CHUNK - idx)
    kdec_f = lane_table(lg_f, CHUNK - 1.0 - idx)
    kdec_b = lane_table(lg_b, idx)

    def state_decay(lg):
        cd = jnp.exp(lg * CHUNK).reshape(N_PAIRS, 2)
        rows = jnp.repeat(cd, RET_KEY_DIM, axis=1)
        r = np.arange(LANES)[:, None] // RET_KEY_DIM
        m = np.arange(2 * RET_VAL_DIM)[None, :] // RET_VAL_DIM
        return jnp.where(jnp.asarray(r == m)[None], rows[:, :, None], 0.0)

    g_f = state_decay(lg_f)
    g_b = state_decay(lg_b)

    proj = _inproj(x2, norm_w[0].reshape(1, D_MODEL), w_in_p)
    attn_o = _attention(proj, bias_t, sink_t, batch, seq)
    pb = _retention_states(proj, cos_t, sin_t, kdec_b, g_b, batch, seq)
    ret_o = _retention(proj, pb, cos_t, sin_t, d_t, qdec_f, qdec_b, kdec_f, g_f,
                       ret_gn_w[0].reshape(1, RET_WIDTH), batch, seq)
    out = _outproj(attn_o, proj, ret_o, x2, attn_nw_p, w_o_a, w_o_r,
                   final_norm_w.reshape(1, D_MODEL))
    return out.reshape(batch, seq, D_MODEL)
```

```python
import functools
import math

import numpy as np
import jax
import jax.numpy as jnp
from jax import lax
from jax.experimental import pallas as pl
from jax.experimental.pallas import tpu as pltpu

D_MODEL = 2048
ATTN_HEAD_DIM = 64
ATTN_WIDTH = 1024
ATTN_HEADS = 16
ATTN_KV_HEADS = 4
ATTN_GROUP = 4
ATTN_KV_COLS = 256
WINDOW = 128
BLOCK = 128
N_BUCKETS = 32
MAX_DISTANCE = 128
RET_WIDTH = 1024
RET_HEADS = 8
RET_VAL_DIM = 128
RET_KEY_DIM = 64
RET_QK_COLS = 512
CHUNK = 128
ROPE_BASE = 10000.0
EPS = 1e-6
NEG = -1e30
LOG2E = math.log2(math.e)

LANES = 128
N_PAIRS = RET_HEADS // 2
KW = 3 * BLOCK

OFF_QA, OFF_GA, OFF_VR, OFF_GR = 0, 1024, 2048, 3072
OFF_KVA, OFF_QR, OFF_KR = 4096, 4608, 5120
IN_WIDTH = 5632

VMEM_LIMIT = 56 * 1024 * 1024

F32 = jnp.float32
BF16 = jnp.bfloat16


def _silu(g):
    return g / (1.0 + jnp.exp(-g))


def _rotary(x, cos, sin_signed):
    lane = lax.broadcasted_iota(jnp.int32, x.shape, 1)
    even = (lane % 2) == 0
    partner = jnp.where(even, pltpu.roll(x, LANES - 1, axis=1), pltpu.roll(x, 1, axis=1))
    return x * cos + partner * sin_signed


def _inproj_kernel(x_ref, nw_ref, w_ref, cos_ref, sin_ref, o_ref, *, row_chunks, tn):
    tm = x_ref.shape[0]
    rc = tm // row_chunks
    nw = nw_ref[...]
    for r in range(row_chunks):
        rows = slice(rc * r, rc * (r + 1))
        x = x_ref[rows, :]
        ms = jnp.mean(x * x, axis=-1, keepdims=True)
        h = (x * lax.rsqrt(ms + EPS) * nw).astype(BF16)
        for n in range(IN_WIDTH // tn):
            col = tn * n
            res = jnp.dot(h, w_ref[:, col:col + tn], preferred_element_type=F32)
            if OFF_GA <= col < OFF_GA + ATTN_WIDTH or OFF_GR <= col < OFF_GR + RET_WIDTH:
                res = _silu(res)
            elif col >= OFF_QR:
                cos = cos_ref[rows, :]
                sin = sin_ref[rows, :]
                res = jnp.concatenate(
                    [_rotary(res[:, LANES * c:LANES * (c + 1)], cos, sin)
                     for c in range(tn // LANES)], axis=1)
            o_ref[rows, col:col + tn] = res.astype(o_ref.dtype)


def _inproj(x2, norm_w, w_in_p, cos_t, sin_t, seq, tm=512, row_chunks=2, tn=512):
    m = x2.shape[0]
    tiles_per_seq = seq // tm
    return pl.pallas_call(
        functools.partial(_inproj_kernel, row_chunks=row_chunks, tn=tn),
        grid=(m // tm,),
        in_specs=[
            pl.BlockSpec((tm, D_MODEL), lambda i: (i, 0)),
            pl.BlockSpec((1, D_MODEL), lambda i: (0, 0)),
            pl.BlockSpec((D_MODEL, IN_WIDTH), lambda i: (0, 0)),
            pl.BlockSpec((tm, LANES), lambda i: (i % tiles_per_seq, 0)),
            pl.BlockSpec((tm, LANES), lambda i: (i % tiles_per_seq, 0)),
        ],
        out_specs=pl.BlockSpec((tm, IN_WIDTH), lambda i: (i, 0)),
        out_shape=jax.ShapeDtypeStruct((m, IN_WIDTH), BF16),
        compiler_params=pltpu.CompilerParams(
            dimension_semantics=("arbitrary",),
            vmem_limit_bytes=VMEM_LIMIT),
        name="inproj",
    )(x2, norm_w, w_in_p, cos_t, sin_t)


def _attn_kernel(q_ref, kvp_ref, kvc_ref, kvn_ref, bias_ref, sink_ref, o_ref):
    i = pl.program_id(1)
    nt = pl.num_programs(1)
    n_blk = q_ref.shape[0] // BLOCK
    thr_in = jnp.float32(0.5 * NEG)
    thr_first = jnp.where(i > 0, 0.5 * NEG, -NEG).astype(F32)
    thr_last = jnp.where(i < nt - 1, 0.5 * NEG, -NEG).astype(F32)
    kv = jnp.concatenate([kvp_ref[...], kvc_ref[...], kvn_ref[...]], axis=0)
    lane = lax.broadcasted_iota(jnp.int32, (KW, LANES), 1)
    low_half = lane < ATTN_HEAD_DIM
    hd = ATTN_HEAD_DIM

    def scores(item):
        blk, j, half = item
        sel = low_half if half == 0 else jnp.logical_not(low_half)
        kj = kv[BLOCK * blk:BLOCK * blk + KW, LANES * j:LANES * (j + 1)]
        kh = jnp.where(sel, kj, jnp.zeros_like(kj))
        qs = jnp.concatenate(
            [q_ref[BLOCK * blk:BLOCK * (blk + 1), LANES * c:LANES * (c + 1)]
             for c in range(ATTN_GROUP * j, ATTN_GROUP * (j + 1))], axis=0)
        return lax.dot_general(kh, qs, (((1,), (1,)), ((), ())),
                               preferred_element_type=F32)

    def finish(item, st):
        blk, j, half = item
        thr_l = thr_first if blk == 0 else thr_in
        thr_r = thr_last if blk == n_blk - 1 else thr_in
        sel = low_half if half == 0 else jnp.logical_not(low_half)
        vj = kv[BLOCK * blk:BLOCK * blk + KW,
                ATTN_KV_COLS + LANES * j:ATTN_KV_COLS + LANES * (j + 1)]
        vh = jnp.where(sel, vj, jnp.ones_like(vj))
        b = bias_ref[2 * j + half]
        t = jnp.concatenate([
            jnp.where(b[:BLOCK] > thr_l, st[:BLOCK] + b[:BLOCK], NEG),
            st[BLOCK:2 * BLOCK] + b[BLOCK:2 * BLOCK],
            jnp.where(b[2 * BLOCK:] > thr_r, st[2 * BLOCK:] + b[2 * BLOCK:], NEG),
        ], axis=0)
        sk = sink_ref[2 * j + half]
        m = jnp.maximum(jnp.max(t, axis=0, keepdims=True), sk)
        p = jnp.exp2(t - m).astype(BF16)
        ot = lax.dot_general(vh, p, (((0,), (0,)), ((), ())),
                             preferred_element_type=F32)
        if half == 0:
            num, den = ot[:hd], ot[hd:hd + 1]
        else:
            num, den = ot[hd:], ot[0:1]
        return num / (den + jnp.exp2(sk - m))

    items = [(blk, j, half) for blk in range(n_blk)
             for j in range(ATTN_KV_COLS // LANES) for half in range(2)]
    ahead = 1
    pending = [scores(it) for it in items[:ahead]]
    lo_half_out = None
    for idx, item in enumerate(items):
        st = pending.pop(0)
        if idx + ahead < len(items):
            pending.append(scores(items[idx + ahead]))
        out_half = finish(item, st)
        blk, j, half = item
        if half == 0:
            lo_half_out = out_half
            continue
        out_t = jnp.concatenate([lo_half_out, out_half], axis=0)
        for r in range(ATTN_GROUP):
            c = ATTN_GROUP * j + r
            o_ref[BLOCK * blk:BLOCK * (blk + 1), LANES * c:LANES * (c + 1)] = (
                out_t[:, BLOCK * r:BLOCK * (r + 1)].T.astype(o_ref.dtype))


def _attention(proj, bias_t, sink_t, batch, seq, tq=512):
    nt = seq // tq
    bpt = tq // BLOCK
    nb = seq // BLOCK
    kvblk = OFF_KVA // 512
    n_kv = ATTN_KV_HEADS
    prev_blk = lambda b, i: (b * nb + jnp.maximum(bpt * i - 1, 0), kvblk)
    next_blk = lambda b, i: (b * nb + jnp.minimum(bpt * (i + 1), nb - 1), kvblk)
    return pl.pallas_call(
        _attn_kernel,
        grid=(batch, nt),
        in_specs=[
            pl.BlockSpec((tq, ATTN_WIDTH), lambda b, i: (b * nt + i, OFF_QA // 1024)),
            pl.BlockSpec((BLOCK, 512), prev_blk),
            pl.BlockSpec((tq, 512), lambda b, i: (b * nt + i, kvblk)),
            pl.BlockSpec((BLOCK, 512), next_blk),
            pl.BlockSpec((n_kv, KW, ATTN_GROUP * BLOCK), lambda b, i: (0, 0, 0)),
            pl.BlockSpec((n_kv, 1, ATTN_GROUP * BLOCK), lambda b, i: (0, 0, 0)),
        ],
        out_specs=pl.BlockSpec((tq, ATTN_WIDTH), lambda b, i: (b * nt + i, 0)),
        out_shape=jax.ShapeDtypeStruct((batch * seq, ATTN_WIDTH), BF16),
        compiler_params=pltpu.CompilerParams(
            dimension_semantics=("arbitrary", "arbitrary"),
            vmem_limit_bytes=VMEM_LIMIT),
        name="attn",
    )(proj, proj, proj, proj, bias_t, sink_t)


def _state_mask():
    r = lax.broadcasted_iota(jnp.int32, (LANES, 2 * RET_VAL_DIM), 0)
    m = lax.broadcasted_iota(jnp.int32, (LANES, 2 * RET_VAL_DIM), 1)
    return (r // RET_KEY_DIM) == (m // RET_VAL_DIM)


def _state_update(st_ref, c, k_rot, kdec, gdec, vpair):
    kd = (k_rot * kdec).astype(BF16)
    upd = lax.dot_general(kd, vpair, (((0,), (0,)), ((), ())),
                          preferred_element_type=F32)
    st_ref[c] = st_ref[c] * gdec + jnp.where(_state_mask(), upd, 0.0)


def _retstate_kernel(k_ref, v_ref, kdb_ref, gb_ref, pb_ref, st_ref):
    @pl.when(pl.program_id(1) == 0)
    def _():
        st_ref[...] = jnp.zeros_like(st_ref)

    n_ch = k_ref.shape[0] // CHUNK
    for ch in reversed(range(n_ch)):
        rows = slice(CHUNK * ch, CHUNK * (ch + 1))
        pb_ref[ch] = st_ref[...].astype(BF16)
        for c in range(N_PAIRS):
            k = k_ref[rows, LANES * c:LANES * (c + 1)].astype(F32)
            vpair = v_ref[rows, 2 * RET_VAL_DIM * c:2 * RET_VAL_DIM * (c + 1)]
            _state_update(st_ref, c, k, kdb_ref[c], gb_ref[c], vpair)


def _retention_states(proj, kdec_b, g_b, batch, seq, tt=512):
    nt = seq // tt
    cpt = tt // CHUNK
    rrow = lambda b, i: b * nt + (nt - 1 - i)
    return pl.pallas_call(
        _retstate_kernel,
        grid=(batch, nt),
        in_specs=[
            pl.BlockSpec((tt, RET_QK_COLS), lambda b, i: (rrow(b, i), OFF_KR // 512)),
            pl.BlockSpec((tt, RET_WIDTH), lambda b, i: (rrow(b, i), OFF_VR // 1024)),
            pl.BlockSpec((N_PAIRS, CHUNK, LANES), lambda b, i: (0, 0, 0)),
            pl.BlockSpec((N_PAIRS, LANES, 2 * RET_VAL_DIM), lambda b, i: (0, 0, 0)),
        ],
        out_specs=pl.BlockSpec((cpt, N_PAIRS, LANES, 2 * RET_VAL_DIM),
                               lambda b, i: (rrow(b, i), 0, 0, 0)),
        out_shape=jax.ShapeDtypeStruct((batch * nt * cpt, N_PAIRS, LANES, 2 * RET_VAL_DIM), BF16),
        scratch_shapes=[pltpu.VMEM((N_PAIRS, LANES, 2 * RET_VAL_DIM), F32)],
        compiler_params=pltpu.CompilerParams(
            dimension_semantics=("arbitrary", "arbitrary"),
            vmem_limit_bytes=VMEM_LIMIT),
        name="ret_states",
    )(proj, proj, kdec_b, g_b)


def _ret_kernel(q_ref, k_ref, v_ref, g_ref, pb_ref, d_ref,
                qdf_ref, qdb_ref, kdf_ref, gf_ref, gnw_ref, o_ref, st_ref):
    @pl.when(pl.program_id(1) == 0)
    def _():
        st_ref[...] = jnp.zeros_like(st_ref)

    lane = lax.broadcasted_iota(jnp.int32, (CHUNK, LANES), 1)
    low_half = lane < RET_KEY_DIM
    vcol = lax.broadcasted_iota(jnp.int32, (CHUNK, 2 * RET_VAL_DIM), 1)
    first_v = vcol < RET_VAL_DIM
    def scores(item):
        ch, c = item
        rows = slice(CHUNK * ch, CHUNK * (ch + 1))
        kb = k_ref[rows, LANES * c:LANES * (c + 1)]
        q = q_ref[rows, LANES * c:LANES * (c + 1)]
        out = []
        for half in range(2):
            sel = low_half if half == 0 else jnp.logical_not(low_half)
            qh = jnp.where(sel, q, jnp.zeros_like(q))
            out.append(lax.dot_general(qh, kb, (((1,), (1,)), ((), ())),
                                       preferred_element_type=F32))
        return out

    def finish(item, s_pair):
        ch, c = item
        rows = slice(CHUNK * ch, CHUNK * (ch + 1))
        q = q_ref[rows, LANES * c:LANES * (c + 1)].astype(F32)
        k = k_ref[rows, LANES * c:LANES * (c + 1)].astype(F32)
        vpair = v_ref[rows, 2 * RET_VAL_DIM * c:2 * RET_VAL_DIM * (c + 1)]
        parts = [(s_pair[half] * d_ref[2 * c + half]).astype(BF16) for half in range(2)]
        parts.append((q * qdf_ref[c]).astype(BF16))
        parts.append((q * qdb_ref[c]).astype(BF16))
        lhs = jnp.concatenate(parts, axis=1)
        zero_v = jnp.zeros_like(vpair)
        rhs = jnp.concatenate([
            jnp.where(first_v, vpair, zero_v),
            jnp.where(first_v, zero_v, vpair),
            st_ref[c].astype(BF16),
            pb_ref[ch, c],
        ], axis=0)
        o = jnp.dot(lhs, rhs, preferred_element_type=F32)
        _state_update(st_ref, c, k, kdf_ref[c], gf_ref[c], vpair)
        for half in range(2):
            lo_c = RET_VAL_DIM * (2 * c + half)
            oh = o[:, RET_VAL_DIM * half:RET_VAL_DIM * (half + 1)]
            mu = jnp.mean(oh, axis=-1, keepdims=True)
            dev = oh - mu
            var = jnp.mean(dev * dev, axis=-1, keepdims=True)
            y = dev * lax.rsqrt(var + EPS) * gnw_ref[:, lo_c:lo_c + RET_VAL_DIM]
            g = g_ref[rows, lo_c:lo_c + RET_VAL_DIM].astype(F32)
            o_ref[rows, lo_c:lo_c + RET_VAL_DIM] = (y * g).astype(o_ref.dtype)

    items = [(ch, c) for ch in range(q_ref.shape[0] // CHUNK) for c in range(N_PAIRS)]
    s_next = scores(items[0])
    for idx, item in enumerate(items):
        s_cur = s_next
        if idx + 1 < len(items):
            s_next = scores(items[idx + 1])
        finish(item, s_cur)


def _retention(proj, pb, d_t, qdec_f, qdec_b, kdec_f, g_f, gn_w, batch, seq,
               tt=512):
    nt = seq // tt
    cpt = tt // CHUNK
    row = lambda b, i: b * nt + i
    const3 = lambda b, i: (0, 0, 0)
    return pl.pallas_call(
        _ret_kernel,
        grid=(batch, nt),
        in_specs=[
            pl.BlockSpec((tt, RET_QK_COLS), lambda b, i: (row(b, i), OFF_QR // 512)),
            pl.BlockSpec((tt, RET_QK_COLS), lambda b, i: (row(b, i), OFF_KR // 512)),
            pl.BlockSpec((tt, RET_WIDTH), lambda b, i: (row(b, i), OFF_VR // 1024)),
            pl.BlockSpec((tt, RET_WIDTH), lambda b, i: (row(b, i), OFF_GR // 1024)),
            pl.BlockSpec((cpt, N_PAIRS, LANES, 2 * RET_VAL_DIM), lambda b, i: (row(b, i), 0, 0, 0)),
            pl.BlockSpec((RET_HEADS, CHUNK, CHUNK), const3),
            pl.BlockSpec((N_PAIRS, CHUNK, LANES), const3),
            pl.BlockSpec((N_PAIRS, CHUNK, LANES), const3),
            pl.BlockSpec((N_PAIRS, CHUNK, LANES), const3),
            pl.BlockSpec((N_PAIRS, LANES, 2 * RET_VAL_DIM), const3),
            pl.BlockSpec((1, RET_WIDTH), lambda b, i: (0, 0)),
        ],
        out_specs=pl.BlockSpec((tt, RET_WIDTH), lambda b, i: (row(b, i), 0)),
        out_shape=jax.ShapeDtypeStruct((batch * seq, RET_WIDTH), BF16),
        scratch_shapes=[pltpu.VMEM((N_PAIRS, LANES, 2 * RET_VAL_DIM), F32)],
        compiler_params=pltpu.CompilerParams(
            dimension_semantics=("arbitrary", "arbitrary"),
            vmem_limit_bytes=VMEM_LIMIT),
        name="retention",
    )(proj, proj, proj, proj, pb, d_t, qdec_f, qdec_b, kdec_f, g_f, gn_w)


def _outproj_kernel(a_ref, g_ref, r_ref, x_ref, anw_ref, w_ref, fw_ref, o_ref, *, row_chunks):
    rc = x_ref.shape[0] // row_chunks
    for r in range(row_chunks):
        rows = slice(rc * r, rc * (r + 1))
        a = a_ref[rows, :].astype(F32)
        ms = jnp.mean(a * a, axis=-1, keepdims=True)
        a = a * lax.rsqrt(ms + EPS) * anw_ref[...]
        am = (a * g_ref[rows, :].astype(F32)).astype(BF16)
        mixed = jnp.concatenate([r_ref[rows, :], am], axis=1)
        y = x_ref[rows, :] + jnp.dot(mixed, w_ref[...], preferred_element_type=F32)
        ms = jnp.mean(y * y, axis=-1, keepdims=True)
        o_ref[rows, :] = y * lax.rsqrt(ms + EPS) * fw_ref[...]


def _outproj(attn_o, proj, ret_o, x2, attn_nw, w_o_p, final_w, tm=512, row_chunks=2):
    m = x2.shape[0]
    return pl.pallas_call(
        functools.partial(_outproj_kernel, row_chunks=row_chunks),
        grid=(m // tm,),
        in_specs=[
            pl.BlockSpec((tm, ATTN_WIDTH), lambda i: (i, 0)),
            pl.BlockSpec((tm, ATTN_WIDTH), lambda i: (i, OFF_GA // 1024)),
            pl.BlockSpec((tm, RET_WIDTH), lambda i: (i, 0)),
            pl.BlockSpec((tm, D_MODEL), lambda i: (i, 0)),
            pl.BlockSpec((1, ATTN_WIDTH), lambda i: (0, 0)),
            pl.BlockSpec((RET_WIDTH + ATTN_WIDTH, D_MODEL), lambda i: (0, 0)),
            pl.BlockSpec((1, D_MODEL), lambda i: (0, 0)),
        ],
        out_specs=pl.BlockSpec((tm, D_MODEL), lambda i: (i, 0)),
        out_shape=jax.ShapeDtypeStruct((m, D_MODEL), F32),
        compiler_params=pltpu.CompilerParams(
            dimension_semantics=("arbitrary",),
            vmem_limit_bytes=VMEM_LIMIT),
        name="outproj",
    )(attn_o, proj, ret_o, x2, attn_nw, w_o_p, final_w)


def _pair_heads(a, axis):
    a = jnp.moveaxis(a, axis, -1)
    lead = a.shape[:-1]
    d = a.shape[-1] // ATTN_HEADS
    a = a.reshape(lead + (2, 2, ATTN_GROUP, d))
    a = jnp.swapaxes(a, -3, -2)
    a = a.reshape(lead + (ATTN_HEADS * d,))
    return jnp.moveaxis(a, -1, axis)


def _t5_bucket(rel):
    nb = N_BUCKETS // 2
    max_exact = nb // 2
    ret = jnp.where(rel > 0, nb, 0)
    n = jnp.abs(rel)
    nf = jnp.maximum(n, 1).astype(F32)
    large = max_exact + (jnp.log(nf / max_exact) / math.log(MAX_DISTANCE / max_exact)
                         * (nb - max_exact)).astype(jnp.int32)
    large = jnp.minimum(large, nb - 1)
    return ret + jnp.where(n < max_exact, n, large)


def kernel(x, norm_w, w_in, attn_sink, rel_bias, attn_out_norm_w, ret_decay_fwd,
           ret_decay_bwd, ret_gn_w, w_out, final_norm_w):
    batch, seq, _ = x.shape
    assert norm_w.shape[0] == 1 and seq % BLOCK == 0
    x2 = x.reshape(batch * seq, D_MODEL)

    w = w_in[0]
    o = np.cumsum([0, ATTN_WIDTH, ATTN_KV_COLS, ATTN_KV_COLS, ATTN_WIDTH,
                   RET_QK_COLS, RET_QK_COLS, RET_WIDTH, RET_WIDTH])
    w_qa, w_ka, w_va, w_ga, w_qr, w_kr, w_vr, w_gr = [w[:, o[t]:o[t + 1]] for t in range(8)]
    w_in_p = jnp.concatenate([
        _pair_heads(w_qa, 1) * (ATTN_HEAD_DIM ** -0.5 * LOG2E),
        _pair_heads(w_ga, 1), w_vr, w_gr, w_ka, w_va,
        w_qr,
        w_kr * (RET_KEY_DIM ** -0.5),
    ], axis=1).astype(BF16)
    w_o = w_out[0]
    w_o_p = jnp.concatenate([w_o[ATTN_WIDTH:], _pair_heads(w_o[:ATTN_WIDTH], 0)],
                            axis=0).astype(BF16)
    attn_nw_p = _pair_heads(attn_out_norm_w[0], 0).reshape(1, ATTN_WIDTH)

    qi = jnp.arange(BLOCK, dtype=jnp.int32)[:, None]
    kt = jnp.arange(KW, dtype=jnp.int32)[None, :]
    rel = kt - BLOCK - qi
    onehot = (_t5_bucket(rel)[:, :, None] == jnp.arange(N_BUCKETS)[None, None, :]).astype(F32)
    bias = jnp.einsum("qkn,nh->qkh", onehot, rel_bias.astype(F32),
                      precision=lax.Precision.HIGHEST)
    bias = jnp.where((jnp.abs(rel) <= WINDOW)[:, :, None], bias * LOG2E, NEG)
    bias = bias.reshape(BLOCK, KW, 2, 2, ATTN_GROUP)
    bias_t = bias.transpose(2, 3, 1, 4, 0).reshape(ATTN_KV_HEADS, KW, ATTN_GROUP * BLOCK)
    sink = (attn_sink[0].astype(F32) * LOG2E).reshape(ATTN_KV_HEADS, 1, ATTN_GROUP, 1)
    sink_t = jnp.broadcast_to(sink, (ATTN_KV_HEADS, 1, ATTN_GROUP, BLOCK)).reshape(
        ATTN_KV_HEADS, 1, ATTN_GROUP * BLOCK)

    inv = ROPE_BASE ** (-jnp.arange(0, RET_KEY_DIM, 2, dtype=F32) / RET_KEY_DIM)
    ang = jnp.arange(seq, dtype=F32)[:, None] * inv[None, :]
    reps = LANES // RET_KEY_DIM
    cos_t = jnp.tile(jnp.repeat(jnp.cos(ang), 2, axis=1), (1, reps))
    sgn = jnp.tile(jnp.array([-1.0, 1.0], F32), LANES // 2)
    sin_t = jnp.tile(jnp.repeat(jnp.sin(ang), 2, axis=1), (1, reps)) * sgn[None, :]

    lg_f = jax.nn.log_sigmoid(ret_decay_fwd[0].astype(F32))
    lg_b = jax.nn.log_sigmoid(ret_decay_bwd[0].astype(F32))
    idx = jnp.arange(CHUNK, dtype=F32)
    diff = idx[:, None] - idx[None, :]
    d_f = jnp.where((diff >= 0)[None], jnp.exp(lg_f[:, None, None] * jnp.maximum(diff, 0.0)[None]), 0.0)
    d_b = jnp.where((diff < 0)[None], jnp.exp(lg_b[:, None, None] * jnp.maximum(-diff, 0.0)[None]), 0.0)
    d_t = d_f + d_b

    def lane_table(lg, expo):
        per_head = jnp.exp(lg[None, :] * expo[:, None])
        t = jnp.repeat(per_head, RET_KEY_DIM, axis=1)
        return t.reshape(CHUNK, N_PAIRS, LANES).transpose(1, 0, 2)

    qdec_f = lane_table(lg_f, idx + 1.0)
    qdec_b = lane_table(lg_b, CHUNK - idx)
    kdec_f = lane_table(lg_f, CHUNK - 1.0 - idx)
    kdec_b = lane_table(lg_b, idx)

    def state_decay(lg):
        cd = jnp.exp(lg * CHUNK).reshape(N_PAIRS, 2)
        rows = jnp.repeat(cd, RET_KEY_DIM, axis=1)
        r = np.arange(LANES)[:, None] // RET_KEY_DIM
        m = np.arange(2 * RET_VAL_DIM)[None, :] // RET_VAL_DIM
        return jnp.where(jnp.asarray(r == m)[None], rows[:, :, None], 0.0)

    g_f = state_decay(lg_f)
    g_b = state_decay(lg_b)

    proj = _inproj(x2, norm_w[0].reshape(1, D_MODEL), w_in_p, cos_t, sin_t, seq)
    attn_o = _attention(proj, bias_t, sink_t, batch, seq)
    pb = _retention_states(proj, kdec_b, g_b, batch, seq)
    ret_o = _retention(proj, pb, d_t, qdec_f, qdec_b, kdec_f, g_f,
                       ret_gn_w[0].reshape(1, RET_WIDTH), batch, seq)
    out = _outproj(attn_o, proj, ret_o, x2, attn_nw_p, w_o_p,
                   final_norm_w.reshape(1, D_MODEL))
    return out.reshape(batch, seq, D_MODEL)
```

```python
import functools
import math

import numpy as np
import jax
import jax.numpy as jnp
from jax import lax
from jax.experimental import pallas as pl
from jax.experimental.pallas import tpu as pltpu

D_MODEL = 2048
ATTN_HEAD_DIM = 64
ATTN_WIDTH = 1024
ATTN_HEADS = 16
ATTN_KV_HEADS = 4
ATTN_GROUP = 4
ATTN_KV_COLS = 256
WINDOW = 128
BLOCK = 128
N_BUCKETS = 32
MAX_DISTANCE = 128
RET_WIDTH = 1024
RET_HEADS = 8
RET_VAL_DIM = 128
RET_KEY_DIM = 64
RET_QK_COLS = 512
CHUNK = 128
ROPE_BASE = 10000.0
EPS = 1e-6
NEG = -1e30
LOG2E = math.log2(math.e)

LANES = 128
N_PAIRS = RET_HEADS // 2
KW = 3 * BLOCK
CB = 512

OFF_QA = 0
OFF_KA = OFF_QA + ATTN_WIDTH
OFF_GA = OFF_KA + 2 * ATTN_KV_COLS
OFF_QR = OFF_GA + ATTN_WIDTH
OFF_KR = OFF_QR + RET_QK_COLS
OFF_VR = OFF_KR + RET_QK_COLS
OFF_GR = OFF_VR + RET_WIDTH
IN_WIDTH = OFF_GR + RET_WIDTH

VMEM_LIMIT = 56 * 1024 * 1024

F32 = jnp.float32
BF16 = jnp.bfloat16


def _roll_half(x):
    u = pltpu.bitcast(x, jnp.uint32)
    return pltpu.bitcast(pltpu.roll(u, LANES // 2, axis=1), x.dtype)


def _silu(g):
    return g / (1.0 + jnp.exp(-g))


def _inproj_kernel(x_ref, nw_ref, w_ref, inv_ref, sgn_ref, o_ref, *, row_chunks, tiles_per_seq):
    tm = x_ref.shape[0]
    rc = tm // row_chunks
    nw = nw_ref[...]
    tile_pos = (pl.program_id(0) % tiles_per_seq) * tm
    lane = lax.broadcasted_iota(jnp.int32, (rc, LANES), 1)
    even = (lane % 2) == 0
    for r in range(row_chunks):
        rows = slice(rc * r, rc * (r + 1))
        x = x_ref[rows, :]
        ms = jnp.mean(x * x, axis=-1, keepdims=True)
        h = (x * lax.rsqrt(ms + EPS) * nw).astype(BF16)
        pos = (tile_pos + rc * r + lax.broadcasted_iota(jnp.int32, (rc, LANES), 0)).astype(F32)
        ang = pos * inv_ref[...]
        cos = jnp.cos(ang)
        sin_signed = jnp.sin(ang) * sgn_ref[...]
        for col in range(0, IN_WIDTH, CB):
            res = jnp.dot(h, w_ref[:, col:col + CB], preferred_element_type=F32)
            if OFF_GA <= col < OFF_QR or col >= OFF_GR:
                res = _silu(res)
            elif OFF_QR <= col < OFF_VR:
                parts = []
                for c in range(CB // LANES):
                    v = res[:, LANES * c:LANES * (c + 1)]
                    partner = jnp.where(even, pltpu.roll(v, LANES - 1, axis=1),
                                        pltpu.roll(v, 1, axis=1))
                    parts.append(v * cos + partner * sin_signed)
                res = jnp.concatenate(parts, axis=1)
            o_ref[rows, col:col + CB] = res.astype(o_ref.dtype)


def _inproj(x2, norm_w, w_in_p, inv_full, sgn, seq, tm=512, row_chunks=2):
    m = x2.shape[0]
    return pl.pallas_call(
        functools.partial(_inproj_kernel, row_chunks=row_chunks, tiles_per_seq=seq // tm),
        grid=(m // tm,),
        in_specs=[
            pl.BlockSpec((tm, D_MODEL), lambda i: (i, 0)),
            pl.BlockSpec((1, D_MODEL), lambda i: (0, 0)),
            pl.BlockSpec((D_MODEL, IN_WIDTH), lambda i: (0, 0)),
            pl.BlockSpec((1, LANES), lambda i: (0, 0)),
            pl.BlockSpec((1, LANES), lambda i: (0, 0)),
        ],
        out_specs=pl.BlockSpec((tm, IN_WIDTH), lambda i: (i, 0)),
        out_shape=jax.ShapeDtypeStruct((m, IN_WIDTH), BF16),
        compiler_params=pltpu.CompilerParams(
            dimension_semantics=("arbitrary",),
            vmem_limit_bytes=VMEM_LIMIT),
        name="inproj",
    )(x2, norm_w, w_in_p, inv_full, sgn)


def _attn_kernel(q_ref, kvp_ref, kvc_ref, kvn_ref, bias_ref, sink_ref, o_ref):
    i = pl.program_id(1)
    nt = pl.num_programs(1)
    n_blk = q_ref.shape[0] // BLOCK
    hd = ATTN_HEAD_DIM
    thr_in = jnp.float32(0.5 * NEG)
    thr_first = jnp.where(i > 0, 0.5 * NEG, -NEG).astype(F32)
    thr_last = jnp.where(i < nt - 1, 0.5 * NEG, -NEG).astype(F32)
    kv = jnp.concatenate([kvp_ref[...], kvc_ref[...], kvn_ref[...]], axis=0)
    lane = lax.broadcasted_iota(jnp.int32, (kv.shape[0], LANES), 1)
    low = lane < hd

    k_side, v_side = {}, {}
    for j in range(ATTN_KV_COLS // LANES):
        kj = kv[:, LANES * j:LANES * (j + 1)]
        vj = kv[:, ATTN_KV_COLS + LANES * j:ATTN_KV_COLS + LANES * (j + 1)]
        for hg in range(2):
            mine = low if hg == 0 else jnp.logical_not(low)
            k_here = jnp.where(mine, kj, jnp.zeros_like(kj))
            v_here = jnp.where(mine, vj, jnp.ones_like(vj))
            k_side[(2 * j + hg, hg)] = k_here
            v_side[(2 * j + hg, hg)] = v_here
            k_side[(2 * j + hg, 1 - hg)] = _roll_half(k_here)
            v_side[(2 * j + hg, 1 - hg)] = _roll_half(v_here)

    def scores(item):
        blk, g = item
        win = slice(BLOCK * blk, BLOCK * blk + KW)
        qs = jnp.concatenate(
            [q_ref[BLOCK * blk:BLOCK * (blk + 1), LANES * c:LANES * (c + 1)]
             for c in (2 * g, 2 * g + 1)], axis=0)
        return [lax.dot_general(k_side[(g, half)][win], qs, (((1,), (1,)), ((), ())),
                                preferred_element_type=F32) for half in range(2)]

    def finish(item, st_pair):
        blk, g = item
        win = slice(BLOCK * blk, BLOCK * blk + KW)
        thr_l = thr_first if blk == 0 else thr_in
        thr_r = thr_last if blk == n_blk - 1 else thr_in
        st = jnp.concatenate(st_pair, axis=1)
        b = bias_ref[g]
        t = jnp.concatenate([
            jnp.where(b[:BLOCK] > thr_l, st[:BLOCK] + b[:BLOCK], NEG),
            st[BLOCK:2 * BLOCK] + b[BLOCK:2 * BLOCK],
            jnp.where(b[2 * BLOCK:] > thr_r, st[2 * BLOCK:] + b[2 * BLOCK:], NEG),
        ], axis=0)
        sk = sink_ref[g]
        m = jnp.maximum(jnp.max(t, axis=0, keepdims=True), sk)
        p = jnp.exp2(t - m).astype(BF16)
        e_sink = jnp.exp2(sk - m)
        outs = []
        for half in range(2):
            cols = slice(2 * BLOCK * half, 2 * BLOCK * (half + 1))
            ot = lax.dot_general(v_side[(g, half)][win], p[:, cols], (((0,), (0,)), ((), ())),
                                 preferred_element_type=F32)
            if half == 0:
                num, den = ot[:hd], ot[hd:hd + 1]
            else:
                num, den = ot[hd:], ot[0:1]
            outs.append(num / (den + e_sink[:, cols]))
        out_t = jnp.concatenate(outs, axis=0)
        for r in range(2):
            c = 2 * g + r
            o_ref[BLOCK * blk:BLOCK * (blk + 1), LANES * c:LANES * (c + 1)] = (
                out_t[:, BLOCK * r:BLOCK * (r + 1)].T.astype(o_ref.dtype))

    items = [(blk, g) for blk in range(n_blk) for g in range(ATTN_KV_HEADS)]
    st_next = scores(items[0])
    for idx, item in enumerate(items):
        st_cur = st_next
        if idx + 1 < len(items):
            st_next = scores(items[idx + 1])
        finish(item, st_cur)


def _attention(proj, bias_t, sink_t, batch, seq, tq=512):
    nt = seq // tq
    bpt = tq // BLOCK
    nb = seq // BLOCK
    kvblk = OFF_KA // CB
    n_kv = ATTN_KV_HEADS
    prev_blk = lambda b, i: (b * nb + jnp.maximum(bpt * i - 1, 0), kvblk)
    next_blk = lambda b, i: (b * nb + jnp.minimum(bpt * (i + 1), nb - 1), kvblk)
    return pl.pallas_call(
        _attn_kernel,
        grid=(batch, nt),
        in_specs=[
            pl.BlockSpec((tq, ATTN_WIDTH), lambda b, i: (b * nt + i, OFF_QA // ATTN_WIDTH)),
            pl.BlockSpec((BLOCK, CB), prev_blk),
            pl.BlockSpec((tq, CB), lambda b, i: (b * nt + i, kvblk)),
            pl.BlockSpec((BLOCK, CB), next_blk),
            pl.BlockSpec((n_kv, KW, ATTN_GROUP * BLOCK), lambda b, i: (0, 0, 0)),
            pl.BlockSpec((n_kv, 1, ATTN_GROUP * BLOCK), lambda b, i: (0, 0, 0)),
        ],
        out_specs=pl.BlockSpec((tq, ATTN_WIDTH), lambda b, i: (b * nt + i, 0)),
        out_shape=jax.ShapeDtypeStruct((batch * seq, ATTN_WIDTH), BF16),
        compiler_params=pltpu.CompilerParams(
            dimension_semantics=("arbitrary", "arbitrary"),
            vmem_limit_bytes=VMEM_LIMIT),
        name="attn",
    )(proj, proj, proj, proj, bias_t, sink_t)


def _state_mask():
    r = lax.broadcasted_iota(jnp.int32, (LANES, 2 * RET_VAL_DIM), 0)
    m = lax.broadcasted_iota(jnp.int32, (LANES, 2 * RET_VAL_DIM), 1)
    return (r // RET_KEY_DIM) == (m // RET_VAL_DIM)


def _state_update(st_ref, c, k_rot, kdec, gdec, vpair):
    kd = (k_rot * kdec).astype(BF16)
    upd = lax.dot_general(kd, vpair, (((0,), (0,)), ((), ())),
                          preferred_element_type=F32)
    st_ref[c] = st_ref[c] * gdec + jnp.where(_state_mask(), upd, 0.0)


def _pair_cols(refs, rows, c, width):
    per_ref = CB // width
    return refs[c // per_ref][rows, width * (c % per_ref):width * (c % per_ref + 1)]


def _retstate_kernel(k_ref, v0_ref, v1_ref, kdb_ref, gb_ref, pb_ref, st_ref):
    @pl.when(pl.program_id(1) == 0)
    def _():
        st_ref[...] = jnp.zeros_like(st_ref)

    n_ch = k_ref.shape[0] // CHUNK
    for ch in reversed(range(n_ch)):
        rows = slice(CHUNK * ch, CHUNK * (ch + 1))
        pb_ref[ch] = st_ref[...].astype(BF16)
        for c in range(N_PAIRS):
            k = k_ref[rows, LANES * c:LANES * (c + 1)].astype(F32)
            vpair = _pair_cols((v0_ref, v1_ref), rows, c, 2 * RET_VAL_DIM)
            _state_update(st_ref, c, k, kdb_ref[c], gb_ref[c], vpair)


def _retention_states(proj, kdec_b, g_b, batch, seq, tt=512):
    nt = seq // tt
    cpt = tt // CHUNK
    rrow = lambda b, i: b * nt + (nt - 1 - i)
    col = lambda off: (lambda b, i: (rrow(b, i), off // CB))
    return pl.pallas_call(
        _retstate_kernel,
        grid=(batch, nt),
        in_specs=[
            pl.BlockSpec((tt, CB), col(OFF_KR)),
            pl.BlockSpec((tt, CB), col(OFF_VR)),
            pl.BlockSpec((tt, CB), col(OFF_VR + CB)),
            pl.BlockSpec((N_PAIRS, CHUNK, LANES), lambda b, i: (0, 0, 0)),
            pl.BlockSpec((N_PAIRS, LANES, 2 * RET_VAL_DIM), lambda b, i: (0, 0, 0)),
        ],
        out_specs=pl.BlockSpec((cpt, N_PAIRS, LANES, 2 * RET_VAL_DIM),
                               lambda b, i: (rrow(b, i), 0, 0, 0)),
        out_shape=jax.ShapeDtypeStruct((batch * nt * cpt, N_PAIRS, LANES, 2 * RET_VAL_DIM), BF16),
        scratch_shapes=[pltpu.VMEM((N_PAIRS, LANES, 2 * RET_VAL_DIM), F32)],
        compiler_params=pltpu.CompilerParams(
            dimension_semantics=("arbitrary", "arbitrary"),
            vmem_limit_bytes=VMEM_LIMIT),
        name="ret_states",
    )(proj, proj, proj, kdec_b, g_b)


def _ret_kernel(q_ref, k_ref, v0_ref, v1_ref, g0_ref, g1_ref, pb_ref, d_ref,
                qdf_ref, qdb_ref, kdf_ref, gf_ref, gnw_ref, o_ref, st_ref):
    @pl.when(pl.program_id(1) == 0)
    def _():
        st_ref[...] = jnp.zeros_like(st_ref)

    lane = lax.broadcasted_iota(jnp.int32, (CHUNK, LANES), 1)
    low_half = lane < RET_KEY_DIM
    vcol = lax.broadcasted_iota(jnp.int32, (CHUNK, 2 * RET_VAL_DIM), 1)
    first_v = vcol < RET_VAL_DIM

    def scores(item):
        ch, c = item
        rows = slice(CHUNK * ch, CHUNK * (ch + 1))
        kb = k_ref[rows, LANES * c:LANES * (c + 1)]
        q = q_ref[rows, LANES * c:LANES * (c + 1)]
        out = []
        for half in range(2):
            sel = low_half if half == 0 else jnp.logical_not(low_half)
            qh = jnp.where(sel, q, jnp.zeros_like(q))
            out.append(lax.dot_general(qh, kb, (((1,), (1,)), ((), ())),
                                       preferred_element_type=F32))
        return out

    def finish(item, s_pair):
        ch, c = item
        rows = slice(CHUNK * ch, CHUNK * (ch + 1))
        q = q_ref[rows, LANES * c:LANES * (c + 1)].astype(F32)
        k = k_ref[rows, LANES * c:LANES * (c + 1)].astype(F32)
        vpair = _pair_cols((v0_ref, v1_ref), rows, c, 2 * RET_VAL_DIM)
        parts = [(s_pair[half] * d_ref[2 * c + half]).astype(BF16) for half in range(2)]
        parts.append((q * qdf_ref[c]).astype(BF16))
        parts.append((q * qdb_ref[c]).astype(BF16))
        lhs = jnp.concatenate(parts, axis=1)
        zero_v = jnp.zeros_like(vpair)
        rhs = jnp.concatenate([
            jnp.where(first_v, vpair, zero_v),
            jnp.where(first_v, zero_v, vpair),
            st_ref[c].astype(BF16),
            pb_ref[ch, c],
        ], axis=0)
        o = jnp.dot(lhs, rhs, preferred_element_type=F32)
        _state_update(st_ref, c, k, kdf_ref[c], gf_ref[c], vpair)
        for half in range(2):
            head = 2 * c + half
            lo_c = RET_VAL_DIM * head
            oh = o[:, RET_VAL_DIM * half:RET_VAL_DIM * (half + 1)]
            mu = jnp.mean(oh, axis=-1, keepdims=True)
            dev = oh - mu
            var = jnp.mean(dev * dev, axis=-1, keepdims=True)
            y = dev * lax.rsqrt(var + EPS) * gnw_ref[:, lo_c:lo_c + RET_VAL_DIM]
            g = _pair_cols((g0_ref, g1_ref), rows, head, RET_VAL_DIM).astype(F32)
            o_ref[rows, lo_c:lo_c + RET_VAL_DIM] = (y * g).astype(o_ref.dtype)

    items = [(ch, c) for ch in range(q_ref.shape[0] // CHUNK) for c in range(N_PAIRS)]
    s_next = scores(items[0])
    for idx, item in enumerate(items):
        s_cur = s_next
        if idx + 1 < len(items):
            s_next = scores(items[idx + 1])
        finish(item, s_cur)


def _retention(proj, pb, d_t, qdec_f, qdec_b, kdec_f, g_f, gn_w, batch, seq, tt=512):
    nt = seq // tt
    cpt = tt // CHUNK
    row = lambda b, i: b * nt + i
    col = lambda off: (lambda b, i: (row(b, i), off // CB))
    const3 = lambda b, i: (0, 0, 0)
    return pl.pallas_call(
        _ret_kernel,
        grid=(batch, nt),
        in_specs=[
            pl.BlockSpec((tt, CB), col(OFF_QR)),
            pl.BlockSpec((tt, CB), col(OFF_KR)),
            pl.BlockSpec((tt, CB), col(OFF_VR)),
            pl.BlockSpec((tt, CB), col(OFF_VR + CB)),
            pl.BlockSpec((tt, CB), col(OFF_GR)),
            pl.BlockSpec((tt, CB), col(OFF_GR + CB)),
            pl.BlockSpec((cpt, N_PAIRS, LANES, 2 * RET_VAL_DIM), lambda b, i: (row(b, i), 0, 0, 0)),
            pl.BlockSpec((RET_HEADS, CHUNK, CHUNK), const3),
            pl.BlockSpec((N_PAIRS, CHUNK, LANES), const3),
            pl.BlockSpec((N_PAIRS, CHUNK, LANES), const3),
            pl.BlockSpec((N_PAIRS, CHUNK, LANES), const3),
            pl.BlockSpec((N_PAIRS, LANES, 2 * RET_VAL_DIM), const3),
            pl.BlockSpec((1, RET_WIDTH), lambda b, i: (0, 0)),
        ],
        out_specs=pl.BlockSpec((tt, RET_WIDTH), lambda b, i: (row(b, i), 0)),
        out_shape=jax.ShapeDtypeStruct((batch * seq, RET_WIDTH), BF16),
        scratch_shapes=[pltpu.VMEM((N_PAIRS, LANES, 2 * RET_VAL_DIM), F32)],
        compiler_params=pltpu.CompilerParams(
            dimension_semantics=("arbitrary", "arbitrary"),
            vmem_limit_bytes=VMEM_LIMIT),
        name="retention",
    )(proj, proj, proj, proj, proj, proj, pb, d_t, qdec_f, qdec_b, kdec_f, g_f, gn_w)


def _outproj_kernel(a_ref, g0_ref, g1_ref, r_ref, x_ref, anw_ref, w_ref, fw_ref, o_ref,
                    *, row_chunks):
    rc = x_ref.shape[0] // row_chunks
    for r in range(row_chunks):
        rows = slice(rc * r, rc * (r + 1))
        a = a_ref[rows, :].astype(F32)
        ms = jnp.mean(a * a, axis=-1, keepdims=True)
        a = a * lax.rsqrt(ms + EPS) * anw_ref[...]
        g = jnp.concatenate([g0_ref[rows, :], g1_ref[rows, :]], axis=1).astype(F32)
        mixed = jnp.concatenate([(a * g).astype(BF16), r_ref[rows, :]], axis=1)
        y = x_ref[rows, :] + jnp.dot(mixed, w_ref[...], preferred_element_type=F32)
        ms = jnp.mean(y * y, axis=-1, keepdims=True)
        o_ref[rows, :] = y * lax.rsqrt(ms + EPS) * fw_ref[...]


def _outproj(attn_o, proj, ret_o, x2, attn_nw, w_o_p, final_w, tm=512, row_chunks=2):
    m = x2.shape[0]
    return pl.pallas_call(
        functools.partial(_outproj_kernel, row_chunks=row_chunks),
        grid=(m // tm,),
        in_specs=[
            pl.BlockSpec((tm, ATTN_WIDTH), lambda i: (i, 0)),
            pl.BlockSpec((tm, CB), lambda i: (i, OFF_GA // CB)),
            pl.BlockSpec((tm, CB), lambda i: (i, OFF_GA // CB + 1)),
            pl.BlockSpec((tm, RET_WIDTH), lambda i: (i, 0)),
            pl.BlockSpec((tm, D_MODEL), lambda i: (i, 0)),
            pl.BlockSpec((1, ATTN_WIDTH), lambda i: (0, 0)),
            pl.BlockSpec((ATTN_WIDTH + RET_WIDTH, D_MODEL), lambda i: (0, 0)),
            pl.BlockSpec((1, D_MODEL), lambda i: (0, 0)),
        ],
        out_specs=pl.BlockSpec((tm, D_MODEL), lambda i: (i, 0)),
        out_shape=jax.ShapeDtypeStruct((m, D_MODEL), F32),
        compiler_params=pltpu.CompilerParams(
            dimension_semantics=("arbitrary",),
            vmem_limit_bytes=VMEM_LIMIT),
        name="outproj",
    )(attn_o, proj, proj, ret_o, x2, attn_nw, w_o_p, final_w)


def _t5_bucket(rel):
    nb = N_BUCKETS // 2
    max_exact = nb // 2
    ret = jnp.where(rel > 0, nb, 0)
    n = jnp.abs(rel)
    nf = jnp.maximum(n, 1).astype(F32)
    large = max_exact + (jnp.log(nf / max_exact) / math.log(MAX_DISTANCE / max_exact)
                         * (nb - max_exact)).astype(jnp.int32)
    large = jnp.minimum(large, nb - 1)
    return ret + jnp.where(n < max_exact, n, large)


def kernel(x, norm_w, w_in, attn_sink, rel_bias, attn_out_norm_w, ret_decay_fwd,
           ret_decay_bwd, ret_gn_w, w_out, final_norm_w):
    batch, seq, _ = x.shape
    assert norm_w.shape[0] == 1 and seq % BLOCK == 0
    x2 = x.reshape(batch * seq, D_MODEL)

    col_scale = np.ones((1, IN_WIDTH), np.float32)
    col_scale[:, OFF_QA:OFF_QA + ATTN_WIDTH] = ATTN_HEAD_DIM ** -0.5 * LOG2E
    col_scale[:, OFF_KR:OFF_KR + RET_QK_COLS] = RET_KEY_DIM ** -0.5
    w_in_p = (w_in[0] * col_scale).astype(BF16)
    w_o_p = w_out[0].astype(BF16)

    qi = jnp.arange(BLOCK, dtype=jnp.int32)[None, None, None, :]
    kt = jnp.arange(KW, dtype=jnp.int32)[:, None, None, None]
    rel = kt - BLOCK - qi
    bucket = _t5_bucket(rel)
    rb = rel_bias.astype(F32).reshape(N_BUCKETS, ATTN_KV_HEADS, 2, 2)
    rb = jnp.swapaxes(rb, 2, 3)
    bias_t = jnp.zeros((ATTN_KV_HEADS, KW, 2, 2, BLOCK), F32)
    for n in range(N_BUCKETS):
        bias_t = jnp.where((bucket == n)[None], rb[n][:, None, :, :, None], bias_t)
    bias_t = jnp.where((jnp.abs(rel) <= WINDOW)[None], bias_t * LOG2E, NEG)
    bias_t = bias_t.reshape(ATTN_KV_HEADS, KW, ATTN_GROUP * BLOCK)
    sink = (attn_sink[0].astype(F32) * LOG2E).reshape(ATTN_KV_HEADS, 1, 2, 2)
    sink = jnp.swapaxes(sink, 2, 3)[..., None]
    sink_t = jnp.broadcast_to(sink, (ATTN_KV_HEADS, 1, 2, 2, BLOCK)).reshape(
        ATTN_KV_HEADS, 1, ATTN_GROUP * BLOCK)

    inv = ROPE_BASE ** (-jnp.arange(0, RET_KEY_DIM, 2, dtype=F32) / RET_KEY_DIM)
    inv_full = jnp.tile(jnp.repeat(inv, 2), LANES // RET_KEY_DIM).reshape(1, LANES)
    sgn = jnp.asarray(np.tile(np.array([-1.0, 1.0], np.float32), LANES // 2).reshape(1, LANES))

    lg_f = jax.nn.log_sigmoid(ret_decay_fwd[0].astype(F32))
    lg_b = jax.nn.log_sigmoid(ret_decay_bwd[0].astype(F32))
    idx = jnp.arange(CHUNK, dtype=F32)
    diff = idx[:, None] - idx[None, :]
    d_f = jnp.where((diff >= 0)[None], jnp.exp(lg_f[:, None, None] * jnp.maximum(diff, 0.0)[None]), 0.0)
    d_b = jnp.where((diff < 0)[None], jnp.exp(lg_b[:, None, None] * jnp.maximum(-diff, 0.0)[None]), 0.0)
    d_t = d_f + d_b

    def lane_table(lg, expo):
        per_head = jnp.exp(lg[None, :] * expo[:, None])
        t = jnp.repeat(per_head, RET_KEY_DIM, axis=1)
        return t.reshape(CHUNK, N_PAIRS, LANES).transpose(1, 0, 2)

    qdec_f = lane_table(lg_f, idx + 1.0)
    qdec_b = lane_table(lg_b, CHUNK - idx)
    kdec_f = lane_table(lg_f, CHUNK - 1.0 - idx)
    kdec_b = lane_table(lg_b, idx)

    def state_decay(lg):
        cd = jnp.exp(lg * CHUNK).reshape(N_PAIRS, 2)
        rows = jnp.repeat(cd, RET_KEY_DIM, axis=1)
        r = np.arange(LANES)[:, None] // RET_KEY_DIM
        m = np.arange(2 * RET_VAL_DIM)[None, :] // RET_VAL_DIM
        return jnp.where(jnp.asarray(r == m)[None], rows[:, :, None], 0.0)

    g_f = state_decay(lg_f)
    g_b = state_decay(lg_b)

    proj = _inproj(x2, norm_w[0].reshape(1, D_MODEL), w_in_p, inv_full, sgn, seq)
    attn_o = _attention(proj, bias_t, sink_t, batch, seq)
    pb = _retention_states(proj, kdec_b, g_b, batch, seq)
    ret_o = _retention(proj, pb, d_t, qdec_f, qdec_b, kdec_f, g_f,
                       ret_gn_w[0].reshape(1, RET_WIDTH), batch, seq)
    out = _outproj(attn_o, proj, ret_o, x2, attn_out_norm_w[0].reshape(1, ATTN_WIDTH), w_o_p,
                   final_norm_w.reshape(1, D_MODEL))
    return out.reshape(batch, seq, D_MODEL)
```

```python
import functools
import math

import numpy as np
import jax
import jax.numpy as jnp
from jax import lax
from jax.experimental import pallas as pl
from jax.experimental.pallas import tpu as pltpu

D_MODEL = 2048
ATTN_HEAD_DIM = 64
ATTN_WIDTH = 1024
ATTN_HEADS = 16
ATTN_KV_HEADS = 4
ATTN_GROUP = 4
ATTN_KV_COLS = 256
WINDOW = 128
BLOCK = 128
N_BUCKETS = 32
MAX_DISTANCE = 128
RET_WIDTH = 1024
RET_HEADS = 8
RET_VAL_DIM = 128
RET_KEY_DIM = 64
RET_QK_COLS = 512
CHUNK = 128
ROPE_BASE = 10000.0
EPS = 1e-6
NEG = -1e30
LOG2E = math.log2(math.e)

LANES = 128
N_PAIRS = RET_HEADS // 2
KW = 3 * BLOCK
CB = 512

OFF_QA = 0
OFF_KA = OFF_QA + ATTN_WIDTH
OFF_GA = OFF_KA + 2 * ATTN_KV_COLS
OFF_QR = OFF_GA + ATTN_WIDTH
OFF_KR = OFF_QR + RET_QK_COLS
OFF_VR = OFF_KR + RET_QK_COLS
OFF_GR = OFF_VR + RET_WIDTH
IN_WIDTH = OFF_GR + RET_WIDTH

VMEM_LIMIT = 56 * 1024 * 1024

F32 = jnp.float32
BF16 = jnp.bfloat16


def _roll_half(x):
    u = pltpu.bitcast(x, jnp.uint32)
    return pltpu.bitcast(pltpu.roll(u, LANES // 2, axis=1), x.dtype)


def _silu(g):
    return g / (1.0 + jnp.exp(-g))


def _inproj_kernel(x_ref, nw_ref, w_ref, rowtab_ref, basetab_ref, o_ref, *, row_splits,
                   tiles_per_seq):
    tm = x_ref.shape[0]
    nw = nw_ref[...]
    tile_blk = (pl.program_id(0) % tiles_per_seq) * (tm // BLOCK)
    start = 0
    for rc in row_splits:
        rows = slice(start, start + rc)
        x = x_ref[rows, :]
        ms = jnp.mean(x * x, axis=-1, keepdims=True)
        h = (x * lax.rsqrt(ms + EPS) * nw).astype(BF16)
        ca, sa, sas = rowtab_ref[0, :rc, :], rowtab_ref[1, :rc, :], rowtab_ref[2, :rc, :]
        base = basetab_ref[tile_blk + start // BLOCK]
        cb, sb, sbs = base[0:1, :], base[1:2, :], base[2:3, :]
        cos = ca * cb - sa * sb
        sin_signed = sas * cb + ca * sbs
        lane = lax.broadcasted_iota(jnp.int32, (rc, LANES), 1)
        even = (lane % 2) == 0
        start += rc
        for col in range(0, IN_WIDTH, CB):
            res = jnp.dot(h, w_ref[:, col:col + CB], preferred_element_type=F32)
            if OFF_GA <= col < OFF_QR or col >= OFF_GR:
                res = _silu(res)
            elif OFF_QR <= col < OFF_VR:
                parts = []
                for c in range(CB // LANES):
                    v = res[:, LANES * c:LANES * (c + 1)]
                    partner = jnp.where(even, pltpu.roll(v, LANES - 1, axis=1),
                                        pltpu.roll(v, 1, axis=1))
                    parts.append(v * cos + partner * sin_signed)
                res = jnp.concatenate(parts, axis=1)
            o_ref[rows, col:col + CB] = res.astype(o_ref.dtype)


def _inproj(x2, norm_w, w_in_p, rowtab, basetab, seq, tm=512, row_splits=(256, 256)):
    m = x2.shape[0]
    assert sum(row_splits) == tm and max(row_splits) <= rowtab.shape[1]
    return pl.pallas_call(
        functools.partial(_inproj_kernel, row_splits=row_splits, tiles_per_seq=seq // tm),
        grid=(m // tm,),
        in_specs=[
            pl.BlockSpec((tm, D_MODEL), lambda i: (i, 0)),
            pl.BlockSpec((1, D_MODEL), lambda i: (0, 0)),
            pl.BlockSpec((D_MODEL, IN_WIDTH), lambda i: (0, 0)),
            pl.BlockSpec(rowtab.shape, lambda i: (0, 0, 0)),
            pl.BlockSpec(basetab.shape, lambda i: (0, 0, 0)),
        ],
        out_specs=pl.BlockSpec((tm, IN_WIDTH), lambda i: (i, 0)),
        out_shape=jax.ShapeDtypeStruct((m, IN_WIDTH), BF16),
        compiler_params=pltpu.CompilerParams(
            dimension_semantics=("arbitrary",),
            vmem_limit_bytes=VMEM_LIMIT),
        name="inproj",
    )(x2, norm_w, w_in_p, rowtab, basetab)


def _attn_kernel(q_ref, kvp_ref, kvc_ref, kvn_ref, bias_ref, sink_ref, o_ref):
    i = pl.program_id(1)
    nt = pl.num_programs(1)
    n_blk = q_ref.shape[0] // BLOCK
    hd = ATTN_HEAD_DIM
    thr_in = jnp.float32(0.5 * NEG)
    thr_first = jnp.where(i > 0, 0.5 * NEG, -NEG).astype(F32)
    thr_last = jnp.where(i < nt - 1, 0.5 * NEG, -NEG).astype(F32)
    kv = jnp.concatenate([kvp_ref[...], kvc_ref[...], kvn_ref[...]], axis=0)
    lane = lax.broadcasted_iota(jnp.int32, (kv.shape[0], LANES), 1)
    low = lane < hd

    k_side, v_side = {}, {}
    for j in range(ATTN_KV_COLS // LANES):
        kj = kv[:, LANES * j:LANES * (j + 1)]
        vj = kv[:, ATTN_KV_COLS + LANES * j:ATTN_KV_COLS + LANES * (j + 1)]
        for hg in range(2):
            mine = low if hg == 0 else jnp.logical_not(low)
            k_here = jnp.where(mine, kj, jnp.zeros_like(kj))
            v_here = jnp.where(mine, vj, jnp.ones_like(vj))
            k_side[(2 * j + hg, hg)] = k_here
            v_side[(2 * j + hg, hg)] = v_here
            k_side[(2 * j + hg, 1 - hg)] = _roll_half(k_here)
            v_side[(2 * j + hg, 1 - hg)] = _roll_half(v_here)

    def scores(item):
        blk, g = item
        win = slice(BLOCK * blk, BLOCK * blk + KW)
        qs = jnp.concatenate(
            [q_ref[BLOCK * blk:BLOCK * (blk + 1), LANES * c:LANES * (c + 1)]
             for c in (2 * g, 2 * g + 1)], axis=0)
        return [lax.dot_general(k_side[(g, half)][win], qs, (((1,), (1,)), ((), ())),
                                preferred_element_type=F32) for half in range(2)]

    def finish(item, st_pair):
        blk, g = item
        win = slice(BLOCK * blk, BLOCK * blk + KW)
        thr_l = thr_first if blk == 0 else thr_in
        thr_r = thr_last if blk == n_blk - 1 else thr_in
        st = jnp.concatenate(st_pair, axis=1)
        b = bias_ref[g]
        t = jnp.concatenate([
            jnp.where(b[:BLOCK] > thr_l, st[:BLOCK] + b[:BLOCK], NEG),
            st[BLOCK:2 * BLOCK] + b[BLOCK:2 * BLOCK],
            jnp.where(b[2 * BLOCK:] > thr_r, st[2 * BLOCK:] + b[2 * BLOCK:], NEG),
        ], axis=0)
        sk = sink_ref[g]
        m = jnp.maximum(jnp.max(t, axis=0, keepdims=True), sk)
        p = jnp.exp2(t - m).astype(BF16)
        e_sink = jnp.exp2(sk - m)
        outs = []
        for half in range(2):
            cols = slice(2 * BLOCK * half, 2 * BLOCK * (half + 1))
            ot = lax.dot_general(v_side[(g, half)][win], p[:, cols], (((0,), (0,)), ((), ())),
                                 preferred_element_type=F32)
            if half == 0:
                num, den = ot[:hd], ot[hd:hd + 1]
            else:
                num, den = ot[hd:], ot[0:1]
            outs.append(num / (den + e_sink[:, cols]))
        out_t = jnp.concatenate(outs, axis=0)
        for r in range(2):
            c = 2 * g + r
            o_ref[BLOCK * blk:BLOCK * (blk + 1), LANES * c:LANES * (c + 1)] = (
                out_t[:, BLOCK * r:BLOCK * (r + 1)].T.astype(o_ref.dtype))

    items = [(blk, g) for blk in range(n_blk) for g in range(ATTN_KV_HEADS)]
    st_next = scores(items[0])
    for idx, item in enumerate(items):
        st_cur = st_next
        if idx + 1 < len(items):
            st_next = scores(items[idx + 1])
        finish(item, st_cur)


def _attention(proj, bias_t, sink_t, batch, seq, tq=512):
    nt = seq // tq
    bpt = tq // BLOCK
    nb = seq // BLOCK
    kvblk = OFF_KA // CB
    n_kv = ATTN_KV_HEADS
    prev_blk = lambda b, i: (b * nb + jnp.maximum(bpt * i - 1, 0), kvblk)
    next_blk = lambda b, i: (b * nb + jnp.minimum(bpt * (i + 1), nb - 1), kvblk)
    return pl.pallas_call(
        _attn_kernel,
        grid=(batch, nt),
        in_specs=[
            pl.BlockSpec((tq, ATTN_WIDTH), lambda b, i: (b * nt + i, OFF_QA // ATTN_WIDTH)),
            pl.BlockSpec((BLOCK, CB), prev_blk),
            pl.BlockSpec((tq, CB), lambda b, i: (b * nt + i, kvblk)),
            pl.BlockSpec((BLOCK, CB), next_blk),
            pl.BlockSpec((n_kv, KW, ATTN_GROUP * BLOCK), lambda b, i: (0, 0, 0)),
            pl.BlockSpec((n_kv, 1, ATTN_GROUP * BLOCK), lambda b, i: (0, 0, 0)),
        ],
        out_specs=pl.BlockSpec((tq, ATTN_WIDTH), lambda b, i: (b * nt + i, 0)),
        out_shape=jax.ShapeDtypeStruct((batch * seq, ATTN_WIDTH), BF16),
        compiler_params=pltpu.CompilerParams(
            dimension_semantics=("arbitrary", "arbitrary"),
            vmem_limit_bytes=VMEM_LIMIT),
        name="attn",
    )(proj, proj, proj, proj, bias_t, sink_t)


PB_ROWS = RET_KEY_DIM


def _first_step():
    return jnp.logical_and(pl.program_id(0) == 0, pl.program_id(1) == 0)


def _lane_decay(lg_ref, c, expo):
    lane = lax.broadcasted_iota(jnp.int32, (CHUNK, LANES), 1)
    lg = jnp.where(lane < RET_KEY_DIM, lg_ref[2 * c], lg_ref[2 * c + 1])
    return jnp.exp(lg * expo)


def _state_mask():
    r = lax.broadcasted_iota(jnp.int32, (LANES, 2 * RET_VAL_DIM), 0)
    m = lax.broadcasted_iota(jnp.int32, (LANES, 2 * RET_VAL_DIM), 1)
    return (r // RET_KEY_DIM) == (m // RET_VAL_DIM)


def _chunk_decay(lg_ref, c):
    r = lax.broadcasted_iota(jnp.int32, (LANES, 2 * RET_VAL_DIM), 0)
    lg = jnp.where(r < RET_KEY_DIM, lg_ref[2 * c], lg_ref[2 * c + 1])
    return jnp.where(_state_mask(), jnp.exp(lg * float(CHUNK)), 0.0)


def _row_index():
    return lax.broadcasted_iota(jnp.int32, (CHUNK, LANES), 0).astype(F32)


def _state_update(st_ref, c, k_rot, kdec, gdec, vpair):
    kd = (k_rot * kdec).astype(BF16)
    upd = lax.dot_general(kd, vpair, (((0,), (0,)), ((), ())),
                          preferred_element_type=F32)
    st_ref[c] = st_ref[c] * gdec + jnp.where(_state_mask(), upd, 0.0)


def _pair_cols(refs, rows, c, width):
    per_ref = CB // width
    return refs[c // per_ref][rows, width * (c % per_ref):width * (c % per_ref + 1)]


def _retstate_kernel(lgb_ref, k_ref, v0_ref, v1_ref, pb_ref, st_ref, kdb_ref, gb_ref):
    @pl.when(_first_step())
    def _():
        idx = _row_index()
        for c in range(N_PAIRS):
            kdb_ref[c] = _lane_decay(lgb_ref, c, idx)
            gb_ref[c] = _chunk_decay(lgb_ref, c)

    @pl.when(pl.program_id(1) == 0)
    def _():
        st_ref[...] = jnp.zeros_like(st_ref)

    n_ch = k_ref.shape[0] // CHUNK
    for ch in reversed(range(n_ch)):
        rows = slice(CHUNK * ch, CHUNK * (ch + 1))
        for c in range(N_PAIRS):
            st = st_ref[c]
            pb_ref[ch, c] = (st[:PB_ROWS] + st[PB_ROWS:]).astype(BF16)
            k = k_ref[rows, LANES * c:LANES * (c + 1)].astype(F32)
            vpair = _pair_cols((v0_ref, v1_ref), rows, c, 2 * RET_VAL_DIM)
            _state_update(st_ref, c, k, kdb_ref[c], gb_ref[c], vpair)


def _retention_states(proj, lg_b, batch, seq, tt=512):
    nt = seq // tt
    cpt = tt // CHUNK
    rrow = lambda b, i: b * nt + (nt - 1 - i)
    col = lambda off: (lambda b, i: (rrow(b, i), off // CB))
    return pl.pallas_call(
        _retstate_kernel,
        grid=(batch, nt),
        in_specs=[
            pl.BlockSpec(memory_space=pltpu.SMEM),
            pl.BlockSpec((tt, CB), col(OFF_KR)),
            pl.BlockSpec((tt, CB), col(OFF_VR)),
            pl.BlockSpec((tt, CB), col(OFF_VR + CB)),
        ],
        out_specs=pl.BlockSpec((cpt, N_PAIRS, PB_ROWS, 2 * RET_VAL_DIM),
                               lambda b, i: (rrow(b, i), 0, 0, 0)),
        out_shape=jax.ShapeDtypeStruct((batch * nt * cpt, N_PAIRS, PB_ROWS, 2 * RET_VAL_DIM), BF16),
        scratch_shapes=[pltpu.VMEM((N_PAIRS, LANES, 2 * RET_VAL_DIM), F32),
                        pltpu.VMEM((N_PAIRS, CHUNK, LANES), F32),
                        pltpu.VMEM((N_PAIRS, LANES, 2 * RET_VAL_DIM), F32)],
        compiler_params=pltpu.CompilerParams(
            dimension_semantics=("arbitrary", "arbitrary"),
            vmem_limit_bytes=VMEM_LIMIT),
        name="ret_states",
    )(lg_b, proj, proj, proj)


def _ret_kernel(lgf_ref, lgb_ref, q_ref, k_ref, v0_ref, v1_ref, g0_ref, g1_ref, pb_ref, gnw_ref,
                o_ref, st_ref, d_ref, qdf_ref, qdb_ref, kdf_ref, gf_ref):
    @pl.when(_first_step())
    def _():
        idx = _row_index()
        col = lax.broadcasted_iota(jnp.int32, (CHUNK, CHUNK), 1).astype(F32)
        diff = idx - col
        for c in range(N_PAIRS):
            qdf_ref[c] = _lane_decay(lgf_ref, c, idx + 1.0)
            qdb_ref[c] = _lane_decay(lgb_ref, c, float(CHUNK) - idx)
            kdf_ref[c] = _lane_decay(lgf_ref, c, float(CHUNK - 1) - idx)
            gf_ref[c] = _chunk_decay(lgf_ref, c)
        for hh in range(RET_HEADS):
            d_ref[hh] = jnp.exp(jnp.where(diff >= 0.0, lgf_ref[hh] * diff, lgb_ref[hh] * (-diff)))

    @pl.when(pl.program_id(1) == 0)
    def _():
        st_ref[...] = jnp.zeros_like(st_ref)

    lane = lax.broadcasted_iota(jnp.int32, (CHUNK, LANES), 1)
    low_half = lane < RET_KEY_DIM
    first_v = lax.broadcasted_iota(jnp.int32, (CHUNK, 2 * RET_VAL_DIM), 1) < RET_VAL_DIM
    first_s = lax.broadcasted_iota(jnp.int32, (PB_ROWS, 2 * RET_VAL_DIM), 1) < RET_VAL_DIM

    def scores(item):
        ch, c = item
        rows = slice(CHUNK * ch, CHUNK * (ch + 1))
        kb = k_ref[rows, LANES * c:LANES * (c + 1)]
        q = q_ref[rows, LANES * c:LANES * (c + 1)]
        out = []
        for half in range(2):
            sel = low_half if half == 0 else jnp.logical_not(low_half)
            qh = jnp.where(sel, q, jnp.zeros_like(q))
            out.append(lax.dot_general(qh, kb, (((1,), (1,)), ((), ())),
                                       preferred_element_type=F32))
        return out

    def finish(item, s_pair):
        ch, c = item
        rows = slice(CHUNK * ch, CHUNK * (ch + 1))
        q = q_ref[rows, LANES * c:LANES * (c + 1)].astype(F32)
        k = k_ref[rows, LANES * c:LANES * (c + 1)].astype(F32)
        vpair = _pair_cols((v0_ref, v1_ref), rows, c, 2 * RET_VAL_DIM)
        parts = [(s_pair[half] * d_ref[2 * c + half]).astype(BF16) for half in range(2)]
        parts.append((q * qdf_ref[c]).astype(BF16))
        parts.append((q * qdb_ref[c]).astype(BF16))
        lhs = jnp.concatenate(parts, axis=1)
        zero_v = jnp.zeros_like(vpair)
        pb = pb_ref[ch, c]
        zero_s = jnp.zeros_like(pb)
        rhs = jnp.concatenate([
            jnp.where(first_v, vpair, zero_v),
            jnp.where(first_v, zero_v, vpair),
            st_ref[c].astype(BF16),
            jnp.where(first_s, pb, zero_s),
            jnp.where(first_s, zero_s, pb),
        ], axis=0)
        o = jnp.dot(lhs, rhs, preferred_element_type=F32)
        _state_update(st_ref, c, k, kdf_ref[c], gf_ref[c], vpair)
        for half in range(2):
            head = 2 * c + half
            lo_c = RET_VAL_DIM * head
            oh = o[:, RET_VAL_DIM * half:RET_VAL_DIM * (half + 1)]
            mu = jnp.mean(oh, axis=-1, keepdims=True)
            dev = oh - mu
            var = jnp.mean(dev * dev, axis=-1, keepdims=True)
            y = dev * lax.rsqrt(var + EPS) * gnw_ref[:, lo_c:lo_c + RET_VAL_DIM]
            g = _pair_cols((g0_ref, g1_ref), rows, head, RET_VAL_DIM).astype(F32)
            o_ref[rows, lo_c:lo_c + RET_VAL_DIM] = (y * g).astype(o_ref.dtype)

    items = [(ch, c) for ch in range(q_ref.shape[0] // CHUNK) for c in range(N_PAIRS)]
    s_next = scores(items[0])
    for idx, item in enumerate(items):
        s_cur = s_next
        if idx + 1 < len(items):
            s_next = scores(items[idx + 1])
        finish(item, s_cur)


def _retention(proj, pb, lg_f, lg_b, gn_w, batch, seq, tt=512):
    nt = seq // tt
    cpt = tt // CHUNK
    row = lambda b, i: b * nt + i
    col = lambda off: (lambda b, i: (row(b, i), off // CB))
    return pl.pallas_call(
        _ret_kernel,
        grid=(batch, nt),
        in_specs=[
            pl.BlockSpec(memory_space=pltpu.SMEM),
            pl.BlockSpec(memory_space=pltpu.SMEM),
            pl.BlockSpec((tt, CB), col(OFF_QR)),
            pl.BlockSpec((tt, CB), col(OFF_KR)),
            pl.BlockSpec((tt, CB), col(OFF_VR)),
            pl.BlockSpec((tt, CB), col(OFF_VR + CB)),
            pl.BlockSpec((tt, CB), col(OFF_GR)),
            pl.BlockSpec((tt, CB), col(OFF_GR + CB)),
            pl.BlockSpec((cpt, N_PAIRS, PB_ROWS, 2 * RET_VAL_DIM), lambda b, i: (row(b, i), 0, 0, 0)),
            pl.BlockSpec((1, RET_WIDTH), lambda b, i: (0, 0)),
        ],
        out_specs=pl.BlockSpec((tt, RET_WIDTH), lambda b, i: (row(b, i), 0)),
        out_shape=jax.ShapeDtypeStruct((batch * seq, RET_WIDTH), BF16),
        scratch_shapes=[pltpu.VMEM((N_PAIRS, LANES, 2 * RET_VAL_DIM), F32),
                        pltpu.VMEM((RET_HEADS, CHUNK, CHUNK), F32),
                        pltpu.VMEM((N_PAIRS, CHUNK, LANES), F32),
                        pltpu.VMEM((N_PAIRS, CHUNK, LANES), F32),
                        pltpu.VMEM((N_PAIRS, CHUNK, LANES), F32),
                        pltpu.VMEM((N_PAIRS, LANES, 2 * RET_VAL_DIM), F32)],
        compiler_params=pltpu.CompilerParams(
            dimension_semantics=("arbitrary", "arbitrary"),
            vmem_limit_bytes=VMEM_LIMIT),
        name="retention",
    )(lg_f, lg_b, proj, proj, proj, proj, proj, proj, pb, gn_w)


def _outproj_kernel(a_ref, g0_ref, g1_ref, r_ref, x_ref, anw_ref, w_ref, fw_ref, o_ref,
                    *, row_splits):
    start = 0
    for rc in row_splits:
        rows = slice(start, start + rc)
        start += rc
        a = a_ref[rows, :].astype(F32)
        ms = jnp.mean(a * a, axis=-1, keepdims=True)
        a = a * lax.rsqrt(ms + EPS) * anw_ref[...]
        g = jnp.concatenate([g0_ref[rows, :], g1_ref[rows, :]], axis=1).astype(F32)
        mixed = jnp.concatenate([r_ref[rows, :], (a * g).astype(BF16)], axis=1)
        y = x_ref[rows, :] + jnp.dot(mixed, w_ref[...], preferred_element_type=F32)
        ms = jnp.mean(y * y, axis=-1, keepdims=True)
        o_ref[rows, :] = y * lax.rsqrt(ms + EPS) * fw_ref[...]


def _outproj(attn_o, proj, ret_o, x2, attn_nw, w_o_p, final_w, tm=512,
             row_splits=(256, 256)):
    m = x2.shape[0]
    assert sum(row_splits) == tm
    return pl.pallas_call(
        functools.partial(_outproj_kernel, row_splits=row_splits),
        grid=(m // tm,),
        in_specs=[
            pl.BlockSpec((tm, ATTN_WIDTH), lambda i: (i, 0)),
            pl.BlockSpec((tm, CB), lambda i: (i, OFF_GA // CB)),
            pl.BlockSpec((tm, CB), lambda i: (i, OFF_GA // CB + 1)),
            pl.BlockSpec((tm, RET_WIDTH), lambda i: (i, 0)),
            pl.BlockSpec((tm, D_MODEL), lambda i: (i, 0)),
            pl.BlockSpec((1, ATTN_WIDTH), lambda i: (0, 0)),
            pl.BlockSpec((ATTN_WIDTH + RET_WIDTH, D_MODEL), lambda i: (0, 0)),
            pl.BlockSpec((1, D_MODEL), lambda i: (0, 0)),
        ],
        out_specs=pl.BlockSpec((tm, D_MODEL), lambda i: (i, 0)),
        out_shape=jax.ShapeDtypeStruct((m, D_MODEL), F32),
        compiler_params=pltpu.CompilerParams(
            dimension_semantics=("arbitrary",),
            vmem_limit_bytes=VMEM_LIMIT),
        name="outproj",
    )(attn_o, proj, proj, ret_o, x2, attn_nw, w_o_p, final_w)


def _t5_bucket(rel):
    nb = N_BUCKETS // 2
    max_exact = nb // 2
    ret = jnp.where(rel > 0, nb, 0)
    n = jnp.abs(rel)
    nf = jnp.maximum(n, 1).astype(F32)
    large = max_exact + (jnp.log(nf / max_exact) / math.log(MAX_DISTANCE / max_exact)
                         * (nb - max_exact)).astype(jnp.int32)
    large = jnp.minimum(large, nb - 1)
    return ret + jnp.where(n < max_exact, n, large)


def kernel(x, norm_w, w_in, attn_sink, rel_bias, attn_out_norm_w, ret_decay_fwd,
           ret_decay_bwd, ret_gn_w, w_out, final_norm_w):
    batch, seq, _ = x.shape
    assert norm_w.shape[0] == 1 and seq % BLOCK == 0
    x2 = x.reshape(batch * seq, D_MODEL)

    col_scale = np.ones((1, IN_WIDTH), np.float32)
    col_scale[:, OFF_QA:OFF_QA + ATTN_WIDTH] = ATTN_HEAD_DIM ** -0.5 * LOG2E
    col_scale[:, OFF_KR:OFF_KR + RET_QK_COLS] = RET_KEY_DIM ** -0.5
    w_in_p = (w_in[0] * col_scale).astype(BF16)
    w_o_p = jnp.concatenate([w_out[0, ATTN_WIDTH:], w_out[0, :ATTN_WIDTH]],
                            axis=0).astype(BF16)

    qi = jnp.arange(BLOCK, dtype=jnp.int32)[None, None, None, :]
    kt = jnp.arange(KW, dtype=jnp.int32)[:, None, None, None]
    rel = kt - BLOCK - qi
    bucket = _t5_bucket(rel)
    rb = rel_bias.astype(F32).reshape(N_BUCKETS, ATTN_KV_HEADS, 2, 2)
    rb = jnp.swapaxes(rb, 2, 3)
    bias_t = jnp.zeros((ATTN_KV_HEADS, KW, 2, 2, BLOCK), F32)
    for n in range(N_BUCKETS):
        bias_t = jnp.where((bucket == n)[None], rb[n][:, None, :, :, None], bias_t)
    bias_t = jnp.where((jnp.abs(rel) <= WINDOW)[None], bias_t * LOG2E, NEG)
    bias_t = bias_t.reshape(ATTN_KV_HEADS, KW, ATTN_GROUP * BLOCK)
    sink = (attn_sink[0].astype(F32) * LOG2E).reshape(ATTN_KV_HEADS, 1, 2, 2)
    sink = jnp.swapaxes(sink, 2, 3)[..., None]
    sink_t = jnp.broadcast_to(sink, (ATTN_KV_HEADS, 1, 2, 2, BLOCK)).reshape(
        ATTN_KV_HEADS, 1, ATTN_GROUP * BLOCK)

    inv = ROPE_BASE ** (-jnp.arange(0, RET_KEY_DIM, 2, dtype=F32) / RET_KEY_DIM)
    inv_full = jnp.tile(jnp.repeat(inv, 2), LANES // RET_KEY_DIM)[None, :]
    sgn = np.tile(np.array([-1.0, 1.0], np.float32), LANES // 2)[None, :]

    def trig_table(pos):
        ang = pos[:, None] * inv_full
        return jnp.stack([jnp.cos(ang), jnp.sin(ang), jnp.sin(ang) * sgn])

    rowtab = trig_table(jnp.arange(2 * BLOCK, dtype=F32))
    basetab = jnp.swapaxes(trig_table(jnp.arange(0, seq, BLOCK, dtype=F32)), 0, 1)

    lg_f = jax.nn.log_sigmoid(ret_decay_fwd[0].astype(F32))
    lg_b = jax.nn.log_sigmoid(ret_decay_bwd[0].astype(F32))

    proj = _inproj(x2, norm_w[0].reshape(1, D_MODEL), w_in_p, rowtab, basetab, seq)
    attn_o = _attention(proj, bias_t, sink_t, batch, seq)
    pb = _retention_states(proj, lg_b, batch, seq)
    ret_o = _retention(proj, pb, lg_f, lg_b, ret_gn_w[0].reshape(1, RET_WIDTH), batch, seq)
    out = _outproj(attn_o, proj, ret_o, x2, attn_out_norm_w[0].reshape(1, ATTN_WIDTH), w_o_p,
                   final_norm_w.reshape(1, D_MODEL))
    return out.reshape(batch, seq, D_MODEL)
```

```python
import functools
import math

import numpy as np
import jax
import jax.numpy as jnp
from jax import lax
from jax.experimental import pallas as pl
from jax.experimental.pallas import tpu as pltpu

D_MODEL = 2048
ATTN_HEAD_DIM = 64
ATTN_WIDTH = 1024
ATTN_HEADS = 16
ATTN_KV_HEADS = 4
ATTN_GROUP = 4
ATTN_KV_COLS = 256
WINDOW = 128
BLOCK = 128
N_BUCKETS = 32
MAX_DISTANCE = 128
RET_WIDTH = 1024
RET_HEADS = 8
RET_VAL_DIM = 128
RET_KEY_DIM = 64
RET_QK_COLS = 512
CHUNK = 128
ROPE_BASE = 10000.0
EPS = 1e-6
NEG = -1e30
BIG = 3e38
LOG2E = math.log2(math.e)

LANES = 128
N_PAIRS = RET_HEADS // 2
KW = 3 * BLOCK
CB = 512

OFF_QA = 0
OFF_KA = OFF_QA + ATTN_WIDTH
OFF_GA = OFF_KA + 2 * ATTN_KV_COLS
OFF_QR = OFF_GA + ATTN_WIDTH
OFF_KR = OFF_QR + RET_QK_COLS
OFF_VR = OFF_KR + RET_QK_COLS
OFF_GR = OFF_VR + RET_WIDTH
IN_WIDTH = OFF_GR + RET_WIDTH

VMEM_LIMIT = 56 * 1024 * 1024

F32 = jnp.float32
BF16 = jnp.bfloat16


def _roll_half(x):
    u = pltpu.bitcast(x, jnp.uint32)
    return pltpu.bitcast(pltpu.roll(u, LANES // 2, axis=1), x.dtype)


def _silu(g):
    return g / (1.0 + jnp.exp(-g))


def _inproj_kernel(x_ref, nw_ref, w_ref, rowtab_ref, basetab_ref, o_ref, *, row_splits,
                   tiles_per_seq):
    tm = x_ref.shape[0]
    nw = nw_ref[...]
    tile_blk = (pl.program_id(0) % tiles_per_seq) * (tm // BLOCK)
    start = 0
    for rc in row_splits:
        rows = slice(start, start + rc)
        x = x_ref[rows, :]
        ms = jnp.mean(x * x, axis=-1, keepdims=True)
        h = (x * lax.rsqrt(ms + EPS) * nw).astype(BF16)
        ca, sa, sas = rowtab_ref[0, :rc, :], rowtab_ref[1, :rc, :], rowtab_ref[2, :rc, :]
        base = basetab_ref[tile_blk + start // BLOCK]
        cb, sb, sbs = base[0:1, :], base[1:2, :], base[2:3, :]
        cos = ca * cb - sa * sb
        sin_signed = sas * cb + ca * sbs
        lane = lax.broadcasted_iota(jnp.int32, (rc, LANES), 1)
        even = (lane % 2) == 0
        start += rc
        for col in range(0, IN_WIDTH, CB):
            res = jnp.dot(h, w_ref[:, col:col + CB], preferred_element_type=F32)
            if OFF_GA <= col < OFF_QR or col >= OFF_GR:
                res = _silu(res)
            elif OFF_QR <= col < OFF_VR:
                parts = []
                for c in range(CB // LANES):
                    v = res[:, LANES * c:LANES * (c + 1)]
                    partner = jnp.where(even, pltpu.roll(v, LANES - 1, axis=1),
                                        pltpu.roll(v, 1, axis=1))
                    parts.append(v * cos + partner * sin_signed)
                res = jnp.concatenate(parts, axis=1)
            o_ref[rows, col:col + CB] = res.astype(o_ref.dtype)


def _inproj(x2, norm_w, w_in_p, rowtab, basetab, seq, tm=512, row_splits=(256, 256)):
    m = x2.shape[0]
    assert sum(row_splits) == tm and max(row_splits) <= rowtab.shape[1]
    return pl.pallas_call(
        functools.partial(_inproj_kernel, row_splits=row_splits, tiles_per_seq=seq // tm),
        grid=(m // tm,),
        in_specs=[
            pl.BlockSpec((tm, D_MODEL), lambda i: (i, 0)),
            pl.BlockSpec((1, D_MODEL), lambda i: (0, 0)),
            pl.BlockSpec((D_MODEL, IN_WIDTH), lambda i: (0, 0)),
            pl.BlockSpec(rowtab.shape, lambda i: (0, 0, 0)),
            pl.BlockSpec(basetab.shape, lambda i: (0, 0, 0)),
        ],
        out_specs=pl.BlockSpec((tm, IN_WIDTH), lambda i: (i, 0)),
        out_shape=jax.ShapeDtypeStruct((m, IN_WIDTH), BF16),
        compiler_params=pltpu.CompilerParams(
            dimension_semantics=("arbitrary",),
            vmem_limit_bytes=VMEM_LIMIT),
        name="inproj",
    )(x2, norm_w, w_in_p, rowtab, basetab)


def _attn_kernel(q_ref, kvp_ref, kvc_ref, kvn_ref, bias_ref, cap_ref, sink_ref, o_ref):
    i = pl.program_id(1)
    nt = pl.num_programs(1)
    n_blk = q_ref.shape[0] // BLOCK
    hd = ATTN_HEAD_DIM
    kv = jnp.concatenate([kvp_ref[...], kvc_ref[...], kvn_ref[...]], axis=0)
    lane = lax.broadcasted_iota(jnp.int32, (kv.shape[0], LANES), 1)
    low = lane < hd

    tile4 = lambda c: jnp.concatenate([c] * ATTN_GROUP, axis=1)
    cap_l, cap_r = tile4(cap_ref[:BLOCK]), tile4(cap_ref[2 * BLOCK:])
    cap_first = jnp.where(i > 0, cap_l, NEG)
    cap_last = jnp.where(i < nt - 1, cap_r, NEG)

    k_side, v_ones = {}, {}
    for j in range(ATTN_KV_COLS // LANES):
        kj = kv[:, LANES * j:LANES * (j + 1)]
        vj = kv[:, ATTN_KV_COLS + LANES * j:ATTN_KV_COLS + LANES * (j + 1)]
        for hg in range(2):
            mine = low if hg == 0 else jnp.logical_not(low)
            k_here = jnp.where(mine, kj, jnp.zeros_like(kj))
            k_side[(2 * j + hg, hg)] = k_here
            k_side[(2 * j + hg, 1 - hg)] = _roll_half(k_here)
            v_ones[2 * j + hg] = jnp.where(mine, vj, jnp.ones_like(vj))

    def scores(item):
        blk, g = item
        win = slice(BLOCK * blk, BLOCK * blk + KW)
        qs = jnp.concatenate(
            [q_ref[BLOCK * blk:BLOCK * (blk + 1), LANES * c:LANES * (c + 1)]
             for c in (2 * g, 2 * g + 1)], axis=0)
        return [lax.dot_general(k_side[(g, half)][win], qs, (((1,), (1,)), ((), ())),
                                preferred_element_type=F32) for half in range(2)]

    def finish(item, st_pair):
        blk, g = item
        win = slice(BLOCK * blk, BLOCK * blk + KW)
        c_l = cap_first if blk == 0 else cap_l
        c_r = cap_last if blk == n_blk - 1 else cap_r
        st = jnp.concatenate(st_pair, axis=1)
        b = jnp.concatenate([bias_ref[ATTN_GROUP * g + t] for t in range(ATTN_GROUP)], axis=1)
        t = jnp.concatenate([
            jnp.minimum(st[:BLOCK] + b[:BLOCK], c_l),
            st[BLOCK:2 * BLOCK] + b[BLOCK:2 * BLOCK],
            jnp.minimum(st[2 * BLOCK:] + b[2 * BLOCK:], c_r),
        ], axis=0)
        sk = sink_ref[g]
        m = jnp.maximum(jnp.max(t, axis=0, keepdims=True), sk)
        p = jnp.exp2(t - m).astype(BF16)
        ot = lax.dot_general(v_ones[g][win], p, (((0,), (0,)), ((), ())),
                             preferred_element_type=F32)
        hg = g % 2
        num = ot[hd * hg:hd * (hg + 1)]
        den = ot[hd * (1 - hg):hd * (1 - hg) + 1]
        out = num * (1.0 / (den + jnp.exp2(sk - m)))
        out_t = jnp.concatenate([out[:, :2 * BLOCK], out[:, 2 * BLOCK:]], axis=0)
        for r in range(2):
            c = 2 * g + r
            o_ref[BLOCK * blk:BLOCK * (blk + 1), LANES * c:LANES * (c + 1)] = (
                out_t[:, BLOCK * r:BLOCK * (r + 1)].T.astype(o_ref.dtype))

    items = [(blk, g) for blk in range(n_blk) for g in range(ATTN_KV_HEADS)]
    st_next = scores(items[0])
    for idx, item in enumerate(items):
        st_cur = st_next
        if idx + 1 < len(items):
            st_next = scores(items[idx + 1])
        finish(item, st_cur)


def _attention(proj, bias_t, cap_t, sink_t, batch, seq, tq=512):
    nt = seq // tq
    bpt = tq // BLOCK
    nb = seq // BLOCK
    kvblk = OFF_KA // CB
    n_kv = ATTN_KV_HEADS
    prev_blk = lambda b, i: (b * nb + jnp.maximum(bpt * i - 1, 0), kvblk)
    next_blk = lambda b, i: (b * nb + jnp.minimum(bpt * (i + 1), nb - 1), kvblk)
    return pl.pallas_call(
        _attn_kernel,
        grid=(batch, nt),
        in_specs=[
            pl.BlockSpec((tq, ATTN_WIDTH), lambda b, i: (b * nt + i, OFF_QA // ATTN_WIDTH)),
            pl.BlockSpec((BLOCK, CB), prev_blk),
            pl.BlockSpec((tq, CB), lambda b, i: (b * nt + i, kvblk)),
            pl.BlockSpec((BLOCK, CB), next_blk),
            pl.BlockSpec((ATTN_HEADS, KW, BLOCK), lambda b, i: (0, 0, 0)),
            pl.BlockSpec((KW, BLOCK), lambda b, i: (0, 0)),
            pl.BlockSpec((n_kv, 1, ATTN_GROUP * BLOCK), lambda b, i: (0, 0, 0)),
        ],
        out_specs=pl.BlockSpec((tq, ATTN_WIDTH), lambda b, i: (b * nt + i, 0)),
        out_shape=jax.ShapeDtypeStruct((batch * seq, ATTN_WIDTH), BF16),
        compiler_params=pltpu.CompilerParams(
            dimension_semantics=("arbitrary", "arbitrary"),
            vmem_limit_bytes=VMEM_LIMIT),
        name="attn",
    )(proj, proj, proj, proj, bias_t, cap_t, sink_t)


PB_ROWS = RET_KEY_DIM


def _first_step():
    return jnp.logical_and(pl.program_id(0) == 0, pl.program_id(1) == 0)


def _lane_decay(lg_ref, c, expo):
    lane = lax.broadcasted_iota(jnp.int32, (CHUNK, LANES), 1)
    lg = jnp.where(lane < RET_KEY_DIM, lg_ref[2 * c], lg_ref[2 * c + 1])
    return jnp.exp(lg * expo)


def _state_mask():
    r = lax.broadcasted_iota(jnp.int32, (LANES, 2 * RET_VAL_DIM), 0)
    m = lax.broadcasted_iota(jnp.int32, (LANES, 2 * RET_VAL_DIM), 1)
    return (r // RET_KEY_DIM) == (m // RET_VAL_DIM)


def _chunk_decay(lg_ref, c):
    r = lax.broadcasted_iota(jnp.int32, (LANES, 2 * RET_VAL_DIM), 0)
    lg = jnp.where(r < RET_KEY_DIM, lg_ref[2 * c], lg_ref[2 * c + 1])
    return jnp.where(_state_mask(), jnp.exp(lg * float(CHUNK)), 0.0)


def _row_index():
    return lax.broadcasted_iota(jnp.int32, (CHUNK, LANES), 0).astype(F32)


def _state_update(st_ref, c, k_rot, kdec, gdec, vpair):
    kd = (k_rot * kdec).astype(BF16)
    upd = lax.dot_general(kd, vpair, (((0,), (0,)), ((), ())),
                          preferred_element_type=F32)
    st_ref[c] = st_ref[c] * gdec + jnp.where(_state_mask(), upd, 0.0)


def _pair_cols(refs, rows, c, width):
    per_ref = CB // width
    return refs[c // per_ref][rows, width * (c % per_ref):width * (c % per_ref + 1)]


def _retstate_kernel(lgb_ref, k_ref, v0_ref, v1_ref, pb_ref, st_ref, kdb_ref, gb_ref):
    @pl.when(_first_step())
    def _():
        idx = _row_index()
        for c in range(N_PAIRS):
            kdb_ref[c] = _lane_decay(lgb_ref, c, idx)
            gb_ref[c] = _chunk_decay(lgb_ref, c)

    @pl.when(pl.program_id(1) == 0)
    def _():
        st_ref[...] = jnp.zeros_like(st_ref)

    n_ch = k_ref.shape[0] // CHUNK
    for ch in reversed(range(n_ch)):
        rows = slice(CHUNK * ch, CHUNK * (ch + 1))
        for c in range(N_PAIRS):
            st = st_ref[c]
            pb_ref[ch, c] = (st[:PB_ROWS] + st[PB_ROWS:]).astype(BF16)
            k = k_ref[rows, LANES * c:LANES * (c + 1)].astype(F32)
            vpair = _pair_cols((v0_ref, v1_ref), rows, c, 2 * RET_VAL_DIM)
            _state_update(st_ref, c, k, kdb_ref[c], gb_ref[c], vpair)


def _retention_states(proj, lg_b, batch, seq, tt=512):
    nt = seq // tt
    cpt = tt // CHUNK
    rrow = lambda b, i: b * nt + (nt - 1 - i)
    col = lambda off: (lambda b, i: (rrow(b, i), off // CB))
    return pl.pallas_call(
        _retstate_kernel,
        grid=(batch, nt),
        in_specs=[
            pl.BlockSpec(memory_space=pltpu.SMEM),
            pl.BlockSpec((tt, CB), col(OFF_KR)),
            pl.BlockSpec((tt, CB), col(OFF_VR)),
            pl.BlockSpec((tt, CB), col(OFF_VR + CB)),
        ],
        out_specs=pl.BlockSpec((cpt, N_PAIRS, PB_ROWS, 2 * RET_VAL_DIM),
                               lambda b, i: (rrow(b, i), 0, 0, 0)),
        out_shape=jax.ShapeDtypeStruct((batch * nt * cpt, N_PAIRS, PB_ROWS, 2 * RET_VAL_DIM), BF16),
        scratch_shapes=[pltpu.VMEM((N_PAIRS, LANES, 2 * RET_VAL_DIM), F32),
                        pltpu.VMEM((N_PAIRS, CHUNK, LANES), F32),
                        pltpu.VMEM((N_PAIRS, LANES, 2 * RET_VAL_DIM), F32)],
        compiler_params=pltpu.CompilerParams(
            dimension_semantics=("arbitrary", "arbitrary"),
            vmem_limit_bytes=VMEM_LIMIT),
        name="ret_states",
    )(lg_b, proj, proj, proj)


def _ret_kernel(lgf_ref, lgb_ref, q_ref, k_ref, v0_ref, v1_ref, g0_ref, g1_ref, pb_ref, gnw_ref,
                o_ref, st_ref, d_ref, qdf_ref, qdb_ref, kdf_ref, gf_ref):
    @pl.when(_first_step())
    def _():
        idx = _row_index()
        col = lax.broadcasted_iota(jnp.int32, (CHUNK, CHUNK), 1).astype(F32)
        diff = idx - col
        for c in range(N_PAIRS):
            qdf_ref[c] = _lane_decay(lgf_ref, c, idx + 1.0)
            qdb_ref[c] = _lane_decay(lgb_ref, c, float(CHUNK) - idx)
            kdf_ref[c] = _lane_decay(lgf_ref, c, float(CHUNK - 1) - idx)
            gf_ref[c] = _chunk_decay(lgf_ref, c)
        for hh in range(RET_HEADS):
            d_ref[hh] = jnp.exp(jnp.where(diff >= 0.0, lgf_ref[hh] * diff, lgb_ref[hh] * (-diff)))

    @pl.when(pl.program_id(1) == 0)
    def _():
        st_ref[...] = jnp.zeros_like(st_ref)

    lane = lax.broadcasted_iota(jnp.int32, (CHUNK, LANES), 1)
    low_half = lane < RET_KEY_DIM
    first_v = lax.broadcasted_iota(jnp.int32, (CHUNK, 2 * RET_VAL_DIM), 1) < RET_VAL_DIM
    first_s = lax.broadcasted_iota(jnp.int32, (PB_ROWS, 2 * RET_VAL_DIM), 1) < RET_VAL_DIM

    def scores(item):
        ch, c = item
        rows = slice(CHUNK * ch, CHUNK * (ch + 1))
        kb = k_ref[rows, LANES * c:LANES * (c + 1)]
        q = q_ref[rows, LANES * c:LANES * (c + 1)]
        out = []
        for half in range(2):
            sel = low_half if half == 0 else jnp.logical_not(low_half)
            qh = jnp.where(sel, q, jnp.zeros_like(q))
            out.append(lax.dot_general(qh, kb, (((1,), (1,)), ((), ())),
                                       preferred_element_type=F32))
        return out

    def finish(item, s_pair):
        ch, c = item
        rows = slice(CHUNK * ch, CHUNK * (ch + 1))
        q = q_ref[rows, LANES * c:LANES * (c + 1)].astype(F32)
        k = k_ref[rows, LANES * c:LANES * (c + 1)].astype(F32)
        vpair = _pair_cols((v0_ref, v1_ref), rows, c, 2 * RET_VAL_DIM)
        parts = [(s_pair[half] * d_ref[2 * c + half]).astype(BF16) for half in range(2)]
        parts.append((q * qdf_ref[c]).astype(BF16))
        parts.append((q * qdb_ref[c]).astype(BF16))
        lhs = jnp.concatenate(parts, axis=1)
        zero_v = jnp.zeros_like(vpair)
        pb = pb_ref[ch, c]
        zero_s = jnp.zeros_like(pb)
        rhs = jnp.concatenate([
            jnp.where(first_v, vpair, zero_v),
            jnp.where(first_v, zero_v, vpair),
            st_ref[c].astype(BF16),
            jnp.where(first_s, pb, zero_s),
            jnp.where(first_s, zero_s, pb),
        ], axis=0)
        o = jnp.dot(lhs, rhs, preferred_element_type=F32)
        _state_update(st_ref, c, k, kdf_ref[c], gf_ref[c], vpair)
        for half in range(2):
            head = 2 * c + half
            lo_c = RET_VAL_DIM * head
            oh = o[:, RET_VAL_DIM * half:RET_VAL_DIM * (half + 1)]
            mu = jnp.mean(oh, axis=-1, keepdims=True)
            dev = oh - mu
            var = jnp.mean(dev * dev, axis=-1, keepdims=True)
            y = dev * lax.rsqrt(var + EPS) * gnw_ref[:, lo_c:lo_c + RET_VAL_DIM]
            g = _pair_cols((g0_ref, g1_ref), rows, head, RET_VAL_DIM).astype(F32)
            o_ref[rows, lo_c:lo_c + RET_VAL_DIM] = (y * g).astype(o_ref.dtype)

    items = [(ch, c) for ch in range(q_ref.shape[0] // CHUNK) for c in range(N_PAIRS)]
    s_next = scores(items[0])
    for idx, item in enumerate(items):
        s_cur = s_next
        if idx + 1 < len(items):
            s_next = scores(items[idx + 1])
        finish(item, s_cur)


def _retention(proj, pb, lg_f, lg_b, gn_w, batch, seq, tt=512):
    nt = seq // tt
    cpt = tt // CHUNK
    row = lambda b, i: b * nt + i
    col = lambda off: (lambda b, i: (row(b, i), off // CB))
    return pl.pallas_call(
        _ret_kernel,
        grid=(batch, nt),
        in_specs=[
            pl.BlockSpec(memory_space=pltpu.SMEM),
            pl.BlockSpec(memory_space=pltpu.SMEM),
            pl.BlockSpec((tt, CB), col(OFF_QR)),
            pl.BlockSpec((tt, CB), col(OFF_KR)),
            pl.BlockSpec((tt, CB), col(OFF_VR)),
            pl.BlockSpec((tt, CB), col(OFF_VR + CB)),
            pl.BlockSpec((tt, CB), col(OFF_GR)),
            pl.BlockSpec((tt, CB), col(OFF_GR + CB)),
            pl.BlockSpec((cpt, N_PAIRS, PB_ROWS, 2 * RET_VAL_DIM), lambda b, i: (row(b, i), 0, 0, 0)),
            pl.BlockSpec((1, RET_WIDTH), lambda b, i: (0, 0)),
        ],
        out_specs=pl.BlockSpec((tt, RET_WIDTH), lambda b, i: (row(b, i), 0)),
        out_shape=jax.ShapeDtypeStruct((batch * seq, RET_WIDTH), BF16),
        scratch_shapes=[pltpu.VMEM((N_PAIRS, LANES, 2 * RET_VAL_DIM), F32),
                        pltpu.VMEM((RET_HEADS, CHUNK, CHUNK), F32),
                        pltpu.VMEM((N_PAIRS, CHUNK, LANES), F32),
                        pltpu.VMEM((N_PAIRS, CHUNK, LANES), F32),
                        pltpu.VMEM((N_PAIRS, CHUNK, LANES), F32),
                        pltpu.VMEM((N_PAIRS, LANES, 2 * RET_VAL_DIM), F32)],
        compiler_params=pltpu.CompilerParams(
            dimension_semantics=("arbitrary", "arbitrary"),
            vmem_limit_bytes=VMEM_LIMIT),
        name="retention",
    )(lg_f, lg_b, proj, proj, proj, proj, proj, proj, pb, gn_w)


def _outproj_kernel(a_ref, g0_ref, g1_ref, r_ref, x_ref, anw_ref, w_ref, fw_ref, o_ref,
                    *, row_splits):
    start = 0
    for rc in row_splits:
        rows = slice(start, start + rc)
        start += rc
        y = x_ref[rows, :] + jnp.dot(r_ref[rows, :], w_ref[ATTN_WIDTH:, :],
                                     preferred_element_type=F32)
        a = a_ref[rows, :].astype(F32)
        ms = jnp.mean(a * a, axis=-1, keepdims=True)
        a = a * lax.rsqrt(ms + EPS) * anw_ref[...]
        g = jnp.concatenate([g0_ref[rows, :], g1_ref[rows, :]], axis=1).astype(F32)
        y = y + jnp.dot((a * g).astype(BF16), w_ref[:ATTN_WIDTH, :],
                        preferred_element_type=F32)
        ms = jnp.mean(y * y, axis=-1, keepdims=True)
        o_ref[rows, :] = y * lax.rsqrt(ms + EPS) * fw_ref[...]


def _outproj(attn_o, proj, ret_o, x2, attn_nw, w_o_p, final_w, tm=512,
             row_splits=(256, 256)):
    m = x2.shape[0]
    assert sum(row_splits) == tm
    return pl.pallas_call(
        functools.partial(_outproj_kernel, row_splits=row_splits),
        grid=(m // tm,),
        in_specs=[
            pl.BlockSpec((tm, ATTN_WIDTH), lambda i: (i, 0)),
            pl.BlockSpec((tm, CB), lambda i: (i, OFF_GA // CB)),
            pl.BlockSpec((tm, CB), lambda i: (i, OFF_GA // CB + 1)),
            pl.BlockSpec((tm, RET_WIDTH), lambda i: (i, 0)),
            pl.BlockSpec((tm, D_MODEL), lambda i: (i, 0)),
            pl.BlockSpec((1, ATTN_WIDTH), lambda i: (0, 0)),
            pl.BlockSpec((ATTN_WIDTH + RET_WIDTH, D_MODEL), lambda i: (0, 0)),
            pl.BlockSpec((1, D_MODEL), lambda i: (0, 0)),
        ],
        out_specs=pl.BlockSpec((tm, D_MODEL), lambda i: (i, 0)),
        out_shape=jax.ShapeDtypeStruct((m, D_MODEL), F32),
        compiler_params=pltpu.CompilerParams(
            dimension_semantics=("arbitrary",),
            vmem_limit_bytes=VMEM_LIMIT),
        name="outproj",
    )(attn_o, proj, proj, ret_o, x2, attn_nw, w_o_p, final_w)


def _t5_bucket(rel):
    nb = N_BUCKETS // 2
    max_exact = nb // 2
    ret = jnp.where(rel > 0, nb, 0)
    n = jnp.abs(rel)
    nf = jnp.maximum(n, 1).astype(F32)
    large = max_exact + (jnp.log(nf / max_exact) / math.log(MAX_DISTANCE / max_exact)
                         * (nb - max_exact)).astype(jnp.int32)
    large = jnp.minimum(large, nb - 1)
    return ret + jnp.where(n < max_exact, n, large)


def kernel(x, norm_w, w_in, attn_sink, rel_bias, attn_out_norm_w, ret_decay_fwd,
           ret_decay_bwd, ret_gn_w, w_out, final_norm_w):
    batch, seq, _ = x.shape
    assert norm_w.shape[0] == 1 and seq % BLOCK == 0
    x2 = x.reshape(batch * seq, D_MODEL)

    col_scale = np.ones((1, IN_WIDTH), np.float32)
    col_scale[:, OFF_QA:OFF_QA + ATTN_WIDTH] = ATTN_HEAD_DIM ** -0.5 * LOG2E
    col_scale[:, OFF_KR:OFF_KR + RET_QK_COLS] = RET_KEY_DIM ** -0.5
    w_in_p = (w_in[0] * col_scale).astype(BF16)
    w_o_p = w_out[0].astype(BF16)

    qi = jnp.arange(BLOCK, dtype=jnp.int32)[None, :]
    kt = jnp.arange(KW, dtype=jnp.int32)[:, None]
    rel = kt - BLOCK - qi
    onehot = (_t5_bucket(rel)[None] == jnp.arange(N_BUCKETS)[:, None, None]).astype(F32)
    rb = rel_bias.astype(F32).reshape(N_BUCKETS, ATTN_KV_HEADS, 2, 2)
    rb = jnp.swapaxes(rb, 2, 3).reshape(N_BUCKETS, ATTN_HEADS)
    bias_t = jnp.einsum("nh,nkq->hkq", rb, onehot,
                        precision=lax.Precision.HIGHEST)
    bias_t = bias_t * LOG2E
    cap_t = jnp.where(jnp.abs(rel) <= WINDOW, BIG, NEG).astype(F32)
    sink = (attn_sink[0].astype(F32) * LOG2E).reshape(ATTN_KV_HEADS, 1, 2, 2)
    sink = jnp.swapaxes(sink, 2, 3)[..., None]
    sink_t = jnp.broadcast_to(sink, (ATTN_KV_HEADS, 1, 2, 2, BLOCK)).reshape(
        ATTN_KV_HEADS, 1, ATTN_GROUP * BLOCK)

    inv = ROPE_BASE ** (-jnp.arange(0, RET_KEY_DIM, 2, dtype=F32) / RET_KEY_DIM)
    inv_full = jnp.tile(jnp.repeat(inv, 2), LANES // RET_KEY_DIM)[None, :]
    sgn = np.tile(np.array([-1.0, 1.0], np.float32), LANES // 2)[None, :]

    def trig_table(pos):
        ang = pos[:, None] * inv_full
        return jnp.stack([jnp.cos(ang), jnp.sin(ang), jnp.sin(ang) * sgn])

    rowtab = trig_table(jnp.arange(2 * BLOCK, dtype=F32))
    basetab = jnp.swapaxes(trig_table(jnp.arange(0, seq, BLOCK, dtype=F32)), 0, 1)

    lg_f = jax.nn.log_sigmoid(ret_decay_fwd[0].astype(F32))
    lg_b = jax.nn.log_sigmoid(ret_decay_bwd[0].astype(F32))

    proj = _inproj(x2, norm_w[0].reshape(1, D_MODEL), w_in_p, rowtab, basetab, seq)
    attn_o = _attention(proj, bias_t, cap_t, sink_t, batch, seq)
    pb = _retention_states(proj, lg_b, batch, seq)
    ret_o = _retention(proj, pb, lg_f, lg_b, ret_gn_w[0].reshape(1, RET_WIDTH), batch, seq)
    out = _outproj(attn_o, proj, ret_o, x2, attn_out_norm_w[0].reshape(1, ATTN_WIDTH), w_o_p,
                   final_norm_w.reshape(1, D_MODEL))
    return out.reshape(batch, seq, D_MODEL)
```

```python
import functools
import math

import numpy as np
import jax
import jax.numpy as jnp
from jax import lax
from jax.experimental import pallas as pl
from jax.experimental.pallas import tpu as pltpu

D_MODEL = 2048
ATTN_HEAD_DIM = 64
ATTN_WIDTH = 1024
ATTN_HEADS = 16
ATTN_KV_HEADS = 4
ATTN_GROUP = 4
ATTN_KV_COLS = 256
WINDOW = 128
BLOCK = 128
N_BUCKETS = 32
MAX_DISTANCE = 128
RET_WIDTH = 1024
RET_HEADS = 8
RET_VAL_DIM = 128
RET_KEY_DIM = 64
RET_QK_COLS = 512
CHUNK = 128
ROPE_BASE = 10000.0
EPS = 1e-6
NEG = -1e30
BIG = 3e38
LOG2E = math.log2(math.e)

LANES = 128
N_PAIRS = RET_HEADS // 2
KW = 3 * BLOCK
CB = 512

OFF_QA = 0
OFF_KA = OFF_QA + ATTN_WIDTH
OFF_GA = OFF_KA + 2 * ATTN_KV_COLS
OFF_QR = OFF_GA + ATTN_WIDTH
OFF_KR = OFF_QR + RET_QK_COLS
OFF_VR = OFF_KR + RET_QK_COLS
OFF_GR = OFF_VR + RET_WIDTH
IN_WIDTH = OFF_GR + RET_WIDTH

VMEM_LIMIT = 56 * 1024 * 1024

F32 = jnp.float32
BF16 = jnp.bfloat16


def _roll_half(x):
    return pltpu.roll(x.astype(F32), LANES // 2, axis=1).astype(x.dtype)


def _silu(g):
    return g / (1.0 + jnp.exp(-g))


def _inproj_kernel(x_ref, nw_ref, w_ref, rowtab_ref, basetab_ref, o_ref, *, row_splits,
                   tiles_per_seq):
    tm = x_ref.shape[0]
    nw = nw_ref[...]
    tile_blk = (pl.program_id(0) % tiles_per_seq) * (tm // BLOCK)
    start = 0
    for rc in row_splits:
        rows = slice(start, start + rc)
        x = x_ref[rows, :]
        ms = jnp.mean(x * x, axis=-1, keepdims=True)
        h = (x * lax.rsqrt(ms + EPS) * nw).astype(BF16)
        ca, sa, sas = rowtab_ref[0, :rc, :], rowtab_ref[1, :rc, :], rowtab_ref[2, :rc, :]
        base = basetab_ref[tile_blk + start // BLOCK]
        cb, sb, sbs = base[0:1, :], base[1:2, :], base[2:3, :]
        cos = ca * cb - sa * sb
        sin_signed = sas * cb + ca * sbs
        lane = lax.broadcasted_iota(jnp.int32, (rc, LANES), 1)
        even = (lane % 2) == 0
        start += rc
        for col in range(0, IN_WIDTH, CB):
            res = jnp.dot(h, w_ref[:, col:col + CB], preferred_element_type=F32)
            if OFF_GA <= col < OFF_QR or col >= OFF_GR:
                res = _silu(res)
            elif OFF_QR <= col < OFF_VR:
                parts = []
                for c in range(CB // LANES):
                    v = res[:, LANES * c:LANES * (c + 1)]
                    partner = jnp.where(even, pltpu.roll(v, LANES - 1, axis=1),
                                        pltpu.roll(v, 1, axis=1))
                    parts.append(v * cos + partner * sin_signed)
                res = jnp.concatenate(parts, axis=1)
            o_ref[rows, col:col + CB] = res.astype(o_ref.dtype)


def _inproj(x2, norm_w, w_in_p, rowtab, basetab, seq, tm=512, row_splits=(256, 256)):
    m = x2.shape[0]
    assert sum(row_splits) == tm and max(row_splits) <= rowtab.shape[1]
    return pl.pallas_call(
        functools.partial(_inproj_kernel, row_splits=row_splits, tiles_per_seq=seq // tm),
        grid=(m // tm,),
        in_specs=[
            pl.BlockSpec((tm, D_MODEL), lambda i: (i, 0)),
            pl.BlockSpec((1, D_MODEL), lambda i: (0, 0)),
            pl.BlockSpec((D_MODEL, IN_WIDTH), lambda i: (0, 0)),
            pl.BlockSpec(rowtab.shape, lambda i: (0, 0, 0)),
            pl.BlockSpec(basetab.shape, lambda i: (0, 0, 0)),
        ],
        out_specs=pl.BlockSpec((tm, IN_WIDTH), lambda i: (i, 0)),
        out_shape=jax.ShapeDtypeStruct((m, IN_WIDTH), BF16),
        compiler_params=pltpu.CompilerParams(
            dimension_semantics=("arbitrary",),
            vmem_limit_bytes=VMEM_LIMIT),
        name="inproj",
    )(x2, norm_w, w_in_p, rowtab, basetab)


PB_ROWS = RET_KEY_DIM


def _first_step():
    return jnp.logical_and(pl.program_id(0) == 0, pl.program_id(1) == 0)


def _lane_decay(lg_ref, c, expo):
    lane = lax.broadcasted_iota(jnp.int32, (CHUNK, LANES), 1)
    lg = jnp.where(lane < RET_KEY_DIM, lg_ref[2 * c], lg_ref[2 * c + 1])
    return jnp.exp(lg * expo)


def _state_mask():
    r = lax.broadcasted_iota(jnp.int32, (LANES, 2 * RET_VAL_DIM), 0)
    m = lax.broadcasted_iota(jnp.int32, (LANES, 2 * RET_VAL_DIM), 1)
    return (r // RET_KEY_DIM) == (m // RET_VAL_DIM)


def _chunk_decay(lg_ref, c):
    r = lax.broadcasted_iota(jnp.int32, (LANES, 2 * RET_VAL_DIM), 0)
    lg = jnp.where(r < RET_KEY_DIM, lg_ref[2 * c], lg_ref[2 * c + 1])
    return jnp.where(_state_mask(), jnp.exp(lg * float(CHUNK)), 0.0)


def _row_index():
    return lax.broadcasted_iota(jnp.int32, (CHUNK, LANES), 0).astype(F32)


def _state_update(st_ref, c, k_rot, kdec, gdec, vpair):
    kd = (k_rot * kdec).astype(BF16)
    upd = lax.dot_general(kd, vpair, (((0,), (0,)), ((), ())),
                          preferred_element_type=F32)
    st_ref[c] = st_ref[c] * gdec + jnp.where(_state_mask(), upd, 0.0)


def _pair_cols(refs, rows, c, width):
    per_ref = CB // width
    return refs[c // per_ref][rows, width * (c % per_ref):width * (c % per_ref + 1)]


def _retstate_kernel(lgb_ref, k_ref, v0_ref, v1_ref, pb_ref, st_ref, kdb_ref, gb_ref):
    @pl.when(_first_step())
    def _():
        idx = _row_index()
        for c in range(N_PAIRS):
            kdb_ref[c] = _lane_decay(lgb_ref, c, idx)
            gb_ref[c] = _chunk_decay(lgb_ref, c)

    @pl.when(pl.program_id(1) == 0)
    def _():
        st_ref[...] = jnp.zeros_like(st_ref)

    n_ch = k_ref.shape[0] // CHUNK
    for ch in reversed(range(n_ch)):
        rows = slice(CHUNK * ch, CHUNK * (ch + 1))
        for c in range(N_PAIRS):
            st = st_ref[c]
            pb_ref[ch, c] = (st[:PB_ROWS] + st[PB_ROWS:]).astype(BF16)
            k = k_ref[rows, LANES * c:LANES * (c + 1)].astype(F32)
            vpair = _pair_cols((v0_ref, v1_ref), rows, c, 2 * RET_VAL_DIM)
            _state_update(st_ref, c, k, kdb_ref[c], gb_ref[c], vpair)


def _retention_states(proj, lg_b, batch, seq, tt=1024):
    nt = seq // tt
    cpt = tt // CHUNK
    rrow = lambda b, i: b * nt + (nt - 1 - i)
    col = lambda off: (lambda b, i: (rrow(b, i), off // CB))
    return pl.pallas_call(
        _retstate_kernel,
        grid=(batch, nt),
        in_specs=[
            pl.BlockSpec(memory_space=pltpu.SMEM),
            pl.BlockSpec((tt, CB), col(OFF_KR)),
            pl.BlockSpec((tt, CB), col(OFF_VR)),
            pl.BlockSpec((tt, CB), col(OFF_VR + CB)),
        ],
        out_specs=pl.BlockSpec((cpt, N_PAIRS, PB_ROWS, 2 * RET_VAL_DIM),
                               lambda b, i: (rrow(b, i), 0, 0, 0)),
        out_shape=jax.ShapeDtypeStruct((batch * nt * cpt, N_PAIRS, PB_ROWS, 2 * RET_VAL_DIM), BF16),
        scratch_shapes=[pltpu.VMEM((N_PAIRS, LANES, 2 * RET_VAL_DIM), F32),
                        pltpu.VMEM((N_PAIRS, CHUNK, LANES), F32),
                        pltpu.VMEM((N_PAIRS, LANES, 2 * RET_VAL_DIM), F32)],
        compiler_params=pltpu.CompilerParams(
            dimension_semantics=("arbitrary", "arbitrary"),
            vmem_limit_bytes=VMEM_LIMIT),
        name="ret_states",
    )(lg_b, proj, proj, proj)


GROUP_ROWS = 2 * BLOCK
OUT_COLS = 512


def _attention_stage(i, nt, q_ref, kvp_ref, kvc_ref, kvn_ref, bias_ref, cap_ref, sink_ref, a_scr):
    n_blk = q_ref.shape[0] // BLOCK
    hd = ATTN_HEAD_DIM
    kv = jnp.concatenate([kvp_ref[...], kvc_ref[...], kvn_ref[...]], axis=0)
    lane = lax.broadcasted_iota(jnp.int32, (kv.shape[0], LANES), 1)
    low = lane < hd

    tile4 = lambda c: jnp.concatenate([c] * ATTN_GROUP, axis=1)
    cap_l, cap_r = tile4(cap_ref[:BLOCK]), tile4(cap_ref[2 * BLOCK:])
    cap_first = jnp.where(i > 0, cap_l, NEG)
    cap_last = jnp.where(i < nt - 1, cap_r, NEG)

    k_side, v_ones = {}, {}
    for j in range(ATTN_KV_COLS // LANES):
        kj = kv[:, LANES * j:LANES * (j + 1)]
        vj = kv[:, ATTN_KV_COLS + LANES * j:ATTN_KV_COLS + LANES * (j + 1)]
        for hg in range(2):
            mine = low if hg == 0 else jnp.logical_not(low)
            k_here = jnp.where(mine, kj, jnp.zeros_like(kj))
            k_side[(2 * j + hg, hg)] = k_here
            k_side[(2 * j + hg, 1 - hg)] = _roll_half(k_here)
            v_ones[2 * j + hg] = jnp.where(mine, vj, jnp.ones_like(vj))

    def scores(item):
        blk, g = item
        win = slice(BLOCK * blk, BLOCK * blk + KW)
        qs = jnp.concatenate(
            [q_ref[BLOCK * blk:BLOCK * (blk + 1), LANES * c:LANES * (c + 1)]
             for c in (2 * g, 2 * g + 1)], axis=0)
        return [lax.dot_general(k_side[(g, half)][win], qs, (((1,), (1,)), ((), ())),
                                preferred_element_type=F32) for half in range(2)]

    def finish(item, st_pair):
        blk, g = item
        win = slice(BLOCK * blk, BLOCK * blk + KW)
        c_l = cap_first if blk == 0 else cap_l
        c_r = cap_last if blk == n_blk - 1 else cap_r
        st = jnp.concatenate(st_pair, axis=1)
        b = jnp.concatenate([bias_ref[ATTN_GROUP * g + t] for t in range(ATTN_GROUP)], axis=1)
        t = jnp.concatenate([
            jnp.minimum(st[:BLOCK] + b[:BLOCK], c_l),
            st[BLOCK:2 * BLOCK] + b[BLOCK:2 * BLOCK],
            jnp.minimum(st[2 * BLOCK:] + b[2 * BLOCK:], c_r),
        ], axis=0)
        sk = sink_ref[g]
        m = jnp.maximum(jnp.max(t, axis=0, keepdims=True), sk)
        p = jnp.exp2(t - m).astype(BF16)
        ot = lax.dot_general(v_ones[g][win], p, (((0,), (0,)), ((), ())),
                             preferred_element_type=F32)
        hg = g % 2
        num = ot[hd * hg:hd * (hg + 1)]
        den = ot[hd * (1 - hg):hd * (1 - hg) + 1]
        out = num * (1.0 / (den + jnp.exp2(sk - m)))
        out_t = jnp.concatenate([out[:, :2 * BLOCK], out[:, 2 * BLOCK:]], axis=0)
        for r in range(2):
            c = 2 * g + r
            a_scr[BLOCK * blk:BLOCK * (blk + 1), LANES * c:LANES * (c + 1)] = (
                out_t[:, BLOCK * r:BLOCK * (r + 1)].T.astype(a_scr.dtype))

    return scores, finish


def _retention_stage(q_ref, k_ref, v_refs, g_refs, pb_ref, gnw_ref, st_ref, d_ref, qdf_ref,
                     qdb_ref, kdf_ref, gf_ref, r_scr):
    lane = lax.broadcasted_iota(jnp.int32, (CHUNK, LANES), 1)
    low_half = lane < RET_KEY_DIM
    first_v = lax.broadcasted_iota(jnp.int32, (CHUNK, 2 * RET_VAL_DIM), 1) < RET_VAL_DIM
    first_s = lax.broadcasted_iota(jnp.int32, (PB_ROWS, 2 * RET_VAL_DIM), 1) < RET_VAL_DIM

    def scores(item):
        ch, c = item
        rows = slice(CHUNK * ch, CHUNK * (ch + 1))
        kb = k_ref[rows, LANES * c:LANES * (c + 1)]
        q = q_ref[rows, LANES * c:LANES * (c + 1)]
        out = []
        for half in range(2):
            sel = low_half if half == 0 else jnp.logical_not(low_half)
            qh = jnp.where(sel, q, jnp.zeros_like(q))
            out.append(lax.dot_general(qh, kb, (((1,), (1,)), ((), ())),
                                       preferred_element_type=F32))
        return out

    def finish(item, s_pair):
        ch, c = item
        rows = slice(CHUNK * ch, CHUNK * (ch + 1))
        q = q_ref[rows, LANES * c:LANES * (c + 1)].astype(F32)
        k = k_ref[rows, LANES * c:LANES * (c + 1)].astype(F32)
        vpair = _pair_cols(v_refs, rows, c, 2 * RET_VAL_DIM)
        parts = [(s_pair[half] * d_ref[2 * c + half]).astype(BF16) for half in range(2)]
        parts.append((q * qdf_ref[c]).astype(BF16))
        parts.append((q * qdb_ref[c]).astype(BF16))
        lhs = jnp.concatenate(parts, axis=1)
        zero_v = jnp.zeros_like(vpair)
        pb = pb_ref[ch, c]
        zero_s = jnp.zeros_like(pb)
        rhs = jnp.concatenate([
            jnp.where(first_v, vpair, zero_v),
            jnp.where(first_v, zero_v, vpair),
            st_ref[c].astype(BF16),
            jnp.where(first_s, pb, zero_s),
            jnp.where(first_s, zero_s, pb),
        ], axis=0)
        o = jnp.dot(lhs, rhs, preferred_element_type=F32)
        _state_update(st_ref, c, k, kdf_ref[c], gf_ref[c], vpair)
        for half in range(2):
            head = 2 * c + half
            lo_c = RET_VAL_DIM * head
            oh = o[:, RET_VAL_DIM * half:RET_VAL_DIM * (half + 1)]
            mu = jnp.mean(oh, axis=-1, keepdims=True)
            dev = oh - mu
            var = jnp.mean(dev * dev, axis=-1, keepdims=True)
            y = dev * lax.rsqrt(var + EPS) * gnw_ref[:, lo_c:lo_c + RET_VAL_DIM]
            g = _pair_cols(g_refs, rows, head, RET_VAL_DIM).astype(F32)
            r_scr[rows, lo_c:lo_c + RET_VAL_DIM] = (y * g).astype(r_scr.dtype)

    return scores, finish


def _mixer_kernel(lgf_ref, lgb_ref, qa_ref, kvp_ref, kvc_ref, kvn_ref, bias_ref, cap_ref, sink_ref,
                  ga0_ref, ga1_ref, qr_ref, kr_ref, v0_ref, v1_ref, gr0_ref, gr1_ref, pb_ref,
                  gnw_ref, x_ref, anw_ref, w_ref, fw_ref, o_ref,
                  a_scr, r_scr, y_scr, st_ref, d_ref, qdf_ref, qdb_ref, kdf_ref, gf_ref):
    @pl.when(_first_step())
    def _():
        idx = _row_index()
        col = lax.broadcasted_iota(jnp.int32, (CHUNK, CHUNK), 1).astype(F32)
        diff = idx - col
        for c in range(N_PAIRS):
            qdf_ref[c] = _lane_decay(lgf_ref, c, idx + 1.0)
            qdb_ref[c] = _lane_decay(lgb_ref, c, float(CHUNK) - idx)
            kdf_ref[c] = _lane_decay(lgf_ref, c, float(CHUNK - 1) - idx)
            gf_ref[c] = _chunk_decay(lgf_ref, c)
        for hh in range(RET_HEADS):
            d_ref[hh] = jnp.exp(jnp.where(diff >= 0.0, lgf_ref[hh] * diff, lgb_ref[hh] * (-diff)))

    @pl.when(pl.program_id(1) == 0)
    def _():
        st_ref[...] = jnp.zeros_like(st_ref)

    tq = qa_ref.shape[0]
    a_scores, a_finish = _attention_stage(pl.program_id(1), pl.num_programs(1), qa_ref, kvp_ref,
                                          kvc_ref, kvn_ref, bias_ref, cap_ref, sink_ref, a_scr)
    r_scores, r_finish = _retention_stage(qr_ref, kr_ref, (v0_ref, v1_ref), (gr0_ref, gr1_ref),
                                          pb_ref, gnw_ref, st_ref, d_ref, qdf_ref, qdb_ref,
                                          kdf_ref, gf_ref, r_scr)
    stage = {"a": (a_scores, a_finish), "r": (r_scores, r_finish)}

    n_grp = tq // GROUP_ROWS
    n_out = D_MODEL // OUT_COLS
    gated = {}

    def out_chunk(grp, n):
        rows = slice(GROUP_ROWS * grp, GROUP_ROWS * (grp + 1))
        cols = slice(OUT_COLS * n, OUT_COLS * (n + 1))
        if grp not in gated:
            a = a_scr[rows, :].astype(F32)
            ms = jnp.mean(a * a, axis=-1, keepdims=True)
            g = jnp.concatenate([ga0_ref[rows, :], ga1_ref[rows, :]], axis=1).astype(F32)
            gated[grp] = (a * lax.rsqrt(ms + EPS) * anw_ref[...] * g).astype(BF16)
        y = (x_ref[rows, cols]
             + jnp.dot(r_scr[rows, :], w_ref[ATTN_WIDTH:, cols], preferred_element_type=F32)
             + jnp.dot(gated[grp], w_ref[:ATTN_WIDTH, cols], preferred_element_type=F32))
        y_scr[rows, cols] = y
        if n == n_out - 1:
            yy = y_scr[rows, :]
            ms = jnp.mean(yy * yy, axis=-1, keepdims=True)
            o_ref[rows, :] = yy * lax.rsqrt(ms + EPS) * fw_ref[...]

    per = GROUP_ROWS // BLOCK
    items = []
    for grp in range(n_grp):
        for blk in range(per * grp, per * (grp + 1)):
            for u in range(ATTN_KV_HEADS):
                items.append(("a", (blk, u), grp))
                items.append(("r", (blk, u), grp))
    per_group = len(items) // n_grp
    every = per_group // n_out
    pending = stage[items[0][0]][0](items[0][1])
    for idx, (kind, item, grp) in enumerate(items):
        cur = pending
        if idx + 1 < len(items):
            nxt = items[idx + 1]
            pending = stage[nxt[0]][0](nxt[1])
        stage[kind][1](item, cur)
        pos = idx % per_group
        if grp > 0 and pos % every == every - 1:
            out_chunk(grp - 1, pos // every)
    for n in range(n_out):
        out_chunk(n_grp - 1, n)


def _mixer(proj, pb, x2, bias_t, cap_t, sink_t, lg_f, lg_b, gn_w, attn_nw, w_o_p, final_w,
           batch, seq, tq=512):
    nt = seq // tq
    bpt = tq // BLOCK
    nb = seq // BLOCK
    row = lambda b, i: b * nt + i
    col = lambda off: (lambda b, i: (row(b, i), off // CB))
    prev_blk = lambda b, i: (b * nb + jnp.maximum(bpt * i - 1, 0), OFF_KA // CB)
    next_blk = lambda b, i: (b * nb + jnp.minimum(bpt * (i + 1), nb - 1), OFF_KA // CB)
    const2 = lambda b, i: (0, 0)
    const3 = lambda b, i: (0, 0, 0)
    smem = pl.BlockSpec(memory_space=pltpu.SMEM)
    state_shape = (N_PAIRS, LANES, 2 * RET_VAL_DIM)
    lane_tab = (N_PAIRS, CHUNK, LANES)
    return pl.pallas_call(
        _mixer_kernel,
        grid=(batch, nt),
        in_specs=[
            smem, smem,
            pl.BlockSpec((tq, ATTN_WIDTH), lambda b, i: (row(b, i), OFF_QA // ATTN_WIDTH)),
            pl.BlockSpec((BLOCK, CB), prev_blk),
            pl.BlockSpec((tq, CB), col(OFF_KA)),
            pl.BlockSpec((BLOCK, CB), next_blk),
            pl.BlockSpec((ATTN_HEADS, KW, BLOCK), const3),
            pl.BlockSpec((KW, BLOCK), const2),
            pl.BlockSpec((ATTN_KV_HEADS, 1, ATTN_GROUP * BLOCK), const3),
            pl.BlockSpec((tq, CB), col(OFF_GA)),
            pl.BlockSpec((tq, CB), col(OFF_GA + CB)),
            pl.BlockSpec((tq, CB), col(OFF_QR)),
            pl.BlockSpec((tq, CB), col(OFF_KR)),
            pl.BlockSpec((tq, CB), col(OFF_VR)),
            pl.BlockSpec((tq, CB), col(OFF_VR + CB)),
            pl.BlockSpec((tq, CB), col(OFF_GR)),
            pl.BlockSpec((tq, CB), col(OFF_GR + CB)),
            pl.BlockSpec((bpt, N_PAIRS, PB_ROWS, 2 * RET_VAL_DIM), lambda b, i: (row(b, i), 0, 0, 0)),
            pl.BlockSpec((1, RET_WIDTH), const2),
            pl.BlockSpec((tq, D_MODEL), lambda b, i: (row(b, i), 0)),
            pl.BlockSpec((1, ATTN_WIDTH), const2),
            pl.BlockSpec((ATTN_WIDTH + RET_WIDTH, D_MODEL), const2),
            pl.BlockSpec((1, D_MODEL), const2),
        ],
        out_specs=pl.BlockSpec((tq, D_MODEL), lambda b, i: (row(b, i), 0)),
        out_shape=jax.ShapeDtypeStruct((batch * seq, D_MODEL), F32),
        scratch_shapes=[pltpu.VMEM((tq, ATTN_WIDTH), BF16),
                        pltpu.VMEM((tq, RET_WIDTH), BF16),
                        pltpu.VMEM((tq, D_MODEL), F32),
                        pltpu.VMEM(state_shape, F32),
                        pltpu.VMEM((RET_HEADS, CHUNK, CHUNK), F32),
                        pltpu.VMEM(lane_tab, F32),
                        pltpu.VMEM(lane_tab, F32),
                        pltpu.VMEM(lane_tab, F32),
                        pltpu.VMEM(state_shape, F32)],
        compiler_params=pltpu.CompilerParams(
            dimension_semantics=("arbitrary", "arbitrary"),
            vmem_limit_bytes=VMEM_LIMIT),
        name="mixer",
    )(lg_f, lg_b, proj, proj, proj, proj, bias_t, cap_t, sink_t, proj, proj, proj, proj, proj, proj,
      proj, proj, pb, gn_w, x2, attn_nw, w_o_p, final_w)


def _t5_bucket(rel):
    nb = N_BUCKETS // 2
    max_exact = nb // 2
    ret = jnp.where(rel > 0, nb, 0)
    n = jnp.abs(rel)
    nf = jnp.maximum(n, 1).astype(F32)
    large = max_exact + (jnp.log(nf / max_exact) / math.log(MAX_DISTANCE / max_exact)
                         * (nb - max_exact)).astype(jnp.int32)
    large = jnp.minimum(large, nb - 1)
    return ret + jnp.where(n < max_exact, n, large)


def kernel(x, norm_w, w_in, attn_sink, rel_bias, attn_out_norm_w, ret_decay_fwd,
           ret_decay_bwd, ret_gn_w, w_out, final_norm_w):
    batch, seq, _ = x.shape
    assert norm_w.shape[0] == 1 and seq % BLOCK == 0
    x2 = x.reshape(batch * seq, D_MODEL)

    col_scale = np.ones((1, IN_WIDTH), np.float32)
    col_scale[:, OFF_QA:OFF_QA + ATTN_WIDTH] = ATTN_HEAD_DIM ** -0.5 * LOG2E
    col_scale[:, OFF_KR:OFF_KR + RET_QK_COLS] = RET_KEY_DIM ** -0.5
    w_in_p = (w_in[0] * col_scale).astype(BF16)
    w_o_p = w_out[0].astype(BF16)

    qi = jnp.arange(BLOCK, dtype=jnp.int32)[None, :]
    kt = jnp.arange(KW, dtype=jnp.int32)[:, None]
    rel = kt - BLOCK - qi
    bucket = _t5_bucket(rel) & (N_BUCKETS - 1)
    onehot = (bucket[None] == jnp.arange(N_BUCKETS)[:, None, None]).astype(F32)
    rb = rel_bias.astype(F32).reshape(N_BUCKETS, ATTN_KV_HEADS, 2, 2)
    rb = jnp.swapaxes(rb, 2, 3).reshape(N_BUCKETS, ATTN_HEADS)
    bias_t = jnp.einsum("nh,nkq->hkq", rb, onehot,
                        precision=lax.Precision.HIGHEST)
    bias_t = bias_t * LOG2E
    cap_t = jnp.where(jnp.abs(rel) <= WINDOW, BIG, NEG).astype(F32)
    sink = (attn_sink[0].astype(F32) * LOG2E).reshape(ATTN_KV_HEADS, 1, 2, 2)
    sink = jnp.swapaxes(sink, 2, 3)[..., None]
    sink_t = jnp.broadcast_to(sink, (ATTN_KV_HEADS, 1, 2, 2, BLOCK)).reshape(
        ATTN_KV_HEADS, 1, ATTN_GROUP * BLOCK)

    inv = ROPE_BASE ** (-jnp.arange(0, RET_KEY_DIM, 2, dtype=F32) / RET_KEY_DIM)
    inv_full = jnp.tile(jnp.repeat(inv, 2), LANES // RET_KEY_DIM)[None, :]
    sgn = np.tile(np.array([-1.0, 1.0], np.float32), LANES // 2)[None, :]

    def trig_table(pos):
        ang = pos[:, None] * inv_full
        return jnp.stack([jnp.cos(ang), jnp.sin(ang), jnp.sin(ang) * sgn])

    rowtab = trig_table(jnp.arange(2 * BLOCK, dtype=F32))
    basetab = jnp.swapaxes(trig_table(jnp.arange(0, seq, BLOCK, dtype=F32)), 0, 1)

    lg_f = jax.nn.log_sigmoid(ret_decay_fwd[0].astype(F32))
    lg_b = jax.nn.log_sigmoid(ret_decay_bwd[0].astype(F32))

    proj = _inproj(x2, norm_w[0].reshape(1, D_MODEL), w_in_p, rowtab, basetab, seq)
    pb = _retention_states(proj, lg_b, batch, seq)
    out = _mixer(proj, pb, x2, bias_t, cap_t, sink_t, lg_f, lg_b,
                 ret_gn_w[0].reshape(1, RET_WIDTH), attn_out_norm_w[0].reshape(1, ATTN_WIDTH),
                 w_o_p, final_norm_w.reshape(1, D_MODEL), batch, seq)
    return out.reshape(batch, seq, D_MODEL)
```

```python
import functools
import math

import numpy as np
import jax
import jax.numpy as jnp
from jax import lax
from jax.experimental import pallas as pl
from jax.experimental.pallas import tpu as pltpu

D_MODEL = 2048
ATTN_HEAD_DIM = 64
ATTN_WIDTH = 1024
ATTN_HEADS = 16
ATTN_KV_HEADS = 4
ATTN_GROUP = 4
ATTN_KV_COLS = 256
WINDOW = 128
BLOCK = 128
N_BUCKETS = 32
MAX_DISTANCE = 128
RET_WIDTH = 1024
RET_HEADS = 8
RET_VAL_DIM = 128
RET_KEY_DIM = 64
RET_QK_COLS = 512
CHUNK = 128
ROPE_BASE = 10000.0
EPS = 1e-6
NEG = -1e30
BIG = 3e38
LOG2E = math.log2(math.e)

LANES = 128
N_PAIRS = RET_HEADS // 2
KW = 3 * BLOCK
CB = 512

OFF_QA = 0
OFF_KA = OFF_QA + ATTN_WIDTH
OFF_GA = OFF_KA + 2 * ATTN_KV_COLS
OFF_QR = OFF_GA + ATTN_WIDTH
OFF_KR = OFF_QR + RET_QK_COLS
OFF_VR = OFF_KR + RET_QK_COLS
OFF_GR = OFF_VR + RET_WIDTH
IN_WIDTH = OFF_GR + RET_WIDTH

VMEM_LIMIT = 56 * 1024 * 1024

F32 = jnp.float32
BF16 = jnp.bfloat16


def _roll_half(x):
    return pltpu.roll(x.astype(F32), LANES // 2, axis=1).astype(x.dtype)


def _silu(g):
    return g / (1.0 + jnp.exp(-g))


def _inproj_kernel(x_ref, nw_ref, w_ref, rowtab_ref, basetab_ref, o_ref, *, row_splits,
                   tiles_per_seq):
    tm = x_ref.shape[0]
    nw = nw_ref[...]
    tile_blk = (pl.program_id(0) % tiles_per_seq) * (tm // BLOCK)
    start = 0
    for rc in row_splits:
        rows = slice(start, start + rc)
        x = x_ref[rows, :]
        ms = jnp.mean(x * x, axis=-1, keepdims=True)
        h = (x * lax.rsqrt(ms + EPS) * nw).astype(BF16)
        ca, sa, sas = rowtab_ref[0, :rc, :], rowtab_ref[1, :rc, :], rowtab_ref[2, :rc, :]
        base = basetab_ref[tile_blk + start // BLOCK]
        cb, sb, sbs = base[0:1, :], base[1:2, :], base[2:3, :]
        cos = ca * cb - sa * sb
        sin_signed = sas * cb + ca * sbs
        lane = lax.broadcasted_iota(jnp.int32, (rc, LANES), 1)
        even = (lane % 2) == 0
        start += rc
        for col in range(0, IN_WIDTH, CB):
            res = jnp.dot(h, w_ref[:, col:col + CB], preferred_element_type=F32)
            if OFF_GA <= col < OFF_QR or col >= OFF_GR:
                res = _silu(res)
            elif OFF_QR <= col < OFF_VR:
                parts = []
                for c in range(CB // LANES):
                    v = res[:, LANES * c:LANES * (c + 1)]
                    partner = jnp.where(even, pltpu.roll(v, LANES - 1, axis=1),
                                        pltpu.roll(v, 1, axis=1))
                    parts.append(v * cos + partner * sin_signed)
                res = jnp.concatenate(parts, axis=1)
            o_ref[rows, col:col + CB] = res.astype(o_ref.dtype)


def _inproj(x2, norm_w, w_in_p, rowtab, basetab, seq, tm=512, row_splits=(256, 256)):
    m = x2.shape[0]
    assert sum(row_splits) == tm and max(row_splits) <= rowtab.shape[1]
    return pl.pallas_call(
        functools.partial(_inproj_kernel, row_splits=row_splits, tiles_per_seq=seq // tm),
        grid=(m // tm,),
        in_specs=[
            pl.BlockSpec((tm, D_MODEL), lambda i: (i, 0)),
            pl.BlockSpec((1, D_MODEL), lambda i: (0, 0)),
            pl.BlockSpec((D_MODEL, IN_WIDTH), lambda i: (0, 0)),
            pl.BlockSpec(rowtab.shape, lambda i: (0, 0, 0)),
            pl.BlockSpec(basetab.shape, lambda i: (0, 0, 0)),
        ],
        out_specs=pl.BlockSpec((tm, IN_WIDTH), lambda i: (i, 0)),
        out_shape=jax.ShapeDtypeStruct((m, IN_WIDTH), BF16),
        compiler_params=pltpu.CompilerParams(
            dimension_semantics=("arbitrary",),
            vmem_limit_bytes=VMEM_LIMIT),
        name="inproj",
    )(x2, norm_w, w_in_p, rowtab, basetab)


PB_ROWS = RET_KEY_DIM


def _first_step():
    return jnp.logical_and(pl.program_id(0) == 0, pl.program_id(1) == 0)


def _lane_decay(lg_ref, c, expo):
    lane = lax.broadcasted_iota(jnp.int32, (CHUNK, LANES), 1)
    lg = jnp.where(lane < RET_KEY_DIM, lg_ref[2 * c], lg_ref[2 * c + 1])
    return jnp.exp(lg * expo)


def _state_mask():
    r = lax.broadcasted_iota(jnp.int32, (LANES, 2 * RET_VAL_DIM), 0)
    m = lax.broadcasted_iota(jnp.int32, (LANES, 2 * RET_VAL_DIM), 1)
    return (r // RET_KEY_DIM) == (m // RET_VAL_DIM)


def _chunk_decay(lg_ref, c):
    r = lax.broadcasted_iota(jnp.int32, (LANES, 2 * RET_VAL_DIM), 0)
    lg = jnp.where(r < RET_KEY_DIM, lg_ref[2 * c], lg_ref[2 * c + 1])
    return jnp.where(_state_mask(), jnp.exp(lg * float(CHUNK)), 0.0)


def _row_index():
    return lax.broadcasted_iota(jnp.int32, (CHUNK, LANES), 0).astype(F32)


def _state_update(st_ref, c, k_rot, kdec, gdec, vpair):
    kd = (k_rot * kdec).astype(BF16)
    upd = lax.dot_general(kd, vpair, (((0,), (0,)), ((), ())),
                          preferred_element_type=F32)
    st_ref[c] = st_ref[c] * gdec + jnp.where(_state_mask(), upd, 0.0)


def _pair_cols(refs, rows, c, width):
    per_ref = CB // width
    return refs[c // per_ref][rows, width * (c % per_ref):width * (c % per_ref + 1)]


def _retstate_kernel(lgb_ref, k_ref, v0_ref, v1_ref, pb_ref, st_ref, kdb_ref, gb_ref):
    @pl.when(_first_step())
    def _():
        idx = _row_index()
        for c in range(N_PAIRS):
            kdb_ref[c] = _lane_decay(lgb_ref, c, idx)
            gb_ref[c] = _chunk_decay(lgb_ref, c)

    @pl.when(pl.program_id(1) == 0)
    def _():
        st_ref[...] = jnp.zeros_like(st_ref)

    n_ch = k_ref.shape[0] // CHUNK
    for ch in reversed(range(n_ch)):
        rows = slice(CHUNK * ch, CHUNK * (ch + 1))
        for c in range(N_PAIRS):
            st = st_ref[c]
            pb_ref[ch, c] = (st[:PB_ROWS] + st[PB_ROWS:]).astype(BF16)
            k = k_ref[rows, LANES * c:LANES * (c + 1)].astype(F32)
            vpair = _pair_cols((v0_ref, v1_ref), rows, c, 2 * RET_VAL_DIM)
            _state_update(st_ref, c, k, kdb_ref[c], gb_ref[c], vpair)


def _retention_states(proj, lg_b, batch, seq, tt=2048):
    nt = seq // tt
    cpt = tt // CHUNK
    rrow = lambda b, i: b * nt + (nt - 1 - i)
    col = lambda off: (lambda b, i: (rrow(b, i), off // CB))
    return pl.pallas_call(
        _retstate_kernel,
        grid=(batch, nt),
        in_specs=[
            pl.BlockSpec(memory_space=pltpu.SMEM),
            pl.BlockSpec((tt, CB), col(OFF_KR)),
            pl.BlockSpec((tt, CB), col(OFF_VR)),
            pl.BlockSpec((tt, CB), col(OFF_VR + CB)),
        ],
        out_specs=pl.BlockSpec((cpt, N_PAIRS, PB_ROWS, 2 * RET_VAL_DIM),
                               lambda b, i: (rrow(b, i), 0, 0, 0)),
        out_shape=jax.ShapeDtypeStruct((batch * nt * cpt, N_PAIRS, PB_ROWS, 2 * RET_VAL_DIM), BF16),
        scratch_shapes=[pltpu.VMEM((N_PAIRS, LANES, 2 * RET_VAL_DIM), F32),
                        pltpu.VMEM((N_PAIRS, CHUNK, LANES), F32),
                        pltpu.VMEM((N_PAIRS, LANES, 2 * RET_VAL_DIM), F32)],
        compiler_params=pltpu.CompilerParams(
            dimension_semantics=("arbitrary", "arbitrary"),
            vmem_limit_bytes=VMEM_LIMIT),
        name="ret_states",
    )(lg_b, proj, proj, proj)


GROUP_ROWS = 2 * BLOCK
OUT_COLS = 512


def _attention_stage(i, nt, q_ref, kvp_ref, kvc_ref, kvn_ref, bias_ref, cap_ref, sink_ref, a_scr):
    n_blk = q_ref.shape[0] // BLOCK
    hd = ATTN_HEAD_DIM
    kv = jnp.concatenate([kvp_ref[...], kvc_ref[...], kvn_ref[...]], axis=0)
    lane = lax.broadcasted_iota(jnp.int32, (kv.shape[0], LANES), 1)
    low = lane < hd

    tile4 = lambda c: jnp.concatenate([c] * ATTN_GROUP, axis=1)
    cap_l, cap_r = tile4(cap_ref[:BLOCK]), tile4(cap_ref[2 * BLOCK:])
    cap_first = jnp.where(i > 0, cap_l, NEG)
    cap_last = jnp.where(i < nt - 1, cap_r, NEG)

    k_side, v_ones = {}, {}
    for j in range(ATTN_KV_COLS // LANES):
        kj = kv[:, LANES * j:LANES * (j + 1)]
        vj = kv[:, ATTN_KV_COLS + LANES * j:ATTN_KV_COLS + LANES * (j + 1)]
        for hg in range(2):
            mine = low if hg == 0 else jnp.logical_not(low)
            k_here = jnp.where(mine, kj, jnp.zeros_like(kj))
            k_side[(2 * j + hg, hg)] = k_here
            k_side[(2 * j + hg, 1 - hg)] = _roll_half(k_here)
            v_ones[2 * j + hg] = jnp.where(mine, vj, jnp.ones_like(vj))

    def scores(item):
        blk, g = item
        win = slice(BLOCK * blk, BLOCK * blk + KW)
        qs = jnp.concatenate(
            [q_ref[BLOCK * blk:BLOCK * (blk + 1), LANES * c:LANES * (c + 1)]
             for c in (2 * g, 2 * g + 1)], axis=0)
        return [lax.dot_general(k_side[(g, half)][win], qs, (((1,), (1,)), ((), ())),
                                preferred_element_type=F32) for half in range(2)]

    def softmax(item, st_pair):
        blk, g = item
        c_l = cap_first if blk == 0 else cap_l
        c_r = cap_last if blk == n_blk - 1 else cap_r
        st = jnp.concatenate(st_pair, axis=1)
        b = jnp.concatenate([bias_ref[ATTN_GROUP * g + t] for t in range(ATTN_GROUP)], axis=1)
        t = jnp.concatenate([
            jnp.minimum(st[:BLOCK] + b[:BLOCK], c_l),
            st[BLOCK:2 * BLOCK] + b[BLOCK:2 * BLOCK],
            jnp.minimum(st[2 * BLOCK:] + b[2 * BLOCK:], c_r),
        ], axis=0)
        sk = sink_ref[g]
        m = jnp.maximum(jnp.max(t, axis=0, keepdims=True), sk)
        return jnp.exp2(t - m).astype(BF16), jnp.exp2(sk - m)

    def finish(item, probs):
        blk, g = item
        p, e_sink = probs
        win = slice(BLOCK * blk, BLOCK * blk + KW)
        ot = lax.dot_general(v_ones[g][win], p, (((0,), (0,)), ((), ())),
                             preferred_element_type=F32)
        hg = g % 2
        num = ot[hd * hg:hd * (hg + 1)]
        den = ot[hd * (1 - hg):hd * (1 - hg) + 1]
        out = num * (1.0 / (den + e_sink))
        out_t = jnp.concatenate([out[:, :2 * BLOCK], out[:, 2 * BLOCK:]], axis=0)
        for r in range(2):
            c = 2 * g + r
            a_scr[BLOCK * blk:BLOCK * (blk + 1), LANES * c:LANES * (c + 1)] = (
                out_t[:, BLOCK * r:BLOCK * (r + 1)].T.astype(a_scr.dtype))

    return scores, softmax, finish


def _retention_stage(q_ref, k_ref, v_refs, g_refs, pb_ref, gnw_ref, st_ref, d_ref, qdf_ref,
                     qdb_ref, kdf_ref, gf_ref, r_scr):
    lane = lax.broadcasted_iota(jnp.int32, (CHUNK, LANES), 1)
    low_half = lane < RET_KEY_DIM
    first_v = lax.broadcasted_iota(jnp.int32, (CHUNK, 2 * RET_VAL_DIM), 1) < RET_VAL_DIM
    first_s = lax.broadcasted_iota(jnp.int32, (PB_ROWS, 2 * RET_VAL_DIM), 1) < RET_VAL_DIM

    def scores(item):
        ch, c = item
        rows = slice(CHUNK * ch, CHUNK * (ch + 1))
        kb = k_ref[rows, LANES * c:LANES * (c + 1)]
        q = q_ref[rows, LANES * c:LANES * (c + 1)]
        out = []
        for half in range(2):
            sel = low_half if half == 0 else jnp.logical_not(low_half)
            qh = jnp.where(sel, q, jnp.zeros_like(q))
            out.append(lax.dot_general(qh, kb, (((1,), (1,)), ((), ())),
                                       preferred_element_type=F32))
        return out

    def decay(item, s_pair):
        ch, c = item
        rows = slice(CHUNK * ch, CHUNK * (ch + 1))
        q = q_ref[rows, LANES * c:LANES * (c + 1)].astype(F32)
        parts = [(s_pair[half] * d_ref[2 * c + half]).astype(BF16) for half in range(2)]
        parts.append((q * qdf_ref[c]).astype(BF16))
        parts.append((q * qdb_ref[c]).astype(BF16))
        return jnp.concatenate(parts, axis=1)

    def finish(item, lhs):
        ch, c = item
        rows = slice(CHUNK * ch, CHUNK * (ch + 1))
        k = k_ref[rows, LANES * c:LANES * (c + 1)].astype(F32)
        vpair = _pair_cols(v_refs, rows, c, 2 * RET_VAL_DIM)
        zero_v = jnp.zeros_like(vpair)
        pb = pb_ref[ch, c]
        zero_s = jnp.zeros_like(pb)
        rhs = jnp.concatenate([
            jnp.where(first_v, vpair, zero_v),
            jnp.where(first_v, zero_v, vpair),
            st_ref[c].astype(BF16),
            jnp.where(first_s, pb, zero_s),
            jnp.where(first_s, zero_s, pb),
        ], axis=0)
        o = jnp.dot(lhs, rhs, preferred_element_type=F32)
        _state_update(st_ref, c, k, kdf_ref[c], gf_ref[c], vpair)
        for half in range(2):
            head = 2 * c + half
            lo_c = RET_VAL_DIM * head
            oh = o[:, RET_VAL_DIM * half:RET_VAL_DIM * (half + 1)]
            mu = jnp.mean(oh, axis=-1, keepdims=True)
            dev = oh - mu
            var = jnp.mean(dev * dev, axis=-1, keepdims=True)
            y = dev * lax.rsqrt(var + EPS) * gnw_ref[:, lo_c:lo_c + RET_VAL_DIM]
            g = _pair_cols(g_refs, rows, head, RET_VAL_DIM).astype(F32)
            r_scr[rows, lo_c:lo_c + RET_VAL_DIM] = (y * g).astype(r_scr.dtype)

    return scores, decay, finish


def _mixer_kernel(lgf_ref, lgb_ref, qa_ref, kvp_ref, kvc_ref, kvn_ref, bias_ref, cap_ref, sink_ref,
                  ga0_ref, ga1_ref, qr_ref, kr_ref, v0_ref, v1_ref, gr0_ref, gr1_ref, pb_ref,
                  gnw_ref, x_ref, anw_ref, w_ref, fw_ref, o_ref,
                  a_scr, r_scr, y_scr, st_ref, d_ref, qdf_ref, qdb_ref, kdf_ref, gf_ref):
    @pl.when(_first_step())
    def _():
        idx = _row_index()
        col = lax.broadcasted_iota(jnp.int32, (CHUNK, CHUNK), 1).astype(F32)
        diff = idx - col
        for c in range(N_PAIRS):
            qdf_ref[c] = _lane_decay(lgf_ref, c, idx + 1.0)
            qdb_ref[c] = _lane_decay(lgb_ref, c, float(CHUNK) - idx)
            kdf_ref[c] = _lane_decay(lgf_ref, c, float(CHUNK - 1) - idx)
            gf_ref[c] = _chunk_decay(lgf_ref, c)
        for hh in range(RET_HEADS):
            d_ref[hh] = jnp.exp(jnp.where(diff >= 0.0, lgf_ref[hh] * diff, lgb_ref[hh] * (-diff)))

    @pl.when(pl.program_id(1) == 0)
    def _():
        st_ref[...] = jnp.zeros_like(st_ref)

    tq = qa_ref.shape[0]
    stage = {
        "a": _attention_stage(pl.program_id(1), pl.num_programs(1), qa_ref, kvp_ref, kvc_ref,
                              kvn_ref, bias_ref, cap_ref, sink_ref, a_scr),
        "r": _retention_stage(qr_ref, kr_ref, (v0_ref, v1_ref), (gr0_ref, gr1_ref), pb_ref,
                              gnw_ref, st_ref, d_ref, qdf_ref, qdb_ref, kdf_ref, gf_ref, r_scr),
    }

    n_grp = tq // GROUP_ROWS
    n_out = D_MODEL // OUT_COLS
    gated = {}

    def out_chunk(grp, n):
        rows = slice(GROUP_ROWS * grp, GROUP_ROWS * (grp + 1))
        cols = slice(OUT_COLS * n, OUT_COLS * (n + 1))
        if grp not in gated:
            a = a_scr[rows, :].astype(F32)
            ms = jnp.mean(a * a, axis=-1, keepdims=True)
            g = jnp.concatenate([ga0_ref[rows, :], ga1_ref[rows, :]], axis=1).astype(F32)
            gated[grp] = (a * lax.rsqrt(ms + EPS) * anw_ref[...] * g).astype(BF16)
        y = (x_ref[rows, cols]
             + jnp.dot(r_scr[rows, :], w_ref[ATTN_WIDTH:, cols], preferred_element_type=F32)
             + jnp.dot(gated[grp], w_ref[:ATTN_WIDTH, cols], preferred_element_type=F32))
        y_scr[rows, cols] = y
        if n == n_out - 1:
            yy = y_scr[rows, :]
            ms = jnp.mean(yy * yy, axis=-1, keepdims=True)
            o_ref[rows, :] = yy * lax.rsqrt(ms + EPS) * fw_ref[...]

    per = GROUP_ROWS // BLOCK
    items = []
    for grp in range(n_grp):
        for blk in range(per * grp, per * (grp + 1)):
            for u in range(ATTN_KV_HEADS):
                items.append(("a", (blk, u), grp))
                items.append(("r", (blk, u), grp))
    per_group = len(items) // n_grp
    every = per_group // n_out
    pending = stage[items[0][0]][0](items[0][1])
    closing = None
    for idx, (kind, item, grp) in enumerate(items):
        cur = pending
        if idx + 1 < len(items):
            nxt = items[idx + 1]
            pending = stage[nxt[0]][0](nxt[1])
        mid = stage[kind][1](item, cur)
        if closing is not None:
            stage[closing[0]][2](closing[1], closing[2])
        closing = (kind, item, mid)
        pos = idx % per_group
        if grp > 0 and pos % every == every - 1:
            out_chunk(grp - 1, pos // every)
    stage[closing[0]][2](closing[1], closing[2])
    for n in range(n_out):
        out_chunk(n_grp - 1, n)


def _mixer(proj, pb, x2, bias_t, cap_t, sink_t, lg_f, lg_b, gn_w, attn_nw, w_o_p, final_w,
           batch, seq, tq=512):
    nt = seq // tq
    bpt = tq // BLOCK
    nb = seq // BLOCK
    row = lambda b, i: b * nt + i
    col = lambda off: (lambda b, i: (row(b, i), off // CB))
    prev_blk = lambda b, i: (b * nb + jnp.maximum(bpt * i - 1, 0), OFF_KA // CB)
    next_blk = lambda b, i: (b * nb + jnp.minimum(bpt * (i + 1), nb - 1), OFF_KA // CB)
    const2 = lambda b, i: (0, 0)
    const3 = lambda b, i: (0, 0, 0)
    smem = pl.BlockSpec(memory_space=pltpu.SMEM)
    state_shape = (N_PAIRS, LANES, 2 * RET_VAL_DIM)
    lane_tab = (N_PAIRS, CHUNK, LANES)
    return pl.pallas_call(
        _mixer_kernel,
        grid=(batch, nt),
        in_specs=[
            smem, smem,
            pl.BlockSpec((tq, ATTN_WIDTH), lambda b, i: (row(b, i), OFF_QA // ATTN_WIDTH)),
            pl.BlockSpec((BLOCK, CB), prev_blk),
            pl.BlockSpec((tq, CB), col(OFF_KA)),
            pl.BlockSpec((BLOCK, CB), next_blk),
            pl.BlockSpec((ATTN_HEADS, KW, BLOCK), const3),
            pl.BlockSpec((KW, BLOCK), const2),
            pl.BlockSpec((ATTN_KV_HEADS, 1, ATTN_GROUP * BLOCK), const3),
            pl.BlockSpec((tq, CB), col(OFF_GA)),
            pl.BlockSpec((tq, CB), col(OFF_GA + CB)),
            pl.BlockSpec((tq, CB), col(OFF_QR)),
            pl.BlockSpec((tq, CB), col(OFF_KR)),
            pl.BlockSpec((tq, CB), col(OFF_VR)),
            pl.BlockSpec((tq, CB), col(OFF_VR + CB)),
            pl.BlockSpec((tq, CB), col(OFF_GR)),
            pl.BlockSpec((tq, CB), col(OFF_GR + CB)),
            pl.BlockSpec((bpt, N_PAIRS, PB_ROWS, 2 * RET_VAL_DIM), lambda b, i: (row(b, i), 0, 0, 0)),
            pl.BlockSpec((1, RET_WIDTH), const2),
            pl.BlockSpec((tq, D_MODEL), lambda b, i: (row(b, i), 0)),
            pl.BlockSpec((1, ATTN_WIDTH), const2),
            pl.BlockSpec((ATTN_WIDTH + RET_WIDTH, D_MODEL), const2),
            pl.BlockSpec((1, D_MODEL), const2),
        ],
        out_specs=pl.BlockSpec((tq, D_MODEL), lambda b, i: (row(b, i), 0)),
        out_shape=jax.ShapeDtypeStruct((batch * seq, D_MODEL), F32),
        scratch_shapes=[pltpu.VMEM((tq, ATTN_WIDTH), BF16),
                        pltpu.VMEM((tq, RET_WIDTH), BF16),
                        pltpu.VMEM((tq, D_MODEL), F32),
                        pltpu.VMEM(state_shape, F32),
                        pltpu.VMEM((RET_HEADS, CHUNK, CHUNK), F32),
                        pltpu.VMEM(lane_tab, F32),
                        pltpu.VMEM(lane_tab, F32),
                        pltpu.VMEM(lane_tab, F32),
                        pltpu.VMEM(state_shape, F32)],
        compiler_params=pltpu.CompilerParams(
            dimension_semantics=("arbitrary", "arbitrary"),
            vmem_limit_bytes=VMEM_LIMIT),
        name="mixer",
    )(lg_f, lg_b, proj, proj, proj, proj, bias_t, cap_t, sink_t, proj, proj, proj, proj, proj, proj,
      proj, proj, pb, gn_w, x2, attn_nw, w_o_p, final_w)


def _t5_bucket(rel):
    nb = N_BUCKETS // 2
    max_exact = nb // 2
    ret = jnp.where(rel > 0, nb, 0)
    n = jnp.abs(rel)
    nf = jnp.maximum(n, 1).astype(F32)
    large = max_exact + (jnp.log(nf / max_exact) / math.log(MAX_DISTANCE / max_exact)
                         * (nb - max_exact)).astype(jnp.int32)
    large = jnp.minimum(large, nb - 1)
    return ret + jnp.where(n < max_exact, n, large)


def kernel(x, norm_w, w_in, attn_sink, rel_bias, attn_out_norm_w, ret_decay_fwd,
           ret_decay_bwd, ret_gn_w, w_out, final_norm_w):
    batch, seq, _ = x.shape
    assert norm_w.shape[0] == 1 and seq % BLOCK == 0
    x2 = x.reshape(batch * seq, D_MODEL)

    col_scale = np.ones((1, IN_WIDTH), np.float32)
    col_scale[:, OFF_QA:OFF_QA + ATTN_WIDTH] = ATTN_HEAD_DIM ** -0.5 * LOG2E
    col_scale[:, OFF_KR:OFF_KR + RET_QK_COLS] = RET_KEY_DIM ** -0.5
    w_in_p = (w_in[0] * col_scale).astype(BF16)
    w_o_p = w_out[0].astype(BF16)

    qi = jnp.arange(BLOCK, dtype=jnp.int32)[None, :]
    kt = jnp.arange(KW, dtype=jnp.int32)[:, None]
    rel = kt - BLOCK - qi
    bucket = _t5_bucket(rel) & (N_BUCKETS - 1)
    onehot = (bucket[None] == jnp.arange(N_BUCKETS)[:, None, None]).astype(F32)
    rb = rel_bias.astype(F32).reshape(N_BUCKETS, ATTN_KV_HEADS, 2, 2)
    rb = jnp.swapaxes(rb, 2, 3).reshape(N_BUCKETS, ATTN_HEADS)
    bias_t = jnp.einsum("nh,nkq->hkq", rb, onehot,
                        precision=lax.Precision.HIGHEST)
    bias_t = bias_t * LOG2E
    cap_t = jnp.where(jnp.abs(rel) <= WINDOW, BIG, NEG).astype(F32)
    sink = (attn_sink[0].astype(F32) * LOG2E).reshape(ATTN_KV_HEADS, 1, 2, 2)
    sink = jnp.swapaxes(sink, 2, 3)[..., None]
    sink_t = jnp.broadcast_to(sink, (ATTN_KV_HEADS, 1, 2, 2, BLOCK)).reshape(
        ATTN_KV_HEADS, 1, ATTN_GROUP * BLOCK)

    inv = ROPE_BASE ** (-jnp.arange(0, RET_KEY_DIM, 2, dtype=F32) / RET_KEY_DIM)
    inv_full = jnp.tile(jnp.repeat(inv, 2), LANES // RET_KEY_DIM)[None, :]
    sgn = np.tile(np.array([-1.0, 1.0], np.float32), LANES // 2)[None, :]

    def trig_table(pos):
        ang = pos[:, None] * inv_full
        return jnp.stack([jnp.cos(ang), jnp.sin(ang), jnp.sin(ang) * sgn])

    rowtab = trig_table(jnp.arange(2 * BLOCK, dtype=F32))
    basetab = jnp.swapaxes(trig_table(jnp.arange(0, seq, BLOCK, dtype=F32)), 0, 1)

    lg_f = jax.nn.log_sigmoid(ret_decay_fwd[0].astype(F32))
    lg_b = jax.nn.log_sigmoid(ret_decay_bwd[0].astype(F32))

    proj = _inproj(x2, norm_w[0].reshape(1, D_MODEL), w_in_p, rowtab, basetab, seq)
    pb = _retention_states(proj, lg_b, batch, seq)
    out = _mixer(proj, pb, x2, bias_t, cap_t, sink_t, lg_f, lg_b,
                 ret_gn_w[0].reshape(1, RET_WIDTH), attn_out_norm_w[0].reshape(1, ATTN_WIDTH),
                 w_o_p, final_norm_w.reshape(1, D_MODEL), batch, seq)
    return out.reshape(batch, seq, D_MODEL)
```

```python
import functools
import math

import numpy as np
import jax
import jax.numpy as jnp
from jax import lax
from jax.experimental import pallas as pl
from jax.experimental.pallas import tpu as pltpu

D_MODEL = 2048
ATTN_HEAD_DIM = 64
ATTN_WIDTH = 1024
ATTN_HEADS = 16
ATTN_KV_HEADS = 4
ATTN_GROUP = 4
ATTN_KV_COLS = 256
WINDOW = 128
BLOCK = 128
N_BUCKETS = 32
MAX_DISTANCE = 128
RET_WIDTH = 1024
RET_HEADS = 8
RET_VAL_DIM = 128
RET_KEY_DIM = 64
RET_QK_COLS = 512
CHUNK = 128
ROPE_BASE = 10000.0
EPS = 1e-6
NEG = -1e30
BIG = 3e38
LOG2E = math.log2(math.e)

LANES = 128
N_PAIRS = RET_HEADS // 2
KW = 3 * BLOCK
CB = 512

OFF_QA = 0
OFF_KA = OFF_QA + ATTN_WIDTH
OFF_GA = OFF_KA + 2 * ATTN_KV_COLS
OFF_QR = OFF_GA + ATTN_WIDTH
OFF_KR = OFF_QR + RET_QK_COLS
OFF_VR = OFF_KR + RET_QK_COLS
OFF_GR = OFF_VR + RET_WIDTH
IN_WIDTH = OFF_GR + RET_WIDTH

VMEM_LIMIT = 56 * 1024 * 1024

F32 = jnp.float32
BF16 = jnp.bfloat16


def _roll_half(x):
    return pltpu.roll(x.astype(F32), LANES // 2, axis=1).astype(x.dtype)


def _silu(g):
    return g / (1.0 + jnp.exp(-g))


def _inproj_kernel(x_ref, nw_ref, w_ref, rowtab_ref, basetab_ref, o_ref, *, row_splits,
                   tiles_per_seq):
    tm = x_ref.shape[0]
    nw = nw_ref[...]
    tile_blk = (pl.program_id(0) % tiles_per_seq) * (tm // BLOCK)
    start = 0
    for rc in row_splits:
        rows = slice(start, start + rc)
        x = x_ref[rows, :]
        ms = jnp.mean(x * x, axis=-1, keepdims=True)
        h = (x * lax.rsqrt(ms + EPS) * nw).astype(BF16)
        ca, sa, sas = rowtab_ref[0, :rc, :], rowtab_ref[1, :rc, :], rowtab_ref[2, :rc, :]
        base = basetab_ref[tile_blk + start // BLOCK]
        cb, sb, sbs = base[0:1, :], base[1:2, :], base[2:3, :]
        cos = ca * cb - sa * sb
        sin_signed = sas * cb + ca * sbs
        lane = lax.broadcasted_iota(jnp.int32, (rc, LANES), 1)
        even = (lane % 2) == 0
        start += rc
        for col in range(0, IN_WIDTH, CB):
            res = jnp.dot(h, w_ref[:, col:col + CB], preferred_element_type=F32)
            if OFF_GA <= col < OFF_QR or col >= OFF_GR:
                res = _silu(res)
            elif OFF_QR <= col < OFF_VR:
                parts = []
                for c in range(CB // LANES):
                    v = res[:, LANES * c:LANES * (c + 1)]
                    partner = jnp.where(even, pltpu.roll(v, LANES - 1, axis=1),
                                        pltpu.roll(v, 1, axis=1))
                    parts.append(v * cos + partner * sin_signed)
                res = jnp.concatenate(parts, axis=1)
            o_ref[rows, col:col + CB] = res.astype(o_ref.dtype)


def _inproj(x2, norm_w, w_in_p, rowtab, basetab, seq, tm=512, row_splits=(256, 256)):
    m = x2.shape[0]
    assert sum(row_splits) == tm and max(row_splits) <= rowtab.shape[1]
    return pl.pallas_call(
        functools.partial(_inproj_kernel, row_splits=row_splits, tiles_per_seq=seq // tm),
        grid=(m // tm,),
        in_specs=[
            pl.BlockSpec((tm, D_MODEL), lambda i: (i, 0)),
            pl.BlockSpec((1, D_MODEL), lambda i: (0, 0)),
            pl.BlockSpec((D_MODEL, IN_WIDTH), lambda i: (0, 0)),
            pl.BlockSpec(rowtab.shape, lambda i: (0, 0, 0)),
            pl.BlockSpec(basetab.shape, lambda i: (0, 0, 0)),
        ],
        out_specs=pl.BlockSpec((tm, IN_WIDTH), lambda i: (i, 0)),
        out_shape=jax.ShapeDtypeStruct((m, IN_WIDTH), BF16),
        compiler_params=pltpu.CompilerParams(
            dimension_semantics=("arbitrary",),
            vmem_limit_bytes=VMEM_LIMIT),
        name="inproj",
    )(x2, norm_w, w_in_p, rowtab, basetab)


PB_ROWS = RET_KEY_DIM


def _first_step():
    return jnp.logical_and(pl.program_id(0) == 0, pl.program_id(1) == 0)


def _lane_decay(lg_ref, c, expo):
    lane = lax.broadcasted_iota(jnp.int32, (CHUNK, LANES), 1)
    lg = jnp.where(lane < RET_KEY_DIM, lg_ref[2 * c], lg_ref[2 * c + 1])
    return jnp.exp(lg * expo)


def _state_mask():
    r = lax.broadcasted_iota(jnp.int32, (LANES, 2 * RET_VAL_DIM), 0)
    m = lax.broadcasted_iota(jnp.int32, (LANES, 2 * RET_VAL_DIM), 1)
    return (r // RET_KEY_DIM) == (m // RET_VAL_DIM)


def _chunk_decay(lg_ref, c):
    r = lax.broadcasted_iota(jnp.int32, (LANES, 2 * RET_VAL_DIM), 0)
    lg = jnp.where(r < RET_KEY_DIM, lg_ref[2 * c], lg_ref[2 * c + 1])
    return jnp.where(_state_mask(), jnp.exp(lg * float(CHUNK)), 0.0)


def _row_index():
    return lax.broadcasted_iota(jnp.int32, (CHUNK, LANES), 0).astype(F32)


def _state_update(st_ref, c, k_rot, kdec, gdec, vpair, alive=None):
    kd = (k_rot * kdec).astype(BF16)
    upd = lax.dot_general(kd, vpair, (((0,), (0,)), ((), ())),
                          preferred_element_type=F32)
    new = st_ref[c] * gdec + jnp.where(_state_mask(), upd, 0.0)
    st_ref[c] = new if alive is None else new * alive


def _pair_cols(refs, rows, c, width):
    per_ref = CB // width
    return refs[c // per_ref][rows, width * (c % per_ref):width * (c % per_ref + 1)]


def _retstate_kernel(lgb_ref, k_ref, v0_ref, v1_ref, pb_ref, st_ref, kdb_ref, gb_ref):
    @pl.when(_first_step())
    def _():
        idx = _row_index()
        for c in range(N_PAIRS):
            kdb_ref[c] = _lane_decay(lgb_ref, c, idx)
            gb_ref[c] = _chunk_decay(lgb_ref, c)

    @pl.when(pl.program_id(1) == 0)
    def _():
        st_ref[...] = jnp.zeros_like(st_ref)

    n_ch = k_ref.shape[0] // CHUNK
    for ch in reversed(range(n_ch)):
        rows = slice(CHUNK * ch, CHUNK * (ch + 1))
        for c in range(N_PAIRS):
            st = st_ref[c]
            pb_ref[ch, c] = (st[:PB_ROWS] + st[PB_ROWS:]).astype(BF16)
            k = k_ref[rows, LANES * c:LANES * (c + 1)].astype(F32)
            vpair = _pair_cols((v0_ref, v1_ref), rows, c, 2 * RET_VAL_DIM)
            _state_update(st_ref, c, k, kdb_ref[c], gb_ref[c], vpair)


def _retention_states(proj, lg_b, batch, seq, tt=2048):
    nt = seq // tt
    cpt = tt // CHUNK
    rrow = lambda b, i: b * nt + (nt - 1 - i)
    col = lambda off: (lambda b, i: (rrow(b, i), off // CB))
    return pl.pallas_call(
        _retstate_kernel,
        grid=(batch, nt),
        in_specs=[
            pl.BlockSpec(memory_space=pltpu.SMEM),
            pl.BlockSpec((tt, CB), col(OFF_KR)),
            pl.BlockSpec((tt, CB), col(OFF_VR)),
            pl.BlockSpec((tt, CB), col(OFF_VR + CB)),
        ],
        out_specs=pl.BlockSpec((cpt, N_PAIRS, PB_ROWS, 2 * RET_VAL_DIM),
                               lambda b, i: (rrow(b, i), 0, 0, 0)),
        out_shape=jax.ShapeDtypeStruct((batch * nt * cpt, N_PAIRS, PB_ROWS, 2 * RET_VAL_DIM), BF16),
        scratch_shapes=[pltpu.VMEM((N_PAIRS, LANES, 2 * RET_VAL_DIM), F32),
                        pltpu.VMEM((N_PAIRS, CHUNK, LANES), F32),
                        pltpu.VMEM((N_PAIRS, LANES, 2 * RET_VAL_DIM), F32)],
        compiler_params=pltpu.CompilerParams(
            dimension_semantics=("arbitrary", "arbitrary"),
            vmem_limit_bytes=VMEM_LIMIT),
        name="ret_states",
    )(lg_b, proj, proj, proj)


GROUP_ROWS = 2 * BLOCK
OUT_COLS = 512


def _attention_stage(left_ok, right_ok, q_ref, kvp_ref, kvc_ref, kvn_ref, bias_ref, cap_ref,
                     sink_ref, a_scr, row0):
    n_blk = q_ref.shape[0] // BLOCK
    hd = ATTN_HEAD_DIM
    kv = jnp.concatenate([kvp_ref[...], kvc_ref[...], kvn_ref[...]], axis=0)
    lane = lax.broadcasted_iota(jnp.int32, (kv.shape[0], LANES), 1)
    low = lane < hd

    tile4 = lambda c: jnp.concatenate([c] * ATTN_GROUP, axis=1)
    cap_l, cap_r = tile4(cap_ref[:BLOCK]), tile4(cap_ref[2 * BLOCK:])
    cap_first = jnp.where(left_ok, cap_l, NEG)
    cap_last = jnp.where(right_ok, cap_r, NEG)

    k_side, v_ones = {}, {}
    for j in range(ATTN_KV_COLS // LANES):
        kj = kv[:, LANES * j:LANES * (j + 1)]
        vj = kv[:, ATTN_KV_COLS + LANES * j:ATTN_KV_COLS + LANES * (j + 1)]
        for hg in range(2):
            mine = low if hg == 0 else jnp.logical_not(low)
            k_here = jnp.where(mine, kj, jnp.zeros_like(kj))
            k_side[(2 * j + hg, hg)] = k_here
            k_side[(2 * j + hg, 1 - hg)] = _roll_half(k_here)
            v_ones[2 * j + hg] = jnp.where(mine, vj, jnp.ones_like(vj))

    def scores(item):
        blk, g = item
        win = slice(BLOCK * blk, BLOCK * blk + KW)
        qs = jnp.concatenate(
            [q_ref[BLOCK * blk:BLOCK * (blk + 1), LANES * c:LANES * (c + 1)]
             for c in (2 * g, 2 * g + 1)], axis=0)
        return [lax.dot_general(k_side[(g, half)][win], qs, (((1,), (1,)), ((), ())),
                                preferred_element_type=F32) for half in range(2)]

    def softmax(item, st_pair):
        blk, g = item
        c_l = cap_first if blk == 0 else cap_l
        c_r = cap_last if blk == n_blk - 1 else cap_r
        st = jnp.concatenate(st_pair, axis=1)
        b = jnp.concatenate([bias_ref[ATTN_GROUP * g + t] for t in range(ATTN_GROUP)], axis=1)
        t = jnp.concatenate([
            jnp.minimum(st[:BLOCK] + b[:BLOCK], c_l),
            st[BLOCK:2 * BLOCK] + b[BLOCK:2 * BLOCK],
            jnp.minimum(st[2 * BLOCK:] + b[2 * BLOCK:], c_r),
        ], axis=0)
        sk = sink_ref[g]
        m = jnp.maximum(jnp.max(t, axis=0, keepdims=True), sk)
        return jnp.exp2(t - m).astype(BF16), jnp.exp2(sk - m)

    def finish(item, probs):
        blk, g = item
        p, e_sink = probs
        win = slice(BLOCK * blk, BLOCK * blk + KW)
        ot = lax.dot_general(v_ones[g][win], p, (((0,), (0,)), ((), ())),
                             preferred_element_type=F32)
        hg = g % 2
        num = ot[hd * hg:hd * (hg + 1)]
        den = ot[hd * (1 - hg):hd * (1 - hg) + 1]
        out = num * (1.0 / (den + e_sink))
        out_t = jnp.concatenate([out[:, :2 * BLOCK], out[:, 2 * BLOCK:]], axis=0)
        for r in range(2):
            c = 2 * g + r
            a_scr[row0 + BLOCK * blk:row0 + BLOCK * (blk + 1), LANES * c:LANES * (c + 1)] = (
                out_t[:, BLOCK * r:BLOCK * (r + 1)].T.astype(a_scr.dtype))

    return scores, softmax, finish


def _retention_stage(q_ref, k_ref, v_refs, g_refs, pb_ref, gnw_ref, st_ref, d_ref, qdf_ref,
                     qdb_ref, kdf_ref, gf_ref, r_scr, row0, alive=None):
    lane = lax.broadcasted_iota(jnp.int32, (CHUNK, LANES), 1)
    low_half = lane < RET_KEY_DIM
    first_v = lax.broadcasted_iota(jnp.int32, (CHUNK, 2 * RET_VAL_DIM), 1) < RET_VAL_DIM
    first_s = lax.broadcasted_iota(jnp.int32, (PB_ROWS, 2 * RET_VAL_DIM), 1) < RET_VAL_DIM

    def scores(item):
        ch, c = item
        rows = slice(CHUNK * ch, CHUNK * (ch + 1))
        kb = k_ref[rows, LANES * c:LANES * (c + 1)]
        q = q_ref[rows, LANES * c:LANES * (c + 1)]
        out = []
        for half in range(2):
            sel = low_half if half == 0 else jnp.logical_not(low_half)
            qh = jnp.where(sel, q, jnp.zeros_like(q))
            out.append(lax.dot_general(qh, kb, (((1,), (1,)), ((), ())),
                                       preferred_element_type=F32))
        return out

    def decay(item, s_pair):
        ch, c = item
        rows = slice(CHUNK * ch, CHUNK * (ch + 1))
        q = q_ref[rows, LANES * c:LANES * (c + 1)].astype(F32)
        parts = [(s_pair[half] * d_ref[2 * c + half]).astype(BF16) for half in range(2)]
        parts.append((q * qdf_ref[c]).astype(BF16))
        parts.append((q * qdb_ref[c]).astype(BF16))
        return jnp.concatenate(parts, axis=1)

    def finish(item, lhs):
        ch, c = item
        rows = slice(CHUNK * ch, CHUNK * (ch + 1))
        k = k_ref[rows, LANES * c:LANES * (c + 1)].astype(F32)
        vpair = _pair_cols(v_refs, rows, c, 2 * RET_VAL_DIM)
        zero_v = jnp.zeros_like(vpair)
        pb = pb_ref[ch, c]
        zero_s = jnp.zeros_like(pb)
        rhs = jnp.concatenate([
            jnp.where(first_v, vpair, zero_v),
            jnp.where(first_v, zero_v, vpair),
            st_ref[c].astype(BF16),
            jnp.where(first_s, pb, zero_s),
            jnp.where(first_s, zero_s, pb),
        ], axis=0)
        o = jnp.dot(lhs, rhs, preferred_element_type=F32)
        _state_update(st_ref, c, k, kdf_ref[c], gf_ref[c], vpair, alive)
        for half in range(2):
            head = 2 * c + half
            lo_c = RET_VAL_DIM * head
            oh = o[:, RET_VAL_DIM * half:RET_VAL_DIM * (half + 1)]
            mu = jnp.mean(oh, axis=-1, keepdims=True)
            dev = oh - mu
            var = jnp.mean(dev * dev, axis=-1, keepdims=True)
            y = dev * lax.rsqrt(var + EPS) * gnw_ref[:, lo_c:lo_c + RET_VAL_DIM]
            g = _pair_cols(g_refs, rows, head, RET_VAL_DIM).astype(F32)
            r_scr[row0 + CHUNK * ch:row0 + CHUNK * (ch + 1), lo_c:lo_c + RET_VAL_DIM] = (
                (y * g).astype(r_scr.dtype))

    return scores, decay, finish


def _mixer_kernel(*refs, nt):
    (lgf_ref, lgb_ref), rest = refs[:2], refs[2:]
    per_phase = 11
    phase_refs = [rest[:per_phase], rest[per_phase:2 * per_phase]]
    (bias_ref, cap_ref, sink_ref, gnw_ref, ga0_ref, ga1_ref, x_ref, anw_ref, w_ref, fw_ref,
     o_ref, a_scr, r_scr, y_scr, st_ref, d_ref, qdf_ref, qdb_ref, kdf_ref,
     gf_ref) = rest[2 * per_phase:]
    s = pl.program_id(1)

    @pl.when(_first_step())
    def _():
        a_scr[...] = jnp.zeros_like(a_scr)
        r_scr[...] = jnp.zeros_like(r_scr)
        idx = _row_index()
        col = lax.broadcasted_iota(jnp.int32, (CHUNK, CHUNK), 1).astype(F32)
        diff = idx - col
        for c in range(N_PAIRS):
            qdf_ref[c] = _lane_decay(lgf_ref, c, idx + 1.0)
            qdb_ref[c] = _lane_decay(lgb_ref, c, float(CHUNK) - idx)
            kdf_ref[c] = _lane_decay(lgf_ref, c, float(CHUNK - 1) - idx)
            gf_ref[c] = _chunk_decay(lgf_ref, c)
        for hh in range(RET_HEADS):
            d_ref[hh] = jnp.exp(jnp.where(diff >= 0.0, lgf_ref[hh] * diff, lgb_ref[hh] * (-diff)))

    @pl.when(s == 0)
    def _():
        st_ref[...] = jnp.zeros_like(st_ref)

    alive_a = jnp.where(s > 0, 1.0, 0.0).astype(F32)
    stages = []
    for ph, row0 in ((0, GROUP_ROWS), (1, 0)):
        qa, kvp, kvc, kvn, qr, kr, v0, v1, gr0, gr1, pb = phase_refs[ph]
        if ph == 0:
            left_ok, right_ok, alive = True, s < nt, alive_a
        else:
            left_ok, right_ok, alive = s > 0, True, None
        stages.append({
            "a": _attention_stage(left_ok, right_ok, qa, kvp, kvc, kvn, bias_ref, cap_ref,
                                  sink_ref, a_scr, row0),
            "r": _retention_stage(qr, kr, (v0, v1), (gr0, gr1), pb, gnw_ref, st_ref, d_ref,
                                  qdf_ref, qdb_ref, kdf_ref, gf_ref, r_scr, row0, alive),
        })

    n_out = D_MODEL // OUT_COLS
    gated = {}

    def out_chunk(grp, n):
        rows = slice(GROUP_ROWS * grp, GROUP_ROWS * (grp + 1))
        cols = slice(OUT_COLS * n, OUT_COLS * (n + 1))
        if grp not in gated:
            a = a_scr[rows, :].astype(F32)
            ms = jnp.mean(a * a, axis=-1, keepdims=True)
            g = jnp.concatenate([ga0_ref[rows, :], ga1_ref[rows, :]], axis=1).astype(F32)
            gated[grp] = (a * lax.rsqrt(ms + EPS) * anw_ref[...] * g).astype(BF16)
        y = (x_ref[rows, cols]
             + jnp.dot(r_scr[rows, :], w_ref[ATTN_WIDTH:, cols], preferred_element_type=F32)
             + jnp.dot(gated[grp], w_ref[:ATTN_WIDTH, cols], preferred_element_type=F32))
        y_scr[rows, cols] = y
        if n == n_out - 1:
            yy = y_scr[rows, :]
            ms = jnp.mean(yy * yy, axis=-1, keepdims=True)
            o_ref[rows, :] = yy * lax.rsqrt(ms + EPS) * fw_ref[...]

    items = []
    for ph in range(2):
        for blk in range(GROUP_ROWS // BLOCK):
            for u in range(ATTN_KV_HEADS):
                items.append((ph, "a", (blk, u)))
                items.append((ph, "r", (blk, u)))
    per_phase_items = len(items) // 2
    every = per_phase_items // n_out
    pending = stages[items[0][0]][items[0][1]][0](items[0][2])
    closing = None
    for idx, (ph, kind, item) in enumerate(items):
        cur = pending
        if idx + 1 < len(items):
            nph, nkind, nitem = items[idx + 1]
            pending = stages[nph][nkind][0](nitem)
        mid = stages[ph][kind][1](item, cur)
        if closing is not None:
            stages[closing[0]][closing[1]][2](closing[2], closing[3])
        closing = (ph, kind, item, mid)
        pos = idx % per_phase_items
        if pos % every == every - 1:
            out_chunk(ph, pos // every)
    stages[closing[0]][closing[1]][2](closing[2], closing[3])


def _mixer(proj, pb, x2, bias_t, cap_t, sink_t, lg_f, lg_b, gn_w, attn_nw, w_o_p, final_w,
           batch, seq, tq=512):
    assert tq == 2 * GROUP_ROWS
    nt = seq // tq
    nh = seq // GROUP_ROWS
    nb = seq // BLOCK
    bpg = GROUP_ROWS // BLOCK
    cpg = GROUP_ROWS // CHUNK
    out_t = lambda s: jnp.maximum(s - 1, 0)
    grp_of = (lambda s: jnp.clip(2 * s - 1, 0, nh - 1),
              lambda s: jnp.minimum(2 * s, nh - 1))
    const2 = lambda b, s: (0, 0)
    const3 = lambda b, s: (0, 0, 0)
    smem = pl.BlockSpec(memory_space=pltpu.SMEM)

    def phase_specs(g):
        col = lambda off: (lambda b, s: (b * nh + g(s), off // CB))
        prev_blk = lambda b, s: (b * nb + jnp.maximum(bpg * g(s) - 1, 0), OFF_KA // CB)
        next_blk = lambda b, s: (b * nb + jnp.minimum(bpg * (g(s) + 1), nb - 1), OFF_KA // CB)
        return [
            pl.BlockSpec((GROUP_ROWS, ATTN_WIDTH), lambda b, s: (b * nh + g(s), OFF_QA // ATTN_WIDTH)),
            pl.BlockSpec((BLOCK, CB), prev_blk),
            pl.BlockSpec((GROUP_ROWS, CB), col(OFF_KA)),
            pl.BlockSpec((BLOCK, CB), next_blk),
            pl.BlockSpec((GROUP_ROWS, CB), col(OFF_QR)),
            pl.BlockSpec((GROUP_ROWS, CB), col(OFF_KR)),
            pl.BlockSpec((GROUP_ROWS, CB), col(OFF_VR)),
            pl.BlockSpec((GROUP_ROWS, CB), col(OFF_VR + CB)),
            pl.BlockSpec((GROUP_ROWS, CB), col(OFF_GR)),
            pl.BlockSpec((GROUP_ROWS, CB), col(OFF_GR + CB)),
            pl.BlockSpec((cpg, N_PAIRS, PB_ROWS, 2 * RET_VAL_DIM),
                         lambda b, s: (b * nh + g(s), 0, 0, 0)),
        ]

    ocol = lambda off: (lambda b, s: (b * nt + out_t(s), off // CB))
    state_shape = (N_PAIRS, LANES, 2 * RET_VAL_DIM)
    lane_tab = (N_PAIRS, CHUNK, LANES)
    phase_args = [proj] * 10 + [pb]
    return pl.pallas_call(
        functools.partial(_mixer_kernel, nt=nt),
        grid=(batch, nt + 1),
        in_specs=[smem, smem] + phase_specs(grp_of[0]) + phase_specs(grp_of[1]) + [
            pl.BlockSpec((ATTN_HEADS, KW, BLOCK), const3),
            pl.BlockSpec((KW, BLOCK), const2),
            pl.BlockSpec((ATTN_KV_HEADS, 1, ATTN_GROUP * BLOCK), const3),
            pl.BlockSpec((1, RET_WIDTH), const2),
            pl.BlockSpec((tq, CB), ocol(OFF_GA)),
            pl.BlockSpec((tq, CB), ocol(OFF_GA + CB)),
            pl.BlockSpec((tq, D_MODEL), lambda b, s: (b * nt + out_t(s), 0)),
            pl.BlockSpec((1, ATTN_WIDTH), const2),
            pl.BlockSpec((ATTN_WIDTH + RET_WIDTH, D_MODEL), const2),
            pl.BlockSpec((1, D_MODEL), const2),
        ],
        out_specs=pl.BlockSpec((tq, D_MODEL), lambda b, s: (b * nt + out_t(s), 0)),
        out_shape=jax.ShapeDtypeStruct((batch * seq, D_MODEL), F32),
        scratch_shapes=[pltpu.VMEM((tq, ATTN_WIDTH), BF16),
                        pltpu.VMEM((tq, RET_WIDTH), BF16),
                        pltpu.VMEM((tq, D_MODEL), F32),
                        pltpu.VMEM(state_shape, F32),
                        pltpu.VMEM((RET_HEADS, CHUNK, CHUNK), F32),
                        pltpu.VMEM(lane_tab, F32),
                        pltpu.VMEM(lane_tab, F32),
                        pltpu.VMEM(lane_tab, F32),
                        pltpu.VMEM(state_shape, F32)],
        compiler_params=pltpu.CompilerParams(
            dimension_semantics=("arbitrary", "arbitrary"),
            vmem_limit_bytes=VMEM_LIMIT),
        name="mixer",
    )(lg_f, lg_b, *phase_args, *phase_args, bias_t, cap_t, sink_t, gn_w, proj, proj, x2, attn_nw,
      w_o_p, final_w)


def _t5_bucket(rel):
    nb = N_BUCKETS // 2
    max_exact = nb // 2
    ret = jnp.where(rel > 0, nb, 0)
    n = jnp.abs(rel)
    nf = jnp.maximum(n, 1).astype(F32)
    large = max_exact + (jnp.log(nf / max_exact) / math.log(MAX_DISTANCE / max_exact)
                         * (nb - max_exact)).astype(jnp.int32)
    large = jnp.minimum(large, nb - 1)
    return ret + jnp.where(n < max_exact, n, large)


def kernel(x, norm_w, w_in, attn_sink, rel_bias, attn_out_norm_w, ret_decay_fwd,
           ret_decay_bwd, ret_gn_w, w_out, final_norm_w):
    batch, seq, _ = x.shape
    assert norm_w.shape[0] == 1 and seq % BLOCK == 0
    x2 = x.reshape(batch * seq, D_MODEL)

    col_scale = np.ones((1, IN_WIDTH), np.float32)
    col_scale[:, OFF_QA:OFF_QA + ATTN_WIDTH] = ATTN_HEAD_DIM ** -0.5 * LOG2E
    col_scale[:, OFF_KR:OFF_KR + RET_QK_COLS] = RET_KEY_DIM ** -0.5
    w_in_p = (w_in[0] * col_scale).astype(BF16)
    w_o_p = w_out[0].astype(BF16)

    qi = jnp.arange(BLOCK, dtype=jnp.int32)[None, :]
    kt = jnp.arange(KW, dtype=jnp.int32)[:, None]
    rel = kt - BLOCK - qi
    bucket = _t5_bucket(rel) & (N_BUCKETS - 1)
    onehot = (bucket[None] == jnp.arange(N_BUCKETS)[:, None, None]).astype(F32)
    rb = rel_bias.astype(F32).reshape(N_BUCKETS, ATTN_KV_HEADS, 2, 2)
    rb = jnp.swapaxes(rb, 2, 3).reshape(N_BUCKETS, ATTN_HEADS)
    bias_t = jnp.einsum("nh,nkq->hkq", rb, onehot,
                        precision=lax.Precision.HIGHEST)
    bias_t = bias_t * LOG2E
    cap_t = jnp.where(jnp.abs(rel) <= WINDOW, BIG, NEG).astype(F32)
    sink = (attn_sink[0].astype(F32) * LOG2E).reshape(ATTN_KV_HEADS, 1, 2, 2)
    sink = jnp.swapaxes(sink, 2, 3)[..., None]
    sink_t = jnp.broadcast_to(sink, (ATTN_KV_HEADS, 1, 2, 2, BLOCK)).reshape(
        ATTN_KV_HEADS, 1, ATTN_GROUP * BLOCK)

    inv = ROPE_BASE ** (-jnp.arange(0, RET_KEY_DIM, 2, dtype=F32) / RET_KEY_DIM)
    inv_full = jnp.tile(jnp.repeat(inv, 2), LANES // RET_KEY_DIM)[None, :]
    sgn = np.tile(np.array([-1.0, 1.0], np.float32), LANES // 2)[None, :]

    def trig_table(pos):
        ang = pos[:, None] * inv_full
        return jnp.stack([jnp.cos(ang), jnp.sin(ang), jnp.sin(ang) * sgn])

    rowtab = trig_table(jnp.arange(2 * BLOCK, dtype=F32))
    basetab = jnp.swapaxes(trig_table(jnp.arange(0, seq, BLOCK, dtype=F32)), 0, 1)

    lg_f = jax.nn.log_sigmoid(ret_decay_fwd[0].astype(F32))
    lg_b = jax.nn.log_sigmoid(ret_decay_bwd[0].astype(F32))

    proj = _inproj(x2, norm_w[0].reshape(1, D_MODEL), w_in_p, rowtab, basetab, seq)
    pb = _retention_states(proj, lg_b, batch, seq)
    out = _mixer(proj, pb, x2, bias_t, cap_t, sink_t, lg_f, lg_b,
                 ret_gn_w[0].reshape(1, RET_WIDTH), attn_out_norm_w[0].reshape(1, ATTN_WIDTH),
                 w_o_p, final_norm_w.reshape(1, D_MODEL), batch, seq)
    return out.reshape(batch, seq, D_MODEL)
```

```python
import functools
import math

import numpy as np
import jax
import jax.numpy as jnp
from jax import lax
from jax.experimental import pallas as pl
from jax.experimental.pallas import tpu as pltpu

D_MODEL = 2048
ATTN_HEAD_DIM = 64
ATTN_WIDTH = 1024
ATTN_HEADS = 16
ATTN_KV_HEADS = 4
ATTN_GROUP = 4
ATTN_KV_COLS = 256
WINDOW = 128
BLOCK = 128
N_BUCKETS = 32
MAX_DISTANCE = 128
RET_WIDTH = 1024
RET_HEADS = 8
RET_VAL_DIM = 128
RET_KEY_DIM = 64
RET_QK_COLS = 512
CHUNK = 128
ROPE_BASE = 10000.0
EPS = 1e-6
NEG = -1e30
BIG = 3e38
LOG2E = math.log2(math.e)

LANES = 128
N_PAIRS = RET_HEADS // 2
KW = 3 * BLOCK
CB = 512

OFF_QA = 0
OFF_KA = OFF_QA + ATTN_WIDTH
OFF_GA = OFF_KA + 2 * ATTN_KV_COLS
OFF_QR = OFF_GA + ATTN_WIDTH
OFF_KR = OFF_QR + RET_QK_COLS
OFF_VR = OFF_KR + RET_QK_COLS
OFF_GR = OFF_VR + RET_WIDTH
IN_WIDTH = OFF_GR + RET_WIDTH

VMEM_LIMIT = 56 * 1024 * 1024

F32 = jnp.float32
BF16 = jnp.bfloat16


def _roll_half(x):
    return pltpu.roll(x.astype(F32), LANES // 2, axis=1).astype(x.dtype)


PB_ROWS = RET_KEY_DIM


def _first_step():
    return jnp.logical_and(pl.program_id(0) == 0, pl.program_id(1) == 0)


def _lane_decay(lg_ref, c, expo):
    lane = lax.broadcasted_iota(jnp.int32, (CHUNK, LANES), 1)
    lg = jnp.where(lane < RET_KEY_DIM, lg_ref[2 * c], lg_ref[2 * c + 1])
    return jnp.exp(lg * expo)


def _state_mask():
    r = lax.broadcasted_iota(jnp.int32, (LANES, 2 * RET_VAL_DIM), 0)
    m = lax.broadcasted_iota(jnp.int32, (LANES, 2 * RET_VAL_DIM), 1)
    return (r // RET_KEY_DIM) == (m // RET_VAL_DIM)


def _chunk_decay(lg_ref, c):
    r = lax.broadcasted_iota(jnp.int32, (LANES, 2 * RET_VAL_DIM), 0)
    lg = jnp.where(r < RET_KEY_DIM, lg_ref[2 * c], lg_ref[2 * c + 1])
    return jnp.where(_state_mask(), jnp.exp(lg * float(CHUNK)), 0.0)


def _row_index():
    return lax.broadcasted_iota(jnp.int32, (CHUNK, LANES), 0).astype(F32)


def _state_update(st_ref, c, k_rot, kdec, gdec, vpair):
    kd = (k_rot * kdec).astype(BF16)
    upd = lax.dot_general(kd, vpair, (((0,), (0,)), ((), ())),
                          preferred_element_type=F32)
    st_ref[c] = st_ref[c] * gdec + jnp.where(_state_mask(), upd, 0.0)


def _pair_cols(refs, rows, c, width):
    per_ref = CB // width
    return refs[c // per_ref][rows, width * (c % per_ref):width * (c % per_ref + 1)]


def _silu(g):
    return g / (1.0 + jnp.exp(-g))


def _inproj_kernel(lgb_ref, x_ref, nw_ref, w_ref, rowtab_ref, basetab_ref, o_ref, pb_ref,
                   st_ref, kdb_ref, gb_ref, *, row_splits, tiles_per_seq):
    tm = x_ref.shape[0]
    nw = nw_ref[...]
    step = pl.program_id(0)
    tile_blk = (tiles_per_seq - 1 - step % tiles_per_seq) * (tm // BLOCK)

    @pl.when(step == 0)
    def _():
        idx = _row_index()
        for c in range(N_PAIRS):
            kdb_ref[c] = _lane_decay(lgb_ref, c, idx)
            gb_ref[c] = _chunk_decay(lgb_ref, c)

    @pl.when(step % tiles_per_seq == 0)
    def _():
        st_ref[...] = jnp.zeros_like(st_ref)

    start = 0
    for rc in row_splits:
        rows = slice(start, start + rc)
        x = x_ref[rows, :]
        ms = jnp.mean(x * x, axis=-1, keepdims=True)
        h = (x * lax.rsqrt(ms + EPS) * nw).astype(BF16)
        ca, sa, sas = rowtab_ref[0, :rc, :], rowtab_ref[1, :rc, :], rowtab_ref[2, :rc, :]
        base = basetab_ref[tile_blk + start // BLOCK]
        cb, sb, sbs = base[0:1, :], base[1:2, :], base[2:3, :]
        cos = ca * cb - sa * sb
        sin_signed = sas * cb + ca * sbs
        lane = lax.broadcasted_iota(jnp.int32, (rc, LANES), 1)
        even = (lane % 2) == 0
        start += rc
        for col in range(0, IN_WIDTH, CB):
            res = jnp.dot(h, w_ref[:, col:col + CB], preferred_element_type=F32)
            if OFF_GA <= col < OFF_QR or col >= OFF_GR:
                res = _silu(res)
            elif OFF_QR <= col < OFF_VR:
                parts = []
                for c in range(CB // LANES):
                    v = res[:, LANES * c:LANES * (c + 1)]
                    partner = jnp.where(even, pltpu.roll(v, LANES - 1, axis=1),
                                        pltpu.roll(v, 1, axis=1))
                    parts.append(v * cos + partner * sin_signed)
                res = jnp.concatenate(parts, axis=1)
            o_ref[rows, col:col + CB] = res.astype(o_ref.dtype)

    for ch in reversed(range(tm // CHUNK)):
        rows = slice(CHUNK * ch, CHUNK * (ch + 1))
        for c in range(N_PAIRS):
            st = st_ref[c]
            pb_ref[ch, c] = (st[:PB_ROWS] + st[PB_ROWS:]).astype(BF16)
            k = o_ref[rows, OFF_KR + LANES * c:OFF_KR + LANES * (c + 1)].astype(F32)
            vpair = o_ref[rows, OFF_VR + 2 * RET_VAL_DIM * c:OFF_VR + 2 * RET_VAL_DIM * (c + 1)]
            _state_update(st_ref, c, k, kdb_ref[c], gb_ref[c], vpair)


def _inproj(x2, norm_w, w_in_p, rowtab, basetab, lg_b, seq, tm=512, row_splits=(256, 256)):
    m = x2.shape[0]
    assert sum(row_splits) == tm and max(row_splits) <= rowtab.shape[1]
    tps = seq // tm
    cpt = tm // CHUNK
    tile = lambda i: (i // tps) * tps + (tps - 1 - i % tps)
    return pl.pallas_call(
        functools.partial(_inproj_kernel, row_splits=row_splits, tiles_per_seq=tps),
        grid=(m // tm,),
        in_specs=[
            pl.BlockSpec(memory_space=pltpu.SMEM),
            pl.BlockSpec((tm, D_MODEL), lambda i: (tile(i), 0)),
            pl.BlockSpec((1, D_MODEL), lambda i: (0, 0)),
            pl.BlockSpec((D_MODEL, IN_WIDTH), lambda i: (0, 0)),
            pl.BlockSpec(rowtab.shape, lambda i: (0, 0, 0)),
            pl.BlockSpec(basetab.shape, lambda i: (0, 0, 0)),
        ],
        out_specs=[
            pl.BlockSpec((tm, IN_WIDTH), lambda i: (tile(i), 0)),
            pl.BlockSpec((cpt, N_PAIRS, PB_ROWS, 2 * RET_VAL_DIM), lambda i: (tile(i), 0, 0, 0)),
        ],
        out_shape=[
            jax.ShapeDtypeStruct((m, IN_WIDTH), BF16),
            jax.ShapeDtypeStruct((m // CHUNK, N_PAIRS, PB_ROWS, 2 * RET_VAL_DIM), BF16),
        ],
        scratch_shapes=[pltpu.VMEM((N_PAIRS, LANES, 2 * RET_VAL_DIM), F32),
                        pltpu.VMEM((N_PAIRS, CHUNK, LANES), F32),
                        pltpu.VMEM((N_PAIRS, LANES, 2 * RET_VAL_DIM), F32)],
        compiler_params=pltpu.CompilerParams(
            dimension_semantics=("arbitrary",),
            vmem_limit_bytes=VMEM_LIMIT),
        name="inproj",
    )(lg_b, x2, norm_w, w_in_p, rowtab, basetab)


GROUP_ROWS = 2 * BLOCK
OUT_COLS = 512


def _attention_stage(i, nt, q_ref, kvp_ref, kvc_ref, kvn_ref, bias_ref, cap_ref, sink_ref, a_scr):
    n_blk = q_ref.shape[0] // BLOCK
    hd = ATTN_HEAD_DIM
    kv = jnp.concatenate([kvp_ref[...], kvc_ref[...], kvn_ref[...]], axis=0)
    lane = lax.broadcasted_iota(jnp.int32, (kv.shape[0], LANES), 1)
    low = lane < hd

    tile4 = lambda c: jnp.concatenate([c] * ATTN_GROUP, axis=1)
    cap_l, cap_r = tile4(cap_ref[:BLOCK]), tile4(cap_ref[2 * BLOCK:])
    cap_first = jnp.where(i > 0, cap_l, NEG)
    cap_last = jnp.where(i < nt - 1, cap_r, NEG)

    k_side, v_ones = {}, {}
    for j in range(ATTN_KV_COLS // LANES):
        kj = kv[:, LANES * j:LANES * (j + 1)]
        vj = kv[:, ATTN_KV_COLS + LANES * j:ATTN_KV_COLS + LANES * (j + 1)]
        for hg in range(2):
            mine = low if hg == 0 else jnp.logical_not(low)
            k_here = jnp.where(mine, kj, jnp.zeros_like(kj))
            k_side[(2 * j + hg, hg)] = k_here
            k_side[(2 * j + hg, 1 - hg)] = _roll_half(k_here)
            v_ones[2 * j + hg] = jnp.where(mine, vj, jnp.ones_like(vj))

    def scores(item):
        blk, g = item
        win = slice(BLOCK * blk, BLOCK * blk + KW)
        qs = jnp.concatenate(
            [q_ref[BLOCK * blk:BLOCK * (blk + 1), LANES * c:LANES * (c + 1)]
             for c in (2 * g, 2 * g + 1)], axis=0)
        return [lax.dot_general(k_side[(g, half)][win], qs, (((1,), (1,)), ((), ())),
                                preferred_element_type=F32) for half in range(2)]

    def softmax(item, st_pair):
        blk, g = item
        c_l = cap_first if blk == 0 else cap_l
        c_r = cap_last if blk == n_blk - 1 else cap_r
        st = jnp.concatenate(st_pair, axis=1)
        b = jnp.concatenate([bias_ref[ATTN_GROUP * g + t] for t in range(ATTN_GROUP)], axis=1)
        t = jnp.concatenate([
            jnp.minimum(st[:BLOCK] + b[:BLOCK], c_l),
            st[BLOCK:2 * BLOCK] + b[BLOCK:2 * BLOCK],
            jnp.minimum(st[2 * BLOCK:] + b[2 * BLOCK:], c_r),
        ], axis=0)
        sk = sink_ref[g]
        m = jnp.maximum(jnp.max(t, axis=0, keepdims=True), sk)
        return jnp.exp2(t - m).astype(BF16), jnp.exp2(sk - m)

    def finish(item, probs):
        blk, g = item
        p, e_sink = probs
        win = slice(BLOCK * blk, BLOCK * blk + KW)
        ot = lax.dot_general(v_ones[g][win], p, (((0,), (0,)), ((), ())),
                             preferred_element_type=F32)
        hg = g % 2
        num = ot[hd * hg:hd * (hg + 1)]
        den = ot[hd * (1 - hg):hd * (1 - hg) + 1]
        out = num * (1.0 / (den + e_sink))
        out_t = jnp.concatenate([out[:, :2 * BLOCK], out[:, 2 * BLOCK:]], axis=0)
        for r in range(2):
            c = 2 * g + r
            a_scr[BLOCK * blk:BLOCK * (blk + 1), LANES * c:LANES * (c + 1)] = (
                out_t[:, BLOCK * r:BLOCK * (r + 1)].T.astype(a_scr.dtype))

    return scores, softmax, finish


def _retention_stage(q_ref, k_ref, v_refs, g_refs, pb_ref, gnw_ref, st_ref, d_ref, qdf_ref,
                     qdb_ref, kdf_ref, gf_ref, r_scr):
    lane = lax.broadcasted_iota(jnp.int32, (CHUNK, LANES), 1)
    low_half = lane < RET_KEY_DIM
    first_v = lax.broadcasted_iota(jnp.int32, (CHUNK, 2 * RET_VAL_DIM), 1) < RET_VAL_DIM
    first_s = lax.broadcasted_iota(jnp.int32, (PB_ROWS, 2 * RET_VAL_DIM), 1) < RET_VAL_DIM

    def scores(item):
        ch, c = item
        rows = slice(CHUNK * ch, CHUNK * (ch + 1))
        kb = k_ref[rows, LANES * c:LANES * (c + 1)]
        q = q_ref[rows, LANES * c:LANES * (c + 1)]
        out = []
        for half in range(2):
            sel = low_half if half == 0 else jnp.logical_not(low_half)
            qh = jnp.where(sel, q, jnp.zeros_like(q))
            out.append(lax.dot_general(qh, kb, (((1,), (1,)), ((), ())),
                                       preferred_element_type=F32))
        return out

    def decay(item, s_pair):
        ch, c = item
        rows = slice(CHUNK * ch, CHUNK * (ch + 1))
        q = q_ref[rows, LANES * c:LANES * (c + 1)].astype(F32)
        parts = [(s_pair[half] * d_ref[2 * c + half]).astype(BF16) for half in range(2)]
        parts.append((q * qdf_ref[c]).astype(BF16))
        parts.append((q * qdb_ref[c]).astype(BF16))
        return jnp.concatenate(parts, axis=1)

    def finish(item, lhs):
        ch, c = item
        rows = slice(CHUNK * ch, CHUNK * (ch + 1))
        k = k_ref[rows, LANES * c:LANES * (c + 1)].astype(F32)
        vpair = _pair_cols(v_refs, rows, c, 2 * RET_VAL_DIM)
        zero_v = jnp.zeros_like(vpair)
        pb = pb_ref[ch, c]
        zero_s = jnp.zeros_like(pb)
        rhs = jnp.concatenate([
            jnp.where(first_v, vpair, zero_v),
            jnp.where(first_v, zero_v, vpair),
            st_ref[c].astype(BF16),
            jnp.where(first_s, pb, zero_s),
            jnp.where(first_s, zero_s, pb),
        ], axis=0)
        o = jnp.dot(lhs, rhs, preferred_element_type=F32)
        _state_update(st_ref, c, k, kdf_ref[c], gf_ref[c], vpair)
        for half in range(2):
            head = 2 * c + half
            lo_c = RET_VAL_DIM * head
            oh = o[:, RET_VAL_DIM * half:RET_VAL_DIM * (half + 1)]
            mu = jnp.mean(oh, axis=-1, keepdims=True)
            dev = oh - mu
            var = jnp.mean(dev * dev, axis=-1, keepdims=True)
            y = dev * lax.rsqrt(var + EPS) * gnw_ref[:, lo_c:lo_c + RET_VAL_DIM]
            g = _pair_cols(g_refs, rows, head, RET_VAL_DIM).astype(F32)
            r_scr[rows, lo_c:lo_c + RET_VAL_DIM] = (y * g).astype(r_scr.dtype)

    return scores, decay, finish


def _mixer_kernel(lgf_ref, lgb_ref, qa_ref, kvp_ref, kvc_ref, kvn_ref, bias_ref, cap_ref, sink_ref,
                  ga0_ref, ga1_ref, qr_ref, kr_ref, v0_ref, v1_ref, gr0_ref, gr1_ref, pb_ref,
                  gnw_ref, x_ref, anw_ref, w_ref, fw_ref, o_ref,
                  a_scr, r_scr, y_scr, st_ref, d_ref, qdf_ref, qdb_ref, kdf_ref, gf_ref):
    @pl.when(_first_step())
    def _():
        idx = _row_index()
        col = lax.broadcasted_iota(jnp.int32, (CHUNK, CHUNK), 1).astype(F32)
        diff = idx - col
        for c in range(N_PAIRS):
            qdf_ref[c] = _lane_decay(lgf_ref, c, idx + 1.0)
            qdb_ref[c] = _lane_decay(lgb_ref, c, float(CHUNK) - idx)
            kdf_ref[c] = _lane_decay(lgf_ref, c, float(CHUNK - 1) - idx)
            gf_ref[c] = _chunk_decay(lgf_ref, c)
        for hh in range(RET_HEADS):
            d_ref[hh] = jnp.exp(jnp.where(diff >= 0.0, lgf_ref[hh] * diff, lgb_ref[hh] * (-diff)))

    @pl.when(pl.program_id(1) == 0)
    def _():
        st_ref[...] = jnp.zeros_like(st_ref)

    tq = qa_ref.shape[0]
    stage = {
        "a": _attention_stage(pl.program_id(1), pl.num_programs(1), qa_ref, kvp_ref, kvc_ref,
                              kvn_ref, bias_ref, cap_ref, sink_ref, a_scr),
        "r": _retention_stage(qr_ref, kr_ref, (v0_ref, v1_ref), (gr0_ref, gr1_ref), pb_ref,
                              gnw_ref, st_ref, d_ref, qdf_ref, qdb_ref, kdf_ref, gf_ref, r_scr),
    }

    n_grp = tq // GROUP_ROWS
    n_out = D_MODEL // OUT_COLS
    gated = {}

    def out_chunk(grp, n):
        rows = slice(GROUP_ROWS * grp, GROUP_ROWS * (grp + 1))
        cols = slice(OUT_COLS * n, OUT_COLS * (n + 1))
        if grp not in gated:
            a = a_scr[rows, :].astype(F32)
            ms = jnp.mean(a * a, axis=-1, keepdims=True)
            g = jnp.concatenate([ga0_ref[rows, :], ga1_ref[rows, :]], axis=1).astype(F32)
            gated[grp] = (a * lax.rsqrt(ms + EPS) * anw_ref[...] * g).astype(BF16)
        y = (x_ref[rows, cols]
             + jnp.dot(r_scr[rows, :], w_ref[ATTN_WIDTH:, cols], preferred_element_type=F32)
             + jnp.dot(gated[grp], w_ref[:ATTN_WIDTH, cols], preferred_element_type=F32))
        y_scr[rows, cols] = y
        if n == n_out - 1:
            yy = y_scr[rows, :]
            ms = jnp.mean(yy * yy, axis=-1, keepdims=True)
            o_ref[rows, :] = yy * lax.rsqrt(ms + EPS) * fw_ref[...]

    per = GROUP_ROWS // BLOCK
    items = []
    for grp in range(n_grp):
        for blk in range(per * grp, per * (grp + 1)):
            for u in range(ATTN_KV_HEADS):
                items.append(("a", (blk, u), grp))
                items.append(("r", (blk, u), grp))
    per_group = len(items) // n_grp
    every = per_group // n_out
    pending = stage[items[0][0]][0](items[0][1])
    closing = None
    for idx, (kind, item, grp) in enumerate(items):
        cur = pending
        if idx + 1 < len(items):
            nxt = items[idx + 1]
            pending = stage[nxt[0]][0](nxt[1])
        mid = stage[kind][1](item, cur)
        if closing is not None:
            stage[closing[0]][2](closing[1], closing[2])
        closing = (kind, item, mid)
        pos = idx % per_group
        if grp > 0 and pos % every == every - 1:
            out_chunk(grp - 1, pos // every)
    stage[closing[0]][2](closing[1], closing[2])
    for n in range(n_out):
        out_chunk(n_grp - 1, n)


def _mixer(proj, pb, x2, bias_t, cap_t, sink_t, lg_f, lg_b, gn_w, attn_nw, w_o_p, final_w,
           batch, seq, tq=512):
    nt = seq // tq
    bpt = tq // BLOCK
    nb = seq // BLOCK
    row = lambda b, i: b * nt + i
    col = lambda off: (lambda b, i: (row(b, i), off // CB))
    prev_blk = lambda b, i: (b * nb + jnp.maximum(bpt * i - 1, 0), OFF_KA // CB)
    next_blk = lambda b, i: (b * nb + jnp.minimum(bpt * (i + 1), nb - 1), OFF_KA // CB)
    const2 = lambda b, i: (0, 0)
    const3 = lambda b, i: (0, 0, 0)
    smem = pl.BlockSpec(memory_space=pltpu.SMEM)
    state_shape = (N_PAIRS, LANES, 2 * RET_VAL_DIM)
    lane_tab = (N_PAIRS, CHUNK, LANES)
    return pl.pallas_call(
        _mixer_kernel,
        grid=(batch, nt),
        in_specs=[
            smem, smem,
            pl.BlockSpec((tq, ATTN_WIDTH), lambda b, i: (row(b, i), OFF_QA // ATTN_WIDTH)),
            pl.BlockSpec((BLOCK, CB), prev_blk),
            pl.BlockSpec((tq, CB), col(OFF_KA)),
            pl.BlockSpec((BLOCK, CB), next_blk),
            pl.BlockSpec((ATTN_HEADS, KW, BLOCK), const3),
            pl.BlockSpec((KW, BLOCK), const2),
            pl.BlockSpec((ATTN_KV_HEADS, 1, ATTN_GROUP * BLOCK), const3),
            pl.BlockSpec((tq, CB), col(OFF_GA)),
            pl.BlockSpec((tq, CB), col(OFF_GA + CB)),
            pl.BlockSpec((tq, CB), col(OFF_QR)),
            pl.BlockSpec((tq, CB), col(OFF_KR)),
            pl.BlockSpec((tq, CB), col(OFF_VR)),
            pl.BlockSpec((tq, CB), col(OFF_VR + CB)),
            pl.BlockSpec((tq, CB), col(OFF_GR)),
            pl.BlockSpec((tq, CB), col(OFF_GR + CB)),
            pl.BlockSpec((bpt, N_PAIRS, PB_ROWS, 2 * RET_VAL_DIM), lambda b, i: (row(b, i), 0, 0, 0)),
            pl.BlockSpec((1, RET_WIDTH), const2),
            pl.BlockSpec((tq, D_MODEL), lambda b, i: (row(b, i), 0)),
            pl.BlockSpec((1, ATTN_WIDTH), const2),
            pl.BlockSpec((ATTN_WIDTH + RET_WIDTH, D_MODEL), const2),
            pl.BlockSpec((1, D_MODEL), const2),
        ],
        out_specs=pl.BlockSpec((tq, D_MODEL), lambda b, i: (row(b, i), 0)),
        out_shape=jax.ShapeDtypeStruct((batch * seq, D_MODEL), F32),
        scratch_shapes=[pltpu.VMEM((tq, ATTN_WIDTH), BF16),
                        pltpu.VMEM((tq, RET_WIDTH), BF16),
                        pltpu.VMEM((tq, D_MODEL), F32),
                        pltpu.VMEM(state_shape, F32),
                        pltpu.VMEM((RET_HEADS, CHUNK, CHUNK), F32),
                        pltpu.VMEM(lane_tab, F32),
                        pltpu.VMEM(lane_tab, F32),
                        pltpu.VMEM(lane_tab, F32),
                        pltpu.VMEM(state_shape, F32)],
        compiler_params=pltpu.CompilerParams(
            dimension_semantics=("arbitrary", "arbitrary"),
            vmem_limit_bytes=VMEM_LIMIT),
        name="mixer",
    )(lg_f, lg_b, proj, proj, proj, proj, bias_t, cap_t, sink_t, proj, proj, proj, proj, proj, proj,
      proj, proj, pb, gn_w, x2, attn_nw, w_o_p, final_w)


def _t5_bucket(rel):
    nb = N_BUCKETS // 2
    max_exact = nb // 2
    ret = jnp.where(rel > 0, nb, 0)
    n = jnp.abs(rel)
    nf = jnp.maximum(n, 1).astype(F32)
    large = max_exact + (jnp.log(nf / max_exact) / math.log(MAX_DISTANCE / max_exact)
                         * (nb - max_exact)).astype(jnp.int32)
    large = jnp.minimum(large, nb - 1)
    return ret + jnp.where(n < max_exact, n, large)


def kernel(x, norm_w, w_in, attn_sink, rel_bias, attn_out_norm_w, ret_decay_fwd,
           ret_decay_bwd, ret_gn_w, w_out, final_norm_w):
    batch, seq, _ = x.shape
    assert norm_w.shape[0] == 1 and seq % BLOCK == 0
    x2 = x.reshape(batch * seq, D_MODEL)

    col_scale = np.ones((1, IN_WIDTH), np.float32)
    col_scale[:, OFF_QA:OFF_QA + ATTN_WIDTH] = ATTN_HEAD_DIM ** -0.5 * LOG2E
    col_scale[:, OFF_KR:OFF_KR + RET_QK_COLS] = RET_KEY_DIM ** -0.5
    w_in_p = (w_in[0] * col_scale).astype(BF16)
    w_o_p = w_out[0].astype(BF16)

    qi = jnp.arange(BLOCK, dtype=jnp.int32)[None, :]
    kt = jnp.arange(KW, dtype=jnp.int32)[:, None]
    rel = kt - BLOCK - qi
    bucket = _t5_bucket(rel) & (N_BUCKETS - 1)
    onehot = (bucket[None] == jnp.arange(N_BUCKETS)[:, None, None]).astype(F32)
    rb = rel_bias.astype(F32).reshape(N_BUCKETS, ATTN_KV_HEADS, 2, 2)
    rb = jnp.swapaxes(rb, 2, 3).reshape(N_BUCKETS, ATTN_HEADS)
    bias_t = jnp.einsum("nh,nkq->hkq", rb, onehot,
                        precision=lax.Precision.HIGHEST)
    bias_t = bias_t * LOG2E
    cap_t = jnp.where(jnp.abs(rel) <= WINDOW, BIG, NEG).astype(F32)
    sink = (attn_sink[0].astype(F32) * LOG2E).reshape(ATTN_KV_HEADS, 1, 2, 2)
    sink = jnp.swapaxes(sink, 2, 3)[..., None]
    sink_t = jnp.broadcast_to(sink, (ATTN_KV_HEADS, 1, 2, 2, BLOCK)).reshape(
        ATTN_KV_HEADS, 1, ATTN_GROUP * BLOCK)

    inv = ROPE_BASE ** (-jnp.arange(0, RET_KEY_DIM, 2, dtype=F32) / RET_KEY_DIM)
    inv_full = jnp.tile(jnp.repeat(inv, 2), LANES // RET_KEY_DIM)[None, :]
    sgn = np.tile(np.array([-1.0, 1.0], np.float32), LANES // 2)[None, :]

    def trig_table(pos):
        ang = pos[:, None] * inv_full
        return jnp.stack([jnp.cos(ang), jnp.sin(ang), jnp.sin(ang) * sgn])

    rowtab = trig_table(jnp.arange(2 * BLOCK, dtype=F32))
    basetab = jnp.swapaxes(trig_table(jnp.arange(0, seq, BLOCK, dtype=F32)), 0, 1)

    lg_f = jax.nn.log_sigmoid(ret_decay_fwd[0].astype(F32))
    lg_b = jax.nn.log_sigmoid(ret_decay_bwd[0].astype(F32))

    proj, pb = _inproj(x2, norm_w[0].reshape(1, D_MODEL), w_in_p, rowtab, basetab, lg_b, seq)
    out = _mixer(proj, pb, x2, bias_t, cap_t, sink_t, lg_f, lg_b,
                 ret_gn_w[0].reshape(1, RET_WIDTH), attn_out_norm_w[0].reshape(1, ATTN_WIDTH),
                 w_o_p, final_norm_w.reshape(1, D_MODEL), batch, seq)
    return out.reshape(batch, seq, D_MODEL)
```

```python
import functools
import math

import numpy as np
import jax
import jax.numpy as jnp
from jax import lax
from jax.experimental import pallas as pl
from jax.experimental.pallas import tpu as pltpu

D_MODEL = 2048
ATTN_HEAD_DIM = 64
ATTN_WIDTH = 1024
ATTN_HEADS = 16
ATTN_KV_HEADS = 4
ATTN_GROUP = 4
ATTN_KV_COLS = 256
WINDOW = 128
BLOCK = 128
N_BUCKETS = 32
MAX_DISTANCE = 128
RET_WIDTH = 1024
RET_HEADS = 8
RET_VAL_DIM = 128
RET_KEY_DIM = 64
RET_QK_COLS = 512
CHUNK = 128
ROPE_BASE = 10000.0
EPS = 1e-6
NEG = -1e30
BIG = 3e38
LOG2E = math.log2(math.e)

LANES = 128
N_PAIRS = RET_HEADS // 2
KW = 3 * BLOCK
CB = 512

OFF_QA = 0
OFF_KA = OFF_QA + ATTN_WIDTH
OFF_GA = OFF_KA + 2 * ATTN_KV_COLS
OFF_QR = OFF_GA + ATTN_WIDTH
OFF_KR = OFF_QR + RET_QK_COLS
OFF_VR = OFF_KR + RET_QK_COLS
OFF_GR = OFF_VR + RET_WIDTH
IN_WIDTH = OFF_GR + RET_WIDTH

VMEM_LIMIT = 56 * 1024 * 1024

F32 = jnp.float32
BF16 = jnp.bfloat16


def _roll_half(x):
    return pltpu.roll(x.astype(F32), LANES // 2, axis=1).astype(x.dtype)


PB_ROWS = RET_KEY_DIM


def _first_step():
    return jnp.logical_and(pl.program_id(0) == 0, pl.program_id(1) == 0)


def _lane_decay(lg_ref, c, expo):
    lane = lax.broadcasted_iota(jnp.int32, (CHUNK, LANES), 1)
    lg = jnp.where(lane < RET_KEY_DIM, lg_ref[2 * c], lg_ref[2 * c + 1])
    return jnp.exp(lg * expo)


def _state_mask():
    r = lax.broadcasted_iota(jnp.int32, (LANES, 2 * RET_VAL_DIM), 0)
    m = lax.broadcasted_iota(jnp.int32, (LANES, 2 * RET_VAL_DIM), 1)
    return (r // RET_KEY_DIM) == (m // RET_VAL_DIM)


def _chunk_decay(lg_ref, c):
    r = lax.broadcasted_iota(jnp.int32, (LANES, 2 * RET_VAL_DIM), 0)
    lg = jnp.where(r < RET_KEY_DIM, lg_ref[2 * c], lg_ref[2 * c + 1])
    return jnp.where(_state_mask(), jnp.exp(lg * float(CHUNK)), 0.0)


def _row_index():
    return lax.broadcasted_iota(jnp.int32, (CHUNK, LANES), 0).astype(F32)


def _state_update(st_ref, c, k_rot, kdec, gdec, vpair):
    kd = (k_rot * kdec).astype(BF16)
    upd = lax.dot_general(kd, vpair, (((0,), (0,)), ((), ())),
                          preferred_element_type=F32)
    st_ref[c] = st_ref[c] * gdec + jnp.where(_state_mask(), upd, 0.0)


def _pair_cols(refs, rows, c, width):
    per_ref = CB // width
    return refs[c // per_ref][rows, width * (c % per_ref):width * (c % per_ref + 1)]


def _silu(g):
    return g / (1.0 + jnp.exp(-g))


def _inproj_kernel(lgb_ref, x_ref, nw_ref, w_ref, rowtab_ref, basetab_ref, o_ref, pb_ref,
                   st_ref, kdb_ref, gb_ref, *, row_splits, tiles_per_seq):
    tm = x_ref.shape[0]
    nw = nw_ref[...]
    step = pl.program_id(0)
    tile_blk = (tiles_per_seq - 1 - step % tiles_per_seq) * (tm // BLOCK)

    @pl.when(step == 0)
    def _():
        idx = _row_index()
        for c in range(N_PAIRS):
            kdb_ref[c] = _lane_decay(lgb_ref, c, idx)
            gb_ref[c] = _chunk_decay(lgb_ref, c)

    @pl.when(step % tiles_per_seq == 0)
    def _():
        st_ref[...] = jnp.zeros_like(st_ref)

    start = 0
    for rc in row_splits:
        rows = slice(start, start + rc)
        x = x_ref[rows, :]
        ms = jnp.mean(x * x, axis=-1, keepdims=True)
        h = (x * lax.rsqrt(ms + EPS) * nw).astype(BF16)
        ca, sa, sas = rowtab_ref[0, :rc, :], rowtab_ref[1, :rc, :], rowtab_ref[2, :rc, :]
        base = basetab_ref[tile_blk + start // BLOCK]
        cb, sb, sbs = base[0:1, :], base[1:2, :], base[2:3, :]
        cos = ca * cb - sa * sb
        sin_signed = sas * cb + ca * sbs
        lane = lax.broadcasted_iota(jnp.int32, (rc, LANES), 1)
        even = (lane % 2) == 0
        start += rc
        for col in range(0, IN_WIDTH, CB):
            res = jnp.dot(h, w_ref[:, col:col + CB], preferred_element_type=F32)
            if OFF_GA <= col < OFF_QR or col >= OFF_GR:
                res = _silu(res)
            elif OFF_QR <= col < OFF_VR:
                parts = []
                for c in range(CB // LANES):
                    v = res[:, LANES * c:LANES * (c + 1)]
                    partner = jnp.where(even, pltpu.roll(v, LANES - 1, axis=1),
                                        pltpu.roll(v, 1, axis=1))
                    parts.append(v * cos + partner * sin_signed)
                res = jnp.concatenate(parts, axis=1)
            o_ref[rows, col:col + CB] = res.astype(o_ref.dtype)

    for ch in reversed(range(tm // CHUNK)):
        rows = slice(CHUNK * ch, CHUNK * (ch + 1))
        for c in range(N_PAIRS):
            st = st_ref[c]
            pb_ref[ch, c] = (st[:PB_ROWS] + st[PB_ROWS:]).astype(BF16)
            k = o_ref[rows, OFF_KR + LANES * c:OFF_KR + LANES * (c + 1)].astype(F32)
            vpair = o_ref[rows, OFF_VR + 2 * RET_VAL_DIM * c:OFF_VR + 2 * RET_VAL_DIM * (c + 1)]
            _state_update(st_ref, c, k, kdb_ref[c], gb_ref[c], vpair)


def _inproj(x2, norm_w, w_in_p, rowtab, basetab, lg_b, seq, tm=512, row_splits=(256, 256)):
    m = x2.shape[0]
    assert sum(row_splits) == tm and max(row_splits) <= rowtab.shape[1]
    tps = seq // tm
    cpt = tm // CHUNK
    tile = lambda i: (i // tps) * tps + (tps - 1 - i % tps)
    return pl.pallas_call(
        functools.partial(_inproj_kernel, row_splits=row_splits, tiles_per_seq=tps),
        grid=(m // tm,),
        in_specs=[
            pl.BlockSpec(memory_space=pltpu.SMEM),
            pl.BlockSpec((tm, D_MODEL), lambda i: (tile(i), 0)),
            pl.BlockSpec((1, D_MODEL), lambda i: (0, 0)),
            pl.BlockSpec((D_MODEL, IN_WIDTH), lambda i: (0, 0)),
            pl.BlockSpec(rowtab.shape, lambda i: (0, 0, 0)),
            pl.BlockSpec(basetab.shape, lambda i: (0, 0, 0)),
        ],
        out_specs=[
            pl.BlockSpec((tm, IN_WIDTH), lambda i: (tile(i), 0)),
            pl.BlockSpec((cpt, N_PAIRS, PB_ROWS, 2 * RET_VAL_DIM), lambda i: (tile(i), 0, 0, 0)),
        ],
        out_shape=[
            jax.ShapeDtypeStruct((m, IN_WIDTH), BF16),
            jax.ShapeDtypeStruct((m // CHUNK, N_PAIRS, PB_ROWS, 2 * RET_VAL_DIM), BF16),
        ],
        scratch_shapes=[pltpu.VMEM((N_PAIRS, LANES, 2 * RET_VAL_DIM), F32),
                        pltpu.VMEM((N_PAIRS, CHUNK, LANES), F32),
                        pltpu.VMEM((N_PAIRS, LANES, 2 * RET_VAL_DIM), F32)],
        compiler_params=pltpu.CompilerParams(
            dimension_semantics=("arbitrary",),
            vmem_limit_bytes=VMEM_LIMIT),
        name="inproj",
    )(lg_b, x2, norm_w, w_in_p, rowtab, basetab)


GROUP_ROWS = 2 * BLOCK
OUT_AFTER = 1


def _attention_stage(i, nt, q_ref, kvp_ref, kvc_ref, kvn_ref, bias_ref, cap_ref, sink_ref, a_scr):
    n_blk = q_ref.shape[0] // BLOCK
    hd = ATTN_HEAD_DIM
    kv = jnp.concatenate([kvp_ref[...], kvc_ref[...], kvn_ref[...]], axis=0)
    lane = lax.broadcasted_iota(jnp.int32, (kv.shape[0], LANES), 1)
    low = lane < hd

    tile4 = lambda c: jnp.concatenate([c] * ATTN_GROUP, axis=1)
    cap_l, cap_r = tile4(cap_ref[:BLOCK]), tile4(cap_ref[2 * BLOCK:])
    cap_first = jnp.where(i > 0, cap_l, NEG)
    cap_last = jnp.where(i < nt - 1, cap_r, NEG)

    k_side, v_ones = {}, {}
    for j in range(ATTN_KV_COLS // LANES):
        kj = kv[:, LANES * j:LANES * (j + 1)]
        vj = kv[:, ATTN_KV_COLS + LANES * j:ATTN_KV_COLS + LANES * (j + 1)]
        for hg in range(2):
            mine = low if hg == 0 else jnp.logical_not(low)
            k_here = jnp.where(mine, kj, jnp.zeros_like(kj))
            k_side[(2 * j + hg, hg)] = k_here
            k_side[(2 * j + hg, 1 - hg)] = _roll_half(k_here)
            v_ones[2 * j + hg] = jnp.where(mine, vj, jnp.ones_like(vj))

    def scores(item):
        blk, g = item
        win = slice(BLOCK * blk, BLOCK * blk + KW)
        qs = jnp.concatenate(
            [q_ref[BLOCK * blk:BLOCK * (blk + 1), LANES * c:LANES * (c + 1)]
             for c in (2 * g, 2 * g + 1)], axis=0)
        return [lax.dot_general(k_side[(g, half)][win], qs, (((1,), (1,)), ((), ())),
                                preferred_element_type=F32) for half in range(2)]

    def softmax(item, st_pair):
        blk, g = item
        c_l = cap_first if blk == 0 else cap_l
        c_r = cap_last if blk == n_blk - 1 else cap_r
        st = jnp.concatenate(st_pair, axis=1)
        b = jnp.concatenate([bias_ref[ATTN_GROUP * g + t] for t in range(ATTN_GROUP)], axis=1)
        t = jnp.concatenate([
            jnp.minimum(st[:BLOCK] + b[:BLOCK], c_l),
            st[BLOCK:2 * BLOCK] + b[BLOCK:2 * BLOCK],
            jnp.minimum(st[2 * BLOCK:] + b[2 * BLOCK:], c_r),
        ], axis=0)
        sk = sink_ref[g]
        m = jnp.maximum(jnp.max(t, axis=0, keepdims=True), sk)
        return jnp.exp2(t - m).astype(BF16), jnp.exp2(sk - m)

    def finish(item, probs):
        blk, g = item
        p, e_sink = probs
        win = slice(BLOCK * blk, BLOCK * blk + KW)
        ot = lax.dot_general(v_ones[g][win], p, (((0,), (0,)), ((), ())),
                             preferred_element_type=F32)
        hg = g % 2
        num = ot[hd * hg:hd * (hg + 1)]
        den = ot[hd * (1 - hg):hd * (1 - hg) + 1]
        out = num * (1.0 / (den + e_sink))
        out_t = jnp.concatenate([out[:, :2 * BLOCK], out[:, 2 * BLOCK:]], axis=0)
        for r in range(2):
            c = 2 * g + r
            a_scr[BLOCK * blk:BLOCK * (blk + 1), LANES * c:LANES * (c + 1)] = (
                out_t[:, BLOCK * r:BLOCK * (r + 1)].T.astype(a_scr.dtype))

    return scores, softmax, finish


def _retention_stage(q_ref, k_ref, v_refs, g_refs, pb_ref, gnw_ref, st_ref, d_ref, qdf_ref,
                     qdb_ref, kdf_ref, gf_ref, r_scr):
    lane = lax.broadcasted_iota(jnp.int32, (CHUNK, LANES), 1)
    low_half = lane < RET_KEY_DIM
    first_v = lax.broadcasted_iota(jnp.int32, (CHUNK, 2 * RET_VAL_DIM), 1) < RET_VAL_DIM
    first_s = lax.broadcasted_iota(jnp.int32, (PB_ROWS, 2 * RET_VAL_DIM), 1) < RET_VAL_DIM

    def scores(item):
        ch, c = item
        rows = slice(CHUNK * ch, CHUNK * (ch + 1))
        kb = k_ref[rows, LANES * c:LANES * (c + 1)]
        q = q_ref[rows, LANES * c:LANES * (c + 1)]
        out = []
        for half in range(2):
            sel = low_half if half == 0 else jnp.logical_not(low_half)
            qh = jnp.where(sel, q, jnp.zeros_like(q))
            out.append(lax.dot_general(qh, kb, (((1,), (1,)), ((), ())),
                                       preferred_element_type=F32))
        return out

    def decay(item, s_pair):
        ch, c = item
        rows = slice(CHUNK * ch, CHUNK * (ch + 1))
        q = q_ref[rows, LANES * c:LANES * (c + 1)].astype(F32)
        parts = [(s_pair[half] * d_ref[2 * c + half]).astype(BF16) for half in range(2)]
        parts.append((q * qdf_ref[c]).astype(BF16))
        parts.append((q * qdb_ref[c]).astype(BF16))
        return jnp.concatenate(parts, axis=1)

    def finish(item, lhs):
        ch, c = item
        rows = slice(CHUNK * ch, CHUNK * (ch + 1))
        k = k_ref[rows, LANES * c:LANES * (c + 1)].astype(F32)
        vpair = _pair_cols(v_refs, rows, c, 2 * RET_VAL_DIM)
        zero_v = jnp.zeros_like(vpair)
        pb = pb_ref[ch, c]
        zero_s = jnp.zeros_like(pb)
        rhs = jnp.concatenate([
            jnp.where(first_v, vpair, zero_v),
            jnp.where(first_v, zero_v, vpair),
            st_ref[c].astype(BF16),
            jnp.where(first_s, pb, zero_s),
            jnp.where(first_s, zero_s, pb),
        ], axis=0)
        o = jnp.dot(lhs, rhs, preferred_element_type=F32)
        _state_update(st_ref, c, k, kdf_ref[c], gf_ref[c], vpair)
        for half in range(2):
            head = 2 * c + half
            lo_c = RET_VAL_DIM * head
            oh = o[:, RET_VAL_DIM * half:RET_VAL_DIM * (half + 1)]
            mu = jnp.mean(oh, axis=-1, keepdims=True)
            dev = oh - mu
            var = jnp.mean(dev * dev, axis=-1, keepdims=True)
            y = dev * lax.rsqrt(var + EPS) * gnw_ref[:, lo_c:lo_c + RET_VAL_DIM]
            g = _pair_cols(g_refs, rows, head, RET_VAL_DIM).astype(F32)
            r_scr[rows, lo_c:lo_c + RET_VAL_DIM] = (y * g).astype(r_scr.dtype)

    return scores, decay, finish


def _mixer_kernel(lgf_ref, lgb_ref, qa_ref, kvp_ref, kvc_ref, kvn_ref, bias_ref, cap_ref, sink_ref,
                  ga0_ref, ga1_ref, qr_ref, kr_ref, v0_ref, v1_ref, gr0_ref, gr1_ref, pb_ref,
                  gnw_ref, x_ref, anw_ref, w_ref, fw_ref, o_ref,
                  a_scr, r_scr, st_ref, d_ref, qdf_ref, qdb_ref, kdf_ref, gf_ref):
    @pl.when(_first_step())
    def _():
        idx = _row_index()
        col = lax.broadcasted_iota(jnp.int32, (CHUNK, CHUNK), 1).astype(F32)
        diff = idx - col
        for c in range(N_PAIRS):
            qdf_ref[c] = _lane_decay(lgf_ref, c, idx + 1.0)
            qdb_ref[c] = _lane_decay(lgb_ref, c, float(CHUNK) - idx)
            kdf_ref[c] = _lane_decay(lgf_ref, c, float(CHUNK - 1) - idx)
            gf_ref[c] = _chunk_decay(lgf_ref, c)
        for hh in range(RET_HEADS):
            d_ref[hh] = jnp.exp(jnp.where(diff >= 0.0, lgf_ref[hh] * diff, lgb_ref[hh] * (-diff)))

    @pl.when(pl.program_id(1) == 0)
    def _():
        st_ref[...] = jnp.zeros_like(st_ref)

    tq = qa_ref.shape[0]
    stage = {
        "a": _attention_stage(pl.program_id(1), pl.num_programs(1), qa_ref, kvp_ref, kvc_ref,
                              kvn_ref, bias_ref, cap_ref, sink_ref, a_scr),
        "r": _retention_stage(qr_ref, kr_ref, (v0_ref, v1_ref), (gr0_ref, gr1_ref), pb_ref,
                              gnw_ref, st_ref, d_ref, qdf_ref, qdb_ref, kdf_ref, gf_ref, r_scr),
    }

    n_grp = tq // GROUP_ROWS

    def out_group(grp):
        rows = slice(GROUP_ROWS * grp, GROUP_ROWS * (grp + 1))
        a = a_scr[rows, :].astype(F32)
        ms = jnp.mean(a * a, axis=-1, keepdims=True)
        g = jnp.concatenate([ga0_ref[rows, :], ga1_ref[rows, :]], axis=1).astype(F32)
        gated = (a * lax.rsqrt(ms + EPS) * anw_ref[...] * g).astype(BF16)
        y = (x_ref[rows, :]
             + jnp.dot(r_scr[rows, :], w_ref[ATTN_WIDTH:, :], preferred_element_type=F32)
             + jnp.dot(gated, w_ref[:ATTN_WIDTH, :], preferred_element_type=F32))
        ms = jnp.mean(y * y, axis=-1, keepdims=True)
        o_ref[rows, :] = y * lax.rsqrt(ms + EPS) * fw_ref[...]

    per = GROUP_ROWS // BLOCK
    items = []
    for grp in range(n_grp):
        for blk in range(per * grp, per * (grp + 1)):
            for u in range(ATTN_KV_HEADS):
                items.append(("a", (blk, u), grp))
                items.append(("r", (blk, u), grp))
    per_group = len(items) // n_grp
    pending = stage[items[0][0]][0](items[0][1])
    closing = None
    for idx, (kind, item, grp) in enumerate(items):
        cur = pending
        if idx + 1 < len(items):
            nxt = items[idx + 1]
            pending = stage[nxt[0]][0](nxt[1])
        mid = stage[kind][1](item, cur)
        if closing is not None:
            stage[closing[0]][2](closing[1], closing[2])
        closing = (kind, item, mid)
        if grp > 0 and idx % per_group == OUT_AFTER:
            out_group(grp - 1)
    stage[closing[0]][2](closing[1], closing[2])
    out_group(n_grp - 1)


def _mixer(proj, pb, x2, bias_t, cap_t, sink_t, lg_f, lg_b, gn_w, attn_nw, w_o_p, final_w,
           batch, seq, tq=512):
    nt = seq // tq
    bpt = tq // BLOCK
    nb = seq // BLOCK
    row = lambda b, i: b * nt + i
    col = lambda off: (lambda b, i: (row(b, i), off // CB))
    prev_blk = lambda b, i: (b * nb + jnp.maximum(bpt * i - 1, 0), OFF_KA // CB)
    next_blk = lambda b, i: (b * nb + jnp.minimum(bpt * (i + 1), nb - 1), OFF_KA // CB)
    const2 = lambda b, i: (0, 0)
    const3 = lambda b, i: (0, 0, 0)
    smem = pl.BlockSpec(memory_space=pltpu.SMEM)
    state_shape = (N_PAIRS, LANES, 2 * RET_VAL_DIM)
    lane_tab = (N_PAIRS, CHUNK, LANES)
    return pl.pallas_call(
        _mixer_kernel,
        grid=(batch, nt),
        in_specs=[
            smem, smem,
            pl.BlockSpec((tq, ATTN_WIDTH), lambda b, i: (row(b, i), OFF_QA // ATTN_WIDTH)),
            pl.BlockSpec((BLOCK, CB), prev_blk),
            pl.BlockSpec((tq, CB), col(OFF_KA)),
            pl.BlockSpec((BLOCK, CB), next_blk),
            pl.BlockSpec((ATTN_HEADS, KW, BLOCK), const3),
            pl.BlockSpec((KW, BLOCK), const2),
            pl.BlockSpec((ATTN_KV_HEADS, 1, ATTN_GROUP * BLOCK), const3),
            pl.BlockSpec((tq, CB), col(OFF_GA)),
            pl.BlockSpec((tq, CB), col(OFF_GA + CB)),
            pl.BlockSpec((tq, CB), col(OFF_QR)),
            pl.BlockSpec((tq, CB), col(OFF_KR)),
            pl.BlockSpec((tq, CB), col(OFF_VR)),
            pl.BlockSpec((tq, CB), col(OFF_VR + CB)),
            pl.BlockSpec((tq, CB), col(OFF_GR)),
            pl.BlockSpec((tq, CB), col(OFF_GR + CB)),
            pl.BlockSpec((bpt, N_PAIRS, PB_ROWS, 2 * RET_VAL_DIM), lambda b, i: (row(b, i), 0, 0, 0)),
            pl.BlockSpec((1, RET_WIDTH), const2),
            pl.BlockSpec((tq, D_MODEL), lambda b, i: (row(b, i), 0)),
            pl.BlockSpec((1, ATTN_WIDTH), const2),
            pl.BlockSpec((ATTN_WIDTH + RET_WIDTH, D_MODEL), const2),
            pl.BlockSpec((1, D_MODEL), const2),
        ],
        out_specs=pl.BlockSpec((tq, D_MODEL), lambda b, i: (row(b, i), 0)),
        out_shape=jax.ShapeDtypeStruct((batch * seq, D_MODEL), F32),
        scratch_shapes=[pltpu.VMEM((tq, ATTN_WIDTH), BF16),
                        pltpu.VMEM((tq, RET_WIDTH), BF16),
                        pltpu.VMEM(state_shape, F32),
                        pltpu.VMEM((RET_HEADS, CHUNK, CHUNK), F32),
                        pltpu.VMEM(lane_tab, F32),
                        pltpu.VMEM(lane_tab, F32),
                        pltpu.VMEM(lane_tab, F32),
                        pltpu.VMEM(state_shape, F32)],
        compiler_params=pltpu.CompilerParams(
            dimension_semantics=("arbitrary", "arbitrary"),
            vmem_limit_bytes=VMEM_LIMIT),
        name="mixer",
    )(lg_f, lg_b, proj, proj, proj, proj, bias_t, cap_t, sink_t, proj, proj, proj, proj, proj, proj,
      proj, proj, pb, gn_w, x2, attn_nw, w_o_p, final_w)


def _t5_bucket(rel):
    nb = N_BUCKETS // 2
    max_exact = nb // 2
    ret = jnp.where(rel > 0, nb, 0)
    n = jnp.abs(rel)
    nf = jnp.maximum(n, 1).astype(F32)
    large = max_exact + (jnp.log(nf / max_exact) / math.log(MAX_DISTANCE / max_exact)
                         * (nb - max_exact)).astype(jnp.int32)
    large = jnp.minimum(large, nb - 1)
    return ret + jnp.where(n < max_exact, n, large)


def kernel(x, norm_w, w_in, attn_sink, rel_bias, attn_out_norm_w, ret_decay_fwd,
           ret_decay_bwd, ret_gn_w, w_out, final_norm_w):
    batch, seq, _ = x.shape
    assert norm_w.shape[0] == 1 and seq % BLOCK == 0
    x2 = x.reshape(batch * seq, D_MODEL)

    col_scale = np.ones((1, IN_WIDTH), np.float32)
    col_scale[:, OFF_QA:OFF_QA + ATTN_WIDTH] = ATTN_HEAD_DIM ** -0.5 * LOG2E
    col_scale[:, OFF_KR:OFF_KR + RET_QK_COLS] = RET_KEY_DIM ** -0.5
    w_in_p = (w_in[0] * col_scale).astype(BF16)
    w_o_p = w_out[0].astype(BF16)

    qi = jnp.arange(BLOCK, dtype=jnp.int32)[None, :]
    kt = jnp.arange(KW, dtype=jnp.int32)[:, None]
    rel = kt - BLOCK - qi
    bucket = _t5_bucket(rel) & (N_BUCKETS - 1)
    onehot = (bucket[None] == jnp.arange(N_BUCKETS)[:, None, None]).astype(F32)
    rb = rel_bias.astype(F32).reshape(N_BUCKETS, ATTN_KV_HEADS, 2, 2)
    rb = jnp.swapaxes(rb, 2, 3).reshape(N_BUCKETS, ATTN_HEADS)
    bias_t = jnp.einsum("nh,nkq->hkq", rb, onehot,
                        precision=lax.Precision.HIGHEST)
    bias_t = bias_t * LOG2E
    cap_t = jnp.where(jnp.abs(rel) <= WINDOW, BIG, NEG).astype(F32)
    sink = (attn_sink[0].astype(F32) * LOG2E).reshape(ATTN_KV_HEADS, 1, 2, 2)
    sink = jnp.swapaxes(sink, 2, 3)[..., None]
    sink_t = jnp.broadcast_to(sink, (ATTN_KV_HEADS, 1, 2, 2, BLOCK)).reshape(
        ATTN_KV_HEADS, 1, ATTN_GROUP * BLOCK)

    inv = ROPE_BASE ** (-jnp.arange(0, RET_KEY_DIM, 2, dtype=F32) / RET_KEY_DIM)
    inv_full = jnp.tile(jnp.repeat(inv, 2), LANES // RET_KEY_DIM)[None, :]
    sgn = np.tile(np.array([-1.0, 1.0], np.float32), LANES // 2)[None, :]

    def trig_table(pos):
        ang = pos[:, None] * inv_full
        return jnp.stack([jnp.cos(ang), jnp.sin(ang), jnp.sin(ang) * sgn])

    rowtab = trig_table(jnp.arange(2 * BLOCK, dtype=F32))
    basetab = jnp.swapaxes(trig_table(jnp.arange(0, seq, BLOCK, dtype=F32)), 0, 1)

    lg_f = jax.nn.log_sigmoid(ret_decay_fwd[0].astype(F32))
    lg_b = jax.nn.log_sigmoid(ret_decay_bwd[0].astype(F32))

    proj, pb = _inproj(x2, norm_w[0].reshape(1, D_MODEL), w_in_p, rowtab, basetab, lg_b, seq)
    out = _mixer(proj, pb, x2, bias_t, cap_t, sink_t, lg_f, lg_b,
                 ret_gn_w[0].reshape(1, RET_WIDTH), attn_out_norm_w[0].reshape(1, ATTN_WIDTH),
                 w_o_p, final_norm_w.reshape(1, D_MODEL), batch, seq)
    return out.reshape(batch, seq, D_MODEL)
```

```python
import functools
import math

import numpy as np
import jax
import jax.numpy as jnp
from jax import lax
from jax.experimental import pallas as pl
from jax.experimental.pallas import tpu as pltpu

D_MODEL = 2048
ATTN_HEAD_DIM = 64
ATTN_WIDTH = 1024
ATTN_HEADS = 16
ATTN_KV_HEADS = 4
ATTN_GROUP = 4
ATTN_KV_COLS = 256
WINDOW = 128
BLOCK = 128
N_BUCKETS = 32
MAX_DISTANCE = 128
RET_WIDTH = 1024
RET_HEADS = 8
RET_VAL_DIM = 128
RET_KEY_DIM = 64
RET_QK_COLS = 512
CHUNK = 128
ROPE_BASE = 10000.0
EPS = 1e-6
NEG = -1e30
BIG = 3e38
LOG2E = math.log2(math.e)

LANES = 128
N_PAIRS = RET_HEADS // 2
KW = 3 * BLOCK
CB = 512

OFF_QA = 0
OFF_KA = OFF_QA + ATTN_WIDTH
OFF_GA = OFF_KA + 2 * ATTN_KV_COLS
OFF_QR = OFF_GA + ATTN_WIDTH
OFF_KR = OFF_QR + RET_QK_COLS
OFF_VR = OFF_KR + RET_QK_COLS
OFF_GR = OFF_VR + RET_WIDTH
IN_WIDTH = OFF_GR + RET_WIDTH

VMEM_LIMIT = 56 * 1024 * 1024

F32 = jnp.float32
BF16 = jnp.bfloat16


def _roll_half(x):
    return pltpu.roll(x.astype(F32), LANES // 2, axis=1).astype(x.dtype)


PB_ROWS = RET_KEY_DIM


def _first_step():
    return jnp.logical_and(pl.program_id(0) == 0, pl.program_id(1) == 0)


def _lane_decay(lg_ref, c, expo):
    lane = lax.broadcasted_iota(jnp.int32, (CHUNK, LANES), 1)
    lg = jnp.where(lane < RET_KEY_DIM, lg_ref[2 * c], lg_ref[2 * c + 1])
    return jnp.exp(lg * expo)


def _state_mask():
    r = lax.broadcasted_iota(jnp.int32, (LANES, 2 * RET_VAL_DIM), 0)
    m = lax.broadcasted_iota(jnp.int32, (LANES, 2 * RET_VAL_DIM), 1)
    return (r // RET_KEY_DIM) == (m // RET_VAL_DIM)


def _chunk_decay(lg_ref, c):
    r = lax.broadcasted_iota(jnp.int32, (LANES, 2 * RET_VAL_DIM), 0)
    lg = jnp.where(r < RET_KEY_DIM, lg_ref[2 * c], lg_ref[2 * c + 1])
    return jnp.where(_state_mask(), jnp.exp(lg * float(CHUNK)), 0.0)


def _row_index():
    return lax.broadcasted_iota(jnp.int32, (CHUNK, LANES), 0).astype(F32)


def _state_update(st_ref, c, k_rot, kdec, gdec, vpair):
    kd = (k_rot * kdec).astype(BF16)
    upd = lax.dot_general(kd, vpair, (((0,), (0,)), ((), ())),
                          preferred_element_type=F32)
    st_ref[c] = st_ref[c] * gdec + jnp.where(_state_mask(), upd, 0.0)


def _pair_cols(refs, rows, c, width):
    per_ref = CB // width
    return refs[c // per_ref][rows, width * (c % per_ref):width * (c % per_ref + 1)]


def _silu(g):
    return g / (1.0 + jnp.exp(-g))


def _col_scale(col):
    if OFF_QA <= col < OFF_QA + ATTN_WIDTH:
        return ATTN_HEAD_DIM ** -0.5 * LOG2E
    if OFF_KR <= col < OFF_KR + RET_QK_COLS:
        return RET_KEY_DIM ** -0.5
    return 1.0


WB = 256


def _load_weight(whbm_ref, w_ref, stage_ref, sem_ref, col_scale=lambda col: 1.0):
    n_blk = w_ref.shape[1] // WB
    rows_per_iter = 256

    def copy(c):
        return pltpu.make_async_copy(whbm_ref.at[:, pl.ds(c * WB, WB)],
                                     stage_ref.at[c % 2], sem_ref.at[c % 2])

    copy(0).start()
    for c in range(n_blk):
        if c + 1 < n_blk:
            copy(c + 1).start()
        copy(c).wait()
        scale = col_scale(c * WB)

        def body(r, carry, c=c, scale=scale):
            rs = pl.ds(pl.multiple_of(r * rows_per_iter, rows_per_iter), rows_per_iter)
            w = stage_ref[c % 2, rs, :]
            if scale != 1.0:
                w = w * scale
            w_ref[rs, c * WB:(c + 1) * WB] = w.astype(BF16)
            return carry

        lax.fori_loop(0, w_ref.shape[0] // rows_per_iter, body, 0)


def _inproj_kernel(lgb_ref, x_ref, nw_ref, whbm_ref, rowtab_ref, basetab_ref, o_ref, pb_ref,
                   w_ref, stage_ref, sem_ref, st_ref, kdb_ref, gb_ref, *, row_splits,
                   tiles_per_seq):
    tm = x_ref.shape[0]
    nw = nw_ref[...]
    step = pl.program_id(0)
    tile_blk = (tiles_per_seq - 1 - step % tiles_per_seq) * (tm // BLOCK)

    @pl.when(step == 0)
    def _():
        _load_weight(whbm_ref, w_ref, stage_ref, sem_ref, _col_scale)
        idx = _row_index()
        for c in range(N_PAIRS):
            kdb_ref[c] = _lane_decay(lgb_ref, c, idx)
            gb_ref[c] = _chunk_decay(lgb_ref, c)

    @pl.when(step % tiles_per_seq == 0)
    def _():
        st_ref[...] = jnp.zeros_like(st_ref)

    start = 0
    for rc in row_splits:
        rows = slice(start, start + rc)
        x = x_ref[rows, :]
        ms = jnp.mean(x * x, axis=-1, keepdims=True)
        h = (x * lax.rsqrt(ms + EPS) * nw).astype(BF16)
        ca, sa, sas = rowtab_ref[0, :rc, :], rowtab_ref[1, :rc, :], rowtab_ref[2, :rc, :]
        base = basetab_ref[tile_blk + start // BLOCK]
        cb, sb, sbs = base[0:1, :], base[1:2, :], base[2:3, :]
        cos = ca * cb - sa * sb
        sin_signed = sas * cb + ca * sbs
        lane = lax.broadcasted_iota(jnp.int32, (rc, LANES), 1)
        even = (lane % 2) == 0
        start += rc
        for col in range(0, IN_WIDTH, CB):
            res = jnp.dot(h, w_ref[:, col:col + CB], preferred_element_type=F32)
            if OFF_GA <= col < OFF_QR or col >= OFF_GR:
                res = _silu(res)
            elif OFF_QR <= col < OFF_VR:
                parts = []
                for c in range(CB // LANES):
                    v = res[:, LANES * c:LANES * (c + 1)]
                    partner = jnp.where(even, pltpu.roll(v, LANES - 1, axis=1),
                                        pltpu.roll(v, 1, axis=1))
                    parts.append(v * cos + partner * sin_signed)
                res = jnp.concatenate(parts, axis=1)
            o_ref[rows, col:col + CB] = res.astype(o_ref.dtype)

    for ch in reversed(range(tm // CHUNK)):
        rows = slice(CHUNK * ch, CHUNK * (ch + 1))
        for c in range(N_PAIRS):
            st = st_ref[c]
            pb_ref[ch, c] = (st[:PB_ROWS] + st[PB_ROWS:]).astype(BF16)
            k = o_ref[rows, OFF_KR + LANES * c:OFF_KR + LANES * (c + 1)].astype(F32)
            vpair = o_ref[rows, OFF_VR + 2 * RET_VAL_DIM * c:OFF_VR + 2 * RET_VAL_DIM * (c + 1)]
            _state_update(st_ref, c, k, kdb_ref[c], gb_ref[c], vpair)


def _inproj(x2, norm_w, w_in, rowtab, basetab, lg_b, seq, tm=512, row_splits=(256, 256)):
    m = x2.shape[0]
    assert sum(row_splits) == tm and max(row_splits) <= rowtab.shape[1]
    tps = seq // tm
    cpt = tm // CHUNK
    tile = lambda i: (i // tps) * tps + (tps - 1 - i % tps)
    return pl.pallas_call(
        functools.partial(_inproj_kernel, row_splits=row_splits, tiles_per_seq=tps),
        grid=(m // tm,),
        in_specs=[
            pl.BlockSpec(memory_space=pltpu.SMEM),
            pl.BlockSpec((tm, D_MODEL), lambda i: (tile(i), 0)),
            pl.BlockSpec((1, D_MODEL), lambda i: (0, 0)),
            pl.BlockSpec(memory_space=pl.ANY),
            pl.BlockSpec(rowtab.shape, lambda i: (0, 0, 0)),
            pl.BlockSpec(basetab.shape, lambda i: (0, 0, 0)),
        ],
        out_specs=[
            pl.BlockSpec((tm, IN_WIDTH), lambda i: (tile(i), 0)),
            pl.BlockSpec((cpt, N_PAIRS, PB_ROWS, 2 * RET_VAL_DIM), lambda i: (tile(i), 0, 0, 0)),
        ],
        out_shape=[
            jax.ShapeDtypeStruct((m, IN_WIDTH), BF16),
            jax.ShapeDtypeStruct((m // CHUNK, N_PAIRS, PB_ROWS, 2 * RET_VAL_DIM), BF16),
        ],
        scratch_shapes=[pltpu.VMEM((D_MODEL, IN_WIDTH), BF16),
                        pltpu.VMEM((2, D_MODEL, WB), F32),
                        pltpu.SemaphoreType.DMA((2,)),
                        pltpu.VMEM((N_PAIRS, LANES, 2 * RET_VAL_DIM), F32),
                        pltpu.VMEM((N_PAIRS, CHUNK, LANES), F32),
                        pltpu.VMEM((N_PAIRS, LANES, 2 * RET_VAL_DIM), F32)],
        compiler_params=pltpu.CompilerParams(
            dimension_semantics=("arbitrary",),
            vmem_limit_bytes=VMEM_LIMIT),
        name="inproj",
    )(lg_b, x2, norm_w, w_in, rowtab, basetab)


GROUP_ROWS = 2 * BLOCK
OUT_AFTER = 1


def _attention_stage(i, nt, q_ref, kvp_ref, kvc_ref, kvn_ref, bias_ref, cap_ref, sink_ref, a_scr):
    n_blk = q_ref.shape[0] // BLOCK
    hd = ATTN_HEAD_DIM
    kv = jnp.concatenate([kvp_ref[...], kvc_ref[...], kvn_ref[...]], axis=0)
    lane = lax.broadcasted_iota(jnp.int32, (kv.shape[0], LANES), 1)
    low = lane < hd

    tile4 = lambda c: jnp.concatenate([c] * ATTN_GROUP, axis=1)
    cap_l, cap_r = tile4(cap_ref[:BLOCK]), tile4(cap_ref[2 * BLOCK:])
    cap_first = jnp.where(i > 0, cap_l, NEG)
    cap_last = jnp.where(i < nt - 1, cap_r, NEG)

    k_side, v_ones = {}, {}
    for j in range(ATTN_KV_COLS // LANES):
        kj = kv[:, LANES * j:LANES * (j + 1)]
        vj = kv[:, ATTN_KV_COLS + LANES * j:ATTN_KV_COLS + LANES * (j + 1)]
        for hg in range(2):
            mine = low if hg == 0 else jnp.logical_not(low)
            k_here = jnp.where(mine, kj, jnp.zeros_like(kj))
            k_side[(2 * j + hg, hg)] = k_here
            k_side[(2 * j + hg, 1 - hg)] = _roll_half(k_here)
            v_ones[2 * j + hg] = jnp.where(mine, vj, jnp.ones_like(vj))

    def scores(item):
        blk, g = item
        win = slice(BLOCK * blk, BLOCK * blk + KW)
        qs = jnp.concatenate(
            [q_ref[BLOCK * blk:BLOCK * (blk + 1), LANES * c:LANES * (c + 1)]
             for c in (2 * g, 2 * g + 1)], axis=0)
        return [lax.dot_general(k_side[(g, half)][win], qs, (((1,), (1,)), ((), ())),
                                preferred_element_type=F32) for half in range(2)]

    def softmax(item, st_pair):
        blk, g = item
        c_l = cap_first if blk == 0 else cap_l
        c_r = cap_last if blk == n_blk - 1 else cap_r
        st = jnp.concatenate(st_pair, axis=1)
        b = jnp.concatenate([bias_ref[ATTN_GROUP * g + t] for t in range(ATTN_GROUP)], axis=1)
        t = jnp.concatenate([
            jnp.minimum(st[:BLOCK] + b[:BLOCK], c_l),
            st[BLOCK:2 * BLOCK] + b[BLOCK:2 * BLOCK],
            jnp.minimum(st[2 * BLOCK:] + b[2 * BLOCK:], c_r),
        ], axis=0)
        sk = sink_ref[g]
        m = jnp.maximum(jnp.max(t, axis=0, keepdims=True), sk)
        return jnp.exp2(t - m).astype(BF16), jnp.exp2(sk - m)

    def finish(item, probs):
        blk, g = item
        p, e_sink = probs
        win = slice(BLOCK * blk, BLOCK * blk + KW)
        ot = lax.dot_general(v_ones[g][win], p, (((0,), (0,)), ((), ())),
                             preferred_element_type=F32)
        hg = g % 2
        num = ot[hd * hg:hd * (hg + 1)]
        den = ot[hd * (1 - hg):hd * (1 - hg) + 1]
        out = num * (1.0 / (den + e_sink))
        out_t = jnp.concatenate([out[:, :2 * BLOCK], out[:, 2 * BLOCK:]], axis=0)
        for r in range(2):
            c = 2 * g + r
            a_scr[BLOCK * blk:BLOCK * (blk + 1), LANES * c:LANES * (c + 1)] = (
                out_t[:, BLOCK * r:BLOCK * (r + 1)].T.astype(a_scr.dtype))

    return scores, softmax, finish


def _retention_stage(q_ref, k_ref, v_refs, g_refs, pb_ref, gnw_ref, st_ref, d_ref, qdf_ref,
                     qdb_ref, kdf_ref, gf_ref, r_scr):
    lane = lax.broadcasted_iota(jnp.int32, (CHUNK, LANES), 1)
    low_half = lane < RET_KEY_DIM
    first_v = lax.broadcasted_iota(jnp.int32, (CHUNK, 2 * RET_VAL_DIM), 1) < RET_VAL_DIM
    first_s = lax.broadcasted_iota(jnp.int32, (PB_ROWS, 2 * RET_VAL_DIM), 1) < RET_VAL_DIM

    def scores(item):
        ch, c = item
        rows = slice(CHUNK * ch, CHUNK * (ch + 1))
        kb = k_ref[rows, LANES * c:LANES * (c + 1)]
        q = q_ref[rows, LANES * c:LANES * (c + 1)]
        out = []
        for half in range(2):
            sel = low_half if half == 0 else jnp.logical_not(low_half)
            qh = jnp.where(sel, q, jnp.zeros_like(q))
            out.append(lax.dot_general(qh, kb, (((1,), (1,)), ((), ())),
                                       preferred_element_type=F32))
        return out

    def decay(item, s_pair):
        ch, c = item
        rows = slice(CHUNK * ch, CHUNK * (ch + 1))
        q = q_ref[rows, LANES * c:LANES * (c + 1)].astype(F32)
        parts = [(s_pair[half] * d_ref[2 * c + half]).astype(BF16) for half in range(2)]
        parts.append((q * qdf_ref[c]).astype(BF16))
        parts.append((q * qdb_ref[c]).astype(BF16))
        return jnp.concatenate(parts, axis=1)

    def finish(item, lhs):
        ch, c = item
        rows = slice(CHUNK * ch, CHUNK * (ch + 1))
        k = k_ref[rows, LANES * c:LANES * (c + 1)].astype(F32)
        vpair = _pair_cols(v_refs, rows, c, 2 * RET_VAL_DIM)
        zero_v = jnp.zeros_like(vpair)
        pb = pb_ref[ch, c]
        zero_s = jnp.zeros_like(pb)
        rhs = jnp.concatenate([
            jnp.where(first_v, vpair, zero_v),
            jnp.where(first_v, zero_v, vpair),
            st_ref[c].astype(BF16),
            jnp.where(first_s, pb, zero_s),
            jnp.where(first_s, zero_s, pb),
        ], axis=0)
        o = jnp.dot(lhs, rhs, preferred_element_type=F32)
        _state_update(st_ref, c, k, kdf_ref[c], gf_ref[c], vpair)
        for half in range(2):
            head = 2 * c + half
            lo_c = RET_VAL_DIM * head
            oh = o[:, RET_VAL_DIM * half:RET_VAL_DIM * (half + 1)]
            mu = jnp.mean(oh, axis=-1, keepdims=True)
            dev = oh - mu
            var = jnp.mean(dev * dev, axis=-1, keepdims=True)
            y = dev * lax.rsqrt(var + EPS) * gnw_ref[:, lo_c:lo_c + RET_VAL_DIM]
            g = _pair_cols(g_refs, rows, head, RET_VAL_DIM).astype(F32)
            r_scr[rows, lo_c:lo_c + RET_VAL_DIM] = (y * g).astype(r_scr.dtype)

    return scores, decay, finish


def _mixer_kernel(lgf_ref, lgb_ref, qa_ref, kvp_ref, kvc_ref, kvn_ref, bias_ref, cap_ref, sink_ref,
                  ga0_ref, ga1_ref, qr_ref, kr_ref, v0_ref, v1_ref, gr0_ref, gr1_ref, pb_ref,
                  gnw_ref, x_ref, anw_ref, whbm_ref, fw_ref, o_ref,
                  w_ref, stage_ref, sem_ref,
                  a_scr, r_scr, st_ref, d_ref, qdf_ref, qdb_ref, kdf_ref, gf_ref):
    @pl.when(_first_step())
    def _():
        _load_weight(whbm_ref, w_ref, stage_ref, sem_ref)
        idx = _row_index()
        col = lax.broadcasted_iota(jnp.int32, (CHUNK, CHUNK), 1).astype(F32)
        diff = idx - col
        for c in range(N_PAIRS):
            qdf_ref[c] = _lane_decay(lgf_ref, c, idx + 1.0)
            qdb_ref[c] = _lane_decay(lgb_ref, c, float(CHUNK) - idx)
            kdf_ref[c] = _lane_decay(lgf_ref, c, float(CHUNK - 1) - idx)
            gf_ref[c] = _chunk_decay(lgf_ref, c)
        for hh in range(RET_HEADS):
            d_ref[hh] = jnp.exp(jnp.where(diff >= 0.0, lgf_ref[hh] * diff, lgb_ref[hh] * (-diff)))

    @pl.when(pl.program_id(1) == 0)
    def _():
        st_ref[...] = jnp.zeros_like(st_ref)

    tq = qa_ref.shape[0]
    stage = {
        "a": _attention_stage(pl.program_id(1), pl.num_programs(1), qa_ref, kvp_ref, kvc_ref,
                              kvn_ref, bias_ref, cap_ref, sink_ref, a_scr),
        "r": _retention_stage(qr_ref, kr_ref, (v0_ref, v1_ref), (gr0_ref, gr1_ref), pb_ref,
                              gnw_ref, st_ref, d_ref, qdf_ref, qdb_ref, kdf_ref, gf_ref, r_scr),
    }

    n_grp = tq // GROUP_ROWS

    def out_group(grp):
        rows = slice(GROUP_ROWS * grp, GROUP_ROWS * (grp + 1))
        a = a_scr[rows, :].astype(F32)
        ms = jnp.mean(a * a, axis=-1, keepdims=True)
        g = jnp.concatenate([ga0_ref[rows, :], ga1_ref[rows, :]], axis=1).astype(F32)
        gated = (a * lax.rsqrt(ms + EPS) * anw_ref[...] * g).astype(BF16)
        y = (x_ref[rows, :]
             + jnp.dot(r_scr[rows, :], w_ref[ATTN_WIDTH:, :], preferred_element_type=F32)
             + jnp.dot(gated, w_ref[:ATTN_WIDTH, :], preferred_element_type=F32))
        ms = jnp.mean(y * y, axis=-1, keepdims=True)
        o_ref[rows, :] = y * lax.rsqrt(ms + EPS) * fw_ref[...]

    per = GROUP_ROWS // BLOCK
    items = []
    for grp in range(n_grp):
        for blk in range(per * grp, per * (grp + 1)):
            for u in range(ATTN_KV_HEADS):
                items.append(("a", (blk, u), grp))
                items.append(("r", (blk, u), grp))
    per_group = len(items) // n_grp
    pending = stage[items[0][0]][0](items[0][1])
    closing = None
    for idx, (kind, item, grp) in enumerate(items):
        cur = pending
        if idx + 1 < len(items):
            nxt = items[idx + 1]
            pending = stage[nxt[0]][0](nxt[1])
        mid = stage[kind][1](item, cur)
        if closing is not None:
            stage[closing[0]][2](closing[1], closing[2])
        closing = (kind, item, mid)
        if grp > 0 and idx % per_group == OUT_AFTER:
            out_group(grp - 1)
    stage[closing[0]][2](closing[1], closing[2])
    out_group(n_grp - 1)


def _mixer(proj, pb, x2, bias_t, cap_t, sink_t, lg_f, lg_b, gn_w, attn_nw, w_o_p, final_w,
           batch, seq, tq=512):
    nt = seq // tq
    bpt = tq // BLOCK
    nb = seq // BLOCK
    row = lambda b, i: b * nt + i
    col = lambda off: (lambda b, i: (row(b, i), off // CB))
    prev_blk = lambda b, i: (b * nb + jnp.maximum(bpt * i - 1, 0), OFF_KA // CB)
    next_blk = lambda b, i: (b * nb + jnp.minimum(bpt * (i + 1), nb - 1), OFF_KA // CB)
    const2 = lambda b, i: (0, 0)
    const3 = lambda b, i: (0, 0, 0)
    smem = pl.BlockSpec(memory_space=pltpu.SMEM)
    state_shape = (N_PAIRS, LANES, 2 * RET_VAL_DIM)
    lane_tab = (N_PAIRS, CHUNK, LANES)
    return pl.pallas_call(
        _mixer_kernel,
        grid=(batch, nt),
        in_specs=[
            smem, smem,
            pl.BlockSpec((tq, ATTN_WIDTH), lambda b, i: (row(b, i), OFF_QA // ATTN_WIDTH)),
            pl.BlockSpec((BLOCK, CB), prev_blk),
            pl.BlockSpec((tq, CB), col(OFF_KA)),
            pl.BlockSpec((BLOCK, CB), next_blk),
            pl.BlockSpec((ATTN_HEADS, KW, BLOCK), const3),
            pl.BlockSpec((KW, BLOCK), const2),
            pl.BlockSpec((ATTN_KV_HEADS, 1, ATTN_GROUP * BLOCK), const3),
            pl.BlockSpec((tq, CB), col(OFF_GA)),
            pl.BlockSpec((tq, CB), col(OFF_GA + CB)),
            pl.BlockSpec((tq, CB), col(OFF_QR)),
            pl.BlockSpec((tq, CB), col(OFF_KR)),
            pl.BlockSpec((tq, CB), col(OFF_VR)),
            pl.BlockSpec((tq, CB), col(OFF_VR + CB)),
            pl.BlockSpec((tq, CB), col(OFF_GR)),
            pl.BlockSpec((tq, CB), col(OFF_GR + CB)),
            pl.BlockSpec((bpt, N_PAIRS, PB_ROWS, 2 * RET_VAL_DIM), lambda b, i: (row(b, i), 0, 0, 0)),
            pl.BlockSpec((1, RET_WIDTH), const2),
            pl.BlockSpec((tq, D_MODEL), lambda b, i: (row(b, i), 0)),
            pl.BlockSpec((1, ATTN_WIDTH), const2),
            pl.BlockSpec(memory_space=pl.ANY),
            pl.BlockSpec((1, D_MODEL), const2),
        ],
        out_specs=pl.BlockSpec((tq, D_MODEL), lambda b, i: (row(b, i), 0)),
        out_shape=jax.ShapeDtypeStruct((batch * seq, D_MODEL), F32),
        scratch_shapes=[pltpu.VMEM((ATTN_WIDTH + RET_WIDTH, D_MODEL), BF16),
                        pltpu.VMEM((2, ATTN_WIDTH + RET_WIDTH, WB), F32),
                        pltpu.SemaphoreType.DMA((2,)),
                        pltpu.VMEM((tq, ATTN_WIDTH), BF16),
                        pltpu.VMEM((tq, RET_WIDTH), BF16),
                        pltpu.VMEM(state_shape, F32),
                        pltpu.VMEM((RET_HEADS, CHUNK, CHUNK), F32),
                        pltpu.VMEM(lane_tab, F32),
                        pltpu.VMEM(lane_tab, F32),
                        pltpu.VMEM(lane_tab, F32),
                        pltpu.VMEM(state_shape, F32)],
        compiler_params=pltpu.CompilerParams(
            dimension_semantics=("arbitrary", "arbitrary"),
            vmem_limit_bytes=VMEM_LIMIT),
        name="mixer",
    )(lg_f, lg_b, proj, proj, proj, proj, bias_t, cap_t, sink_t, proj, proj, proj, proj, proj, proj,
      proj, proj, pb, gn_w, x2, attn_nw, w_o_p, final_w)


def _t5_bucket(rel):
    nb = N_BUCKETS // 2
    max_exact = nb // 2
    ret = jnp.where(rel > 0, nb, 0)
    n = jnp.abs(rel)
    nf = jnp.maximum(n, 1).astype(F32)
    large = max_exact + (jnp.log(nf / max_exact) / math.log(MAX_DISTANCE / max_exact)
                         * (nb - max_exact)).astype(jnp.int32)
    large = jnp.minimum(large, nb - 1)
    return ret + jnp.where(n < max_exact, n, large)


def kernel(x, norm_w, w_in, attn_sink, rel_bias, attn_out_norm_w, ret_decay_fwd,
           ret_decay_bwd, ret_gn_w, w_out, final_norm_w):
    batch, seq, _ = x.shape
    assert norm_w.shape[0] == 1 and seq % BLOCK == 0
    x2 = x.reshape(batch * seq, D_MODEL)

    w_o_p = w_out[0]

    qi = jnp.arange(BLOCK, dtype=jnp.int32)[None, :]
    kt = jnp.arange(KW, dtype=jnp.int32)[:, None]
    rel = kt - BLOCK - qi
    bucket = _t5_bucket(rel) & (N_BUCKETS - 1)
    onehot = (bucket[None] == jnp.arange(N_BUCKETS)[:, None, None]).astype(F32)
    rb = rel_bias.astype(F32).reshape(N_BUCKETS, ATTN_KV_HEADS, 2, 2)
    rb = jnp.swapaxes(rb, 2, 3).reshape(N_BUCKETS, ATTN_HEADS)
    bias_t = jnp.einsum("nh,nkq->hkq", rb, onehot,
                        precision=lax.Precision.HIGHEST)
    bias_t = bias_t * LOG2E
    cap_t = jnp.where(jnp.abs(rel) <= WINDOW, BIG, NEG).astype(F32)
    sink = (attn_sink[0].astype(F32) * LOG2E).reshape(ATTN_KV_HEADS, 1, 2, 2)
    sink = jnp.swapaxes(sink, 2, 3)[..., None]
    sink_t = jnp.broadcast_to(sink, (ATTN_KV_HEADS, 1, 2, 2, BLOCK)).reshape(
        ATTN_KV_HEADS, 1, ATTN_GROUP * BLOCK)

    inv = ROPE_BASE ** (-jnp.arange(0, RET_KEY_DIM, 2, dtype=F32) / RET_KEY_DIM)
    inv_full = jnp.tile(jnp.repeat(inv, 2), LANES // RET_KEY_DIM)[None, :]
    sgn = np.tile(np.array([-1.0, 1.0], np.float32), LANES // 2)[None, :]

    def trig_table(pos):
        ang = pos[:, None] * inv_full
        return jnp.stack([jnp.cos(ang), jnp.sin(ang), jnp.sin(ang) * sgn])

    rowtab = trig_table(jnp.arange(2 * BLOCK, dtype=F32))
    basetab = jnp.swapaxes(trig_table(jnp.arange(0, seq, BLOCK, dtype=F32)), 0, 1)

    lg_f = jax.nn.log_sigmoid(ret_decay_fwd[0].astype(F32))
    lg_b = jax.nn.log_sigmoid(ret_decay_bwd[0].astype(F32))

    proj, pb = _inproj(x2, norm_w[0].reshape(1, D_MODEL), w_in[0], rowtab, basetab, lg_b, seq)
    out = _mixer(proj, pb, x2, bias_t, cap_t, sink_t, lg_f, lg_b,
                 ret_gn_w[0].reshape(1, RET_WIDTH), attn_out_norm_w[0].reshape(1, ATTN_WIDTH),
                 w_o_p, final_norm_w.reshape(1, D_MODEL), batch, seq)
    return out.reshape(batch, seq, D_MODEL)
```

```python
import functools
import math

import numpy as np
import jax
import jax.numpy as jnp
from jax import lax
from jax.experimental import pallas as pl
from jax.experimental.pallas import tpu as pltpu

D_MODEL = 2048
ATTN_HEAD_DIM = 64
ATTN_WIDTH = 1024
ATTN_HEADS = 16
ATTN_KV_HEADS = 4
ATTN_GROUP = 4
ATTN_KV_COLS = 256
WINDOW = 128
BLOCK = 128
N_BUCKETS = 32
MAX_DISTANCE = 128
RET_WIDTH = 1024
RET_HEADS = 8
RET_VAL_DIM = 128
RET_KEY_DIM = 64
RET_QK_COLS = 512
CHUNK = 128
ROPE_BASE = 10000.0
EPS = 1e-6
NEG = -1e30
BIG = 3e38
LOG2E = math.log2(math.e)

LANES = 128
N_PAIRS = RET_HEADS // 2
KW = 3 * BLOCK
CB = 512

OFF_QA = 0
OFF_KA = OFF_QA + ATTN_WIDTH
OFF_GA = OFF_KA + 2 * ATTN_KV_COLS
OFF_QR = OFF_GA + ATTN_WIDTH
OFF_KR = OFF_QR + RET_QK_COLS
OFF_VR = OFF_KR + RET_QK_COLS
OFF_GR = OFF_VR + RET_WIDTH
IN_WIDTH = OFF_GR + RET_WIDTH

VMEM_LIMIT = 56 * 1024 * 1024

F32 = jnp.float32
BF16 = jnp.bfloat16


def _roll_half(x):
    return pltpu.roll(x.astype(F32), LANES // 2, axis=1).astype(x.dtype)


PB_ROWS = RET_KEY_DIM


def _first_step():
    return jnp.logical_and(pl.program_id(0) == 0, pl.program_id(1) == 0)


def _lane_decay(lg_ref, c, expo):
    lane = lax.broadcasted_iota(jnp.int32, (CHUNK, LANES), 1)
    lg = jnp.where(lane < RET_KEY_DIM, lg_ref[2 * c], lg_ref[2 * c + 1])
    return jnp.exp(lg * expo)


def _state_mask():
    r = lax.broadcasted_iota(jnp.int32, (LANES, 2 * RET_VAL_DIM), 0)
    m = lax.broadcasted_iota(jnp.int32, (LANES, 2 * RET_VAL_DIM), 1)
    return (r // RET_KEY_DIM) == (m // RET_VAL_DIM)


def _chunk_decay(lg_ref, c):
    r = lax.broadcasted_iota(jnp.int32, (LANES, 2 * RET_VAL_DIM), 0)
    lg = jnp.where(r < RET_KEY_DIM, lg_ref[2 * c], lg_ref[2 * c + 1])
    return jnp.where(_state_mask(), jnp.exp(lg * float(CHUNK)), 0.0)


def _row_index():
    return lax.broadcasted_iota(jnp.int32, (CHUNK, LANES), 0).astype(F32)


def _state_update(st_ref, c, k_rot, kdec, gdec, vpair):
    kd = (k_rot * kdec).astype(BF16)
    upd = lax.dot_general(kd, vpair, (((0,), (0,)), ((), ())),
                          preferred_element_type=F32)
    st_ref[c] = st_ref[c] * gdec + jnp.where(_state_mask(), upd, 0.0)


def _pair_cols(refs, rows, c, width):
    per_ref = CB // width
    return refs[c // per_ref][rows, width * (c % per_ref):width * (c % per_ref + 1)]


def _silu(g):
    return g / (1.0 + jnp.exp(-g))


def _col_scale(col):
    if OFF_QA <= col < OFF_QA + ATTN_WIDTH:
        return ATTN_HEAD_DIM ** -0.5 * LOG2E
    if OFF_KR <= col < OFF_KR + RET_QK_COLS:
        return RET_KEY_DIM ** -0.5
    return 1.0


WB = 256
W_SLOTS = 3


def _load_weight(whbm_ref, w_ref, stage_ref, sem_ref, col_scale=lambda col: 1.0):
    n_blk = w_ref.shape[1] // WB
    slots = stage_ref.shape[0]
    rows_per_iter = 256

    def copy(c):
        return pltpu.make_async_copy(whbm_ref.at[:, pl.ds(c * WB, WB)],
                                     stage_ref.at[c % slots], sem_ref.at[c % slots])

    for c in range(slots - 1):
        copy(c).start()
    for c in range(n_blk):
        if c + slots - 1 < n_blk:
            copy(c + slots - 1).start()
        copy(c).wait()
        scale = col_scale(c * WB)

        def body(r, carry, c=c, scale=scale):
            rs = pl.ds(pl.multiple_of(r * rows_per_iter, rows_per_iter), rows_per_iter)
            w = stage_ref[c % slots, rs, :]
            if scale != 1.0:
                w = w * scale
            w_ref[rs, c * WB:(c + 1) * WB] = w.astype(BF16)
            return carry

        lax.fori_loop(0, w_ref.shape[0] // rows_per_iter, body, 0)


def _inproj_kernel(lgb_ref, x_ref, nw_ref, whbm_ref, rowtab_ref, basetab_ref, o_ref, pb_ref,
                   w_ref, stage_ref, sem_ref, st_ref, kdb_ref, gb_ref, *, row_splits,
                   tiles_per_seq):
    tm = x_ref.shape[0]
    nw = nw_ref[...]
    step = pl.program_id(0)
    tile_blk = (tiles_per_seq - 1 - step % tiles_per_seq) * (tm // BLOCK)

    @pl.when(step == 0)
    def _():
        _load_weight(whbm_ref, w_ref, stage_ref, sem_ref, _col_scale)
        idx = _row_index()
        for c in range(N_PAIRS):
            kdb_ref[c] = _lane_decay(lgb_ref, c, idx)
            gb_ref[c] = _chunk_decay(lgb_ref, c)

    @pl.when(step % tiles_per_seq == 0)
    def _():
        st_ref[...] = jnp.zeros_like(st_ref)

    start = 0
    for rc in row_splits:
        rows = slice(start, start + rc)
        x = x_ref[rows, :]
        ms = jnp.mean(x * x, axis=-1, keepdims=True)
        h = (x * lax.rsqrt(ms + EPS) * nw).astype(BF16)
        ca, sa, sas = rowtab_ref[0, :rc, :], rowtab_ref[1, :rc, :], rowtab_ref[2, :rc, :]
        base = basetab_ref[tile_blk + start // BLOCK]
        cb, sb, sbs = base[0:1, :], base[1:2, :], base[2:3, :]
        cos = ca * cb - sa * sb
        sin_signed = sas * cb + ca * sbs
        lane = lax.broadcasted_iota(jnp.int32, (rc, LANES), 1)
        even = (lane % 2) == 0
        start += rc
        for col in range(0, IN_WIDTH, CB):
            res = jnp.dot(h, w_ref[:, col:col + CB], preferred_element_type=F32)
            if OFF_GA <= col < OFF_QR or col >= OFF_GR:
                res = _silu(res)
            elif OFF_QR <= col < OFF_VR:
                parts = []
                for c in range(CB // LANES):
                    v = res[:, LANES * c:LANES * (c + 1)]
                    partner = jnp.where(even, pltpu.roll(v, LANES - 1, axis=1),
                                        pltpu.roll(v, 1, axis=1))
                    parts.append(v * cos + partner * sin_signed)
                res = jnp.concatenate(parts, axis=1)
            o_ref[rows, col:col + CB] = res.astype(o_ref.dtype)

    for ch in reversed(range(tm // CHUNK)):
        rows = slice(CHUNK * ch, CHUNK * (ch + 1))
        for c in range(N_PAIRS):
            st = st_ref[c]
            pb_ref[ch, c] = (st[:PB_ROWS] + st[PB_ROWS:]).astype(BF16)
            k = o_ref[rows, OFF_KR + LANES * c:OFF_KR + LANES * (c + 1)].astype(F32)
            vpair = o_ref[rows, OFF_VR + 2 * RET_VAL_DIM * c:OFF_VR + 2 * RET_VAL_DIM * (c + 1)]
            _state_update(st_ref, c, k, kdb_ref[c], gb_ref[c], vpair)


def _inproj(x2, norm_w, w_in, rowtab, basetab, lg_b, seq, tm=512, row_splits=(256, 256)):
    m = x2.shape[0]
    assert sum(row_splits) == tm and max(row_splits) <= rowtab.shape[1]
    tps = seq // tm
    cpt = tm // CHUNK
    tile = lambda i: (i // tps) * tps + (tps - 1 - i % tps)
    return pl.pallas_call(
        functools.partial(_inproj_kernel, row_splits=row_splits, tiles_per_seq=tps),
        grid=(m // tm,),
        in_specs=[
            pl.BlockSpec(memory_space=pltpu.SMEM),
            pl.BlockSpec((tm, D_MODEL), lambda i: (tile(i), 0)),
            pl.BlockSpec((1, D_MODEL), lambda i: (0, 0)),
            pl.BlockSpec(memory_space=pl.ANY),
            pl.BlockSpec(rowtab.shape, lambda i: (0, 0, 0)),
            pl.BlockSpec(basetab.shape, lambda i: (0, 0, 0)),
        ],
        out_specs=[
            pl.BlockSpec((tm, IN_WIDTH), lambda i: (tile(i), 0)),
            pl.BlockSpec((cpt, N_PAIRS, PB_ROWS, 2 * RET_VAL_DIM), lambda i: (tile(i), 0, 0, 0)),
        ],
        out_shape=[
            jax.ShapeDtypeStruct((m, IN_WIDTH), BF16),
            jax.ShapeDtypeStruct((m // CHUNK, N_PAIRS, PB_ROWS, 2 * RET_VAL_DIM), BF16),
        ],
        scratch_shapes=[pltpu.VMEM((D_MODEL, IN_WIDTH), BF16),
                        pltpu.VMEM((W_SLOTS, D_MODEL, WB), F32),
                        pltpu.SemaphoreType.DMA((W_SLOTS,)),
                        pltpu.VMEM((N_PAIRS, LANES, 2 * RET_VAL_DIM), F32),
                        pltpu.VMEM((N_PAIRS, CHUNK, LANES), F32),
                        pltpu.VMEM((N_PAIRS, LANES, 2 * RET_VAL_DIM), F32)],
        compiler_params=pltpu.CompilerParams(
            dimension_semantics=("arbitrary",),
            vmem_limit_bytes=VMEM_LIMIT),
        name="inproj",
    )(lg_b, x2, norm_w, w_in, rowtab, basetab)


GROUP_ROWS = 2 * BLOCK
OUT_AFTER = 1


def _attention_stage(i, nt, q_ref, kvp_ref, kvc_ref, kvn_ref, bias_ref, cap_ref, sink_ref, a_scr):
    n_blk = q_ref.shape[0] // BLOCK
    hd = ATTN_HEAD_DIM
    kv = jnp.concatenate([kvp_ref[...], kvc_ref[...], kvn_ref[...]], axis=0)
    lane = lax.broadcasted_iota(jnp.int32, (kv.shape[0], LANES), 1)
    low = lane < hd

    tile4 = lambda c: jnp.concatenate([c] * ATTN_GROUP, axis=1)
    cap_l, cap_r = tile4(cap_ref[:BLOCK]), tile4(cap_ref[2 * BLOCK:])
    cap_first = jnp.where(i > 0, cap_l, NEG)
    cap_last = jnp.where(i < nt - 1, cap_r, NEG)

    k_side, v_ones = {}, {}
    for j in range(ATTN_KV_COLS // LANES):
        kj = kv[:, LANES * j:LANES * (j + 1)]
        vj = kv[:, ATTN_KV_COLS + LANES * j:ATTN_KV_COLS + LANES * (j + 1)]
        for hg in range(2):
            mine = low if hg == 0 else jnp.logical_not(low)
            k_here = jnp.where(mine, kj, jnp.zeros_like(kj))
            k_side[(2 * j + hg, hg)] = k_here
            k_side[(2 * j + hg, 1 - hg)] = _roll_half(k_here)
            v_ones[2 * j + hg] = jnp.where(mine, vj, jnp.ones_like(vj))

    def scores(item):
        blk, g = item
        win = slice(BLOCK * blk, BLOCK * blk + KW)
        qs = jnp.concatenate(
            [q_ref[BLOCK * blk:BLOCK * (blk + 1), LANES * c:LANES * (c + 1)]
             for c in (2 * g, 2 * g + 1)], axis=0)
        return [lax.dot_general(k_side[(g, half)][win], qs, (((1,), (1,)), ((), ())),
                                preferred_element_type=F32) for half in range(2)]

    def softmax(item, st_pair):
        blk, g = item
        c_l = cap_first if blk == 0 else cap_l
        c_r = cap_last if blk == n_blk - 1 else cap_r
        st = jnp.concatenate(st_pair, axis=1)
        b = jnp.concatenate([bias_ref[ATTN_GROUP * g + t] for t in range(ATTN_GROUP)], axis=1)
        t = jnp.concatenate([
            jnp.minimum(st[:BLOCK] + b[:BLOCK], c_l),
            st[BLOCK:2 * BLOCK] + b[BLOCK:2 * BLOCK],
            jnp.minimum(st[2 * BLOCK:] + b[2 * BLOCK:], c_r),
        ], axis=0)
        sk = sink_ref[g]
        m = jnp.maximum(jnp.max(t, axis=0, keepdims=True), sk)
        return jnp.exp2(t - m).astype(BF16), jnp.exp2(sk - m)

    def finish(item, probs):
        blk, g = item
        p, e_sink = probs
        win = slice(BLOCK * blk, BLOCK * blk + KW)
        ot = lax.dot_general(v_ones[g][win], p, (((0,), (0,)), ((), ())),
                             preferred_element_type=F32)
        hg = g % 2
        num = ot[hd * hg:hd * (hg + 1)]
        den = ot[hd * (1 - hg):hd * (1 - hg) + 1]
        out = num * (1.0 / (den + e_sink))
        out_t = jnp.concatenate([out[:, :2 * BLOCK], out[:, 2 * BLOCK:]], axis=0)
        for r in range(2):
            c = 2 * g + r
            a_scr[BLOCK * blk:BLOCK * (blk + 1), LANES * c:LANES * (c + 1)] = (
                out_t[:, BLOCK * r:BLOCK * (r + 1)].T.astype(a_scr.dtype))

    return scores, softmax, finish


def _retention_stage(q_ref, k_ref, v_refs, g_refs, pb_ref, gnw_ref, st_ref, d_ref, qdf_ref,
                     qdb_ref, kdf_ref, gf_ref, r_scr):
    lane = lax.broadcasted_iota(jnp.int32, (CHUNK, LANES), 1)
    low_half = lane < RET_KEY_DIM
    first_v = lax.broadcasted_iota(jnp.int32, (CHUNK, 2 * RET_VAL_DIM), 1) < RET_VAL_DIM
    first_s = lax.broadcasted_iota(jnp.int32, (PB_ROWS, 2 * RET_VAL_DIM), 1) < RET_VAL_DIM

    def scores(item):
        ch, c = item
        rows = slice(CHUNK * ch, CHUNK * (ch + 1))
        kb = k_ref[rows, LANES * c:LANES * (c + 1)]
        q = q_ref[rows, LANES * c:LANES * (c + 1)]
        out = []
        for half in range(2):
            sel = low_half if half == 0 else jnp.logical_not(low_half)
            qh = jnp.where(sel, q, jnp.zeros_like(q))
            out.append(lax.dot_general(qh, kb, (((1,), (1,)), ((), ())),
                                       preferred_element_type=F32))
        return out

    def decay(item, s_pair):
        ch, c = item
        rows = slice(CHUNK * ch, CHUNK * (ch + 1))
        q = q_ref[rows, LANES * c:LANES * (c + 1)].astype(F32)
        parts = [(s_pair[half] * d_ref[2 * c + half]).astype(BF16) for half in range(2)]
        parts.append((q * qdf_ref[c]).astype(BF16))
        parts.append((q * qdb_ref[c]).astype(BF16))
        return jnp.concatenate(parts, axis=1)

    def finish(item, lhs):
        ch, c = item
        rows = slice(CHUNK * ch, CHUNK * (ch + 1))
        k = k_ref[rows, LANES * c:LANES * (c + 1)].astype(F32)
        vpair = _pair_cols(v_refs, rows, c, 2 * RET_VAL_DIM)
        zero_v = jnp.zeros_like(vpair)
        pb = pb_ref[ch, c]
        zero_s = jnp.zeros_like(pb)
        rhs = jnp.concatenate([
            jnp.where(first_v, vpair, zero_v),
            jnp.where(first_v, zero_v, vpair),
            st_ref[c].astype(BF16),
            jnp.where(first_s, pb, zero_s),
            jnp.where(first_s, zero_s, pb),
        ], axis=0)
        o = jnp.dot(lhs, rhs, preferred_element_type=F32)
        _state_update(st_ref, c, k, kdf_ref[c], gf_ref[c], vpair)
        for half in range(2):
            head = 2 * c + half
            lo_c = RET_VAL_DIM * head
            oh = o[:, RET_VAL_DIM * half:RET_VAL_DIM * (half + 1)]
            mu = jnp.mean(oh, axis=-1, keepdims=True)
            dev = oh - mu
            var = jnp.mean(dev * dev, axis=-1, keepdims=True)
            y = dev * lax.rsqrt(var + EPS) * gnw_ref[:, lo_c:lo_c + RET_VAL_DIM]
            g = _pair_cols(g_refs, rows, head, RET_VAL_DIM).astype(F32)
            r_scr[rows, lo_c:lo_c + RET_VAL_DIM] = (y * g).astype(r_scr.dtype)

    return scores, decay, finish


def _mixer_kernel(lgf_ref, lgb_ref, qa_ref, kvp_ref, kvc_ref, kvn_ref, bias_ref, cap_ref, sink_ref,
                  ga0_ref, ga1_ref, qr_ref, kr_ref, v0_ref, v1_ref, gr0_ref, gr1_ref, pb_ref,
                  gnw_ref, x_ref, anw_ref, whbm_ref, fw_ref, o_ref,
                  w_ref, stage_ref, sem_ref,
                  a_scr, r_scr, st_ref, d_ref, qdf_ref, qdb_ref, kdf_ref, gf_ref):
    @pl.when(_first_step())
    def _():
        _load_weight(whbm_ref, w_ref, stage_ref, sem_ref)
        idx = _row_index()
        col = lax.broadcasted_iota(jnp.int32, (CHUNK, CHUNK), 1).astype(F32)
        diff = idx - col
        for c in range(N_PAIRS):
            qdf_ref[c] = _lane_decay(lgf_ref, c, idx + 1.0)
            qdb_ref[c] = _lane_decay(lgb_ref, c, float(CHUNK) - idx)
            kdf_ref[c] = _lane_decay(lgf_ref, c, float(CHUNK - 1) - idx)
            gf_ref[c] = _chunk_decay(lgf_ref, c)
        for hh in range(RET_HEADS):
            d_ref[hh] = jnp.exp(jnp.where(diff >= 0.0, lgf_ref[hh] * diff, lgb_ref[hh] * (-diff)))

    @pl.when(pl.program_id(1) == 0)
    def _():
        st_ref[...] = jnp.zeros_like(st_ref)

    tq = qa_ref.shape[0]
    stage = {
        "a": _attention_stage(pl.program_id(1), pl.num_programs(1), qa_ref, kvp_ref, kvc_ref,
                              kvn_ref, bias_ref, cap_ref, sink_ref, a_scr),
        "r": _retention_stage(qr_ref, kr_ref, (v0_ref, v1_ref), (gr0_ref, gr1_ref), pb_ref,
                              gnw_ref, st_ref, d_ref, qdf_ref, qdb_ref, kdf_ref, gf_ref, r_scr),
    }

    n_grp = tq // GROUP_ROWS

    def out_group(grp):
        rows = slice(GROUP_ROWS * grp, GROUP_ROWS * (grp + 1))
        a = a_scr[rows, :].astype(F32)
        ms = jnp.mean(a * a, axis=-1, keepdims=True)
        g = jnp.concatenate([ga0_ref[rows, :], ga1_ref[rows, :]], axis=1).astype(F32)
        gated = (a * lax.rsqrt(ms + EPS) * anw_ref[...] * g).astype(BF16)
        y = (x_ref[rows, :]
             + jnp.dot(r_scr[rows, :], w_ref[ATTN_WIDTH:, :], preferred_element_type=F32)
             + jnp.dot(gated, w_ref[:ATTN_WIDTH, :], preferred_element_type=F32))
        ms = jnp.mean(y * y, axis=-1, keepdims=True)
        o_ref[rows, :] = y * lax.rsqrt(ms + EPS) * fw_ref[...]

    per = GROUP_ROWS // BLOCK
    items = []
    for grp in range(n_grp):
        for blk in range(per * grp, per * (grp + 1)):
            for u in range(ATTN_KV_HEADS):
                items.append(("a", (blk, u), grp))
                items.append(("r", (blk, u), grp))
    per_group = len(items) // n_grp
    pending = stage[items[0][0]][0](items[0][1])
    closing = None
    for idx, (kind, item, grp) in enumerate(items):
        cur = pending
        if idx + 1 < len(items):
            nxt = items[idx + 1]
            pending = stage[nxt[0]][0](nxt[1])
        mid = stage[kind][1](item, cur)
        if closing is not None:
            stage[closing[0]][2](closing[1], closing[2])
        closing = (kind, item, mid)
        if grp > 0 and idx % per_group == OUT_AFTER:
            out_group(grp - 1)
    stage[closing[0]][2](closing[1], closing[2])
    out_group(n_grp - 1)


def _mixer(proj, pb, x2, bias_t, cap_t, sink_t, lg_f, lg_b, gn_w, attn_nw, w_o_p, final_w,
           batch, seq, tq=512):
    nt = seq // tq
    bpt = tq // BLOCK
    nb = seq // BLOCK
    row = lambda b, i: b * nt + i
    col = lambda off: (lambda b, i: (row(b, i), off // CB))
    prev_blk = lambda b, i: (b * nb + jnp.maximum(bpt * i - 1, 0), OFF_KA // CB)
    next_blk = lambda b, i: (b * nb + jnp.minimum(bpt * (i + 1), nb - 1), OFF_KA // CB)
    const2 = lambda b, i: (0, 0)
    const3 = lambda b, i: (0, 0, 0)
    smem = pl.BlockSpec(memory_space=pltpu.SMEM)
    state_shape = (N_PAIRS, LANES, 2 * RET_VAL_DIM)
    lane_tab = (N_PAIRS, CHUNK, LANES)
    return pl.pallas_call(
        _mixer_kernel,
        grid=(batch, nt),
        in_specs=[
            smem, smem,
            pl.BlockSpec((tq, ATTN_WIDTH), lambda b, i: (row(b, i), OFF_QA // ATTN_WIDTH)),
            pl.BlockSpec((BLOCK, CB), prev_blk),
            pl.BlockSpec((tq, CB), col(OFF_KA)),
            pl.BlockSpec((BLOCK, CB), next_blk),
            pl.BlockSpec((ATTN_HEADS, KW, BLOCK), const3),
            pl.BlockSpec((KW, BLOCK), const2),
            pl.BlockSpec((ATTN_KV_HEADS, 1, ATTN_GROUP * BLOCK), const3),
            pl.BlockSpec((tq, CB), col(OFF_GA)),
            pl.BlockSpec((tq, CB), col(OFF_GA + CB)),
            pl.BlockSpec((tq, CB), col(OFF_QR)),
            pl.BlockSpec((tq, CB), col(OFF_KR)),
            pl.BlockSpec((tq, CB), col(OFF_VR)),
            pl.BlockSpec((tq, CB), col(OFF_VR + CB)),
            pl.BlockSpec((tq, CB), col(OFF_GR)),
            pl.BlockSpec((tq, CB), col(OFF_GR + CB)),
            pl.BlockSpec((bpt, N_PAIRS, PB_ROWS, 2 * RET_VAL_DIM), lambda b, i: (row(b, i), 0, 0, 0)),
            pl.BlockSpec((1, RET_WIDTH), const2),
            pl.BlockSpec((tq, D_MODEL), lambda b, i: (row(b, i), 0)),
            pl.BlockSpec((1, ATTN_WIDTH), const2),
            pl.BlockSpec(memory_space=pl.ANY),
            pl.BlockSpec((1, D_MODEL), const2),
        ],
        out_specs=pl.BlockSpec((tq, D_MODEL), lambda b, i: (row(b, i), 0)),
        out_shape=jax.ShapeDtypeStruct((batch * seq, D_MODEL), F32),
        scratch_shapes=[pltpu.VMEM((ATTN_WIDTH + RET_WIDTH, D_MODEL), BF16),
                        pltpu.VMEM((W_SLOTS, ATTN_WIDTH + RET_WIDTH, WB), F32),
                        pltpu.SemaphoreType.DMA((W_SLOTS,)),
                        pltpu.VMEM((tq, ATTN_WIDTH), BF16),
                        pltpu.VMEM((tq, RET_WIDTH), BF16),
                        pltpu.VMEM(state_shape, F32),
                        pltpu.VMEM((RET_HEADS, CHUNK, CHUNK), F32),
                        pltpu.VMEM(lane_tab, F32),
                        pltpu.VMEM(lane_tab, F32),
                        pltpu.VMEM(lane_tab, F32),
                        pltpu.VMEM(state_shape, F32)],
        compiler_params=pltpu.CompilerParams(
            dimension_semantics=("arbitrary", "arbitrary"),
            vmem_limit_bytes=VMEM_LIMIT),
        name="mixer",
    )(lg_f, lg_b, proj, proj, proj, proj, bias_t, cap_t, sink_t, proj, proj, proj, proj, proj, proj,
      proj, proj, pb, gn_w, x2, attn_nw, w_o_p, final_w)


def _t5_bucket(rel):
    nb = N_BUCKETS // 2
    max_exact = nb // 2
    ret = jnp.where(rel > 0, nb, 0)
    n = jnp.abs(rel)
    nf = jnp.maximum(n, 1).astype(F32)
    large = max_exact + (jnp.log(nf / max_exact) / math.log(MAX_DISTANCE / max_exact)
                         * (nb - max_exact)).astype(jnp.int32)
    large = jnp.minimum(large, nb - 1)
    return ret + jnp.where(n < max_exact, n, large)


def kernel(x, norm_w, w_in, attn_sink, rel_bias, attn_out_norm_w, ret_decay_fwd,
           ret_decay_bwd, ret_gn_w, w_out, final_norm_w):
    batch, seq, _ = x.shape
    assert norm_w.shape[0] == 1 and seq % BLOCK == 0
    x2 = x.reshape(batch * seq, D_MODEL)

    w_o_p = w_out[0]

    qi = jnp.arange(BLOCK, dtype=jnp.int32)[None, :]
    kt = jnp.arange(KW, dtype=jnp.int32)[:, None]
    rel = kt - BLOCK - qi
    bucket = _t5_bucket(rel) & (N_BUCKETS - 1)
    onehot = (bucket[None] == jnp.arange(N_BUCKETS)[:, None, None]).astype(F32)
    rb = rel_bias.astype(F32).reshape(N_BUCKETS, ATTN_KV_HEADS, 2, 2)
    rb = jnp.swapaxes(rb, 2, 3).reshape(N_BUCKETS, ATTN_HEADS)
    bias_t = jnp.einsum("nh,nkq->hkq", rb, onehot,
                        precision=lax.Precision.HIGHEST)
    bias_t = bias_t * LOG2E
    cap_t = jnp.where(jnp.abs(rel) <= WINDOW, BIG, NEG).astype(F32)
    sink = (attn_sink[0].astype(F32) * LOG2E).reshape(ATTN_KV_HEADS, 1, 2, 2)
    sink = jnp.swapaxes(sink, 2, 3)[..., None]
    sink_t = jnp.broadcast_to(sink, (ATTN_KV_HEADS, 1, 2, 2, BLOCK)).reshape(
        ATTN_KV_HEADS, 1, ATTN_GROUP * BLOCK)

    inv = ROPE_BASE ** (-jnp.arange(0, RET_KEY_DIM, 2, dtype=F32) / RET_KEY_DIM)
    inv_full = jnp.tile(jnp.repeat(inv, 2), LANES // RET_KEY_DIM)[None, :]
    sgn = np.tile(np.array([-1.0, 1.0], np.float32), LANES // 2)[None, :]

    def trig_table(pos):
        ang = pos[:, None] * inv_full
        return jnp.stack([jnp.cos(ang), jnp.sin(ang), jnp.sin(ang) * sgn])

    rowtab = trig_table(jnp.arange(2 * BLOCK, dtype=F32))
    basetab = jnp.swapaxes(trig_table(jnp.arange(0, seq, BLOCK, dtype=F32)), 0, 1)

    lg_f = jax.nn.log_sigmoid(ret_decay_fwd[0].astype(F32))
    lg_b = jax.nn.log_sigmoid(ret_decay_bwd[0].astype(F32))

    proj, pb = _inproj(x2, norm_w[0].reshape(1, D_MODEL), w_in[0], rowtab, basetab, lg_b, seq)
    out = _mixer(proj, pb, x2, bias_t, cap_t, sink_t, lg_f, lg_b,
                 ret_gn_w[0].reshape(1, RET_WIDTH), attn_out_norm_w[0].reshape(1, ATTN_WIDTH),
                 w_o_p, final_norm_w.reshape(1, D_MODEL), batch, seq)
    return out.reshape(batch, seq, D_MODEL)
```

```python
import functools
import math

import numpy as np
import jax
import jax.numpy as jnp
from jax import lax
from jax.experimental import pallas as pl
from jax.experimental.pallas import tpu as pltpu

D_MODEL = 2048
ATTN_HEAD_DIM = 64
ATTN_WIDTH = 1024
ATTN_HEADS = 16
ATTN_KV_HEADS = 4
ATTN_GROUP = 4
ATTN_KV_COLS = 256
WINDOW = 128
BLOCK = 128
N_BUCKETS = 32
MAX_DISTANCE = 128
RET_WIDTH = 1024
RET_HEADS = 8
RET_VAL_DIM = 128
RET_KEY_DIM = 64
RET_QK_COLS = 512
CHUNK = 128
ROPE_BASE = 10000.0
EPS = 1e-6
NEG = -1e30
BIG = 3e38
LOG2E = math.log2(math.e)

LANES = 128
N_PAIRS = RET_HEADS // 2
KW = 3 * BLOCK
CB = 512

OFF_QA = 0
OFF_KA = OFF_QA + ATTN_WIDTH
OFF_GA = OFF_KA + 2 * ATTN_KV_COLS
OFF_QR = OFF_GA + ATTN_WIDTH
OFF_KR = OFF_QR + RET_QK_COLS
OFF_VR = OFF_KR + RET_QK_COLS
OFF_GR = OFF_VR + RET_WIDTH
IN_WIDTH = OFF_GR + RET_WIDTH

VMEM_LIMIT = 56 * 1024 * 1024

F32 = jnp.float32
BF16 = jnp.bfloat16


def _roll_half(x):
    return pltpu.roll(x.astype(F32), LANES // 2, axis=1).astype(x.dtype)


PB_ROWS = RET_KEY_DIM


def _first_step():
    return jnp.logical_and(pl.program_id(0) == 0, pl.program_id(1) == 0)


def _lane_decay(lg_ref, c, expo):
    lane = lax.broadcasted_iota(jnp.int32, (CHUNK, LANES), 1)
    lg = jnp.where(lane < RET_KEY_DIM, lg_ref[2 * c], lg_ref[2 * c + 1])
    return jnp.exp(lg * expo)


def _state_mask():
    r = lax.broadcasted_iota(jnp.int32, (LANES, 2 * RET_VAL_DIM), 0)
    m = lax.broadcasted_iota(jnp.int32, (LANES, 2 * RET_VAL_DIM), 1)
    return (r // RET_KEY_DIM) == (m // RET_VAL_DIM)


def _chunk_decay(lg_ref, c):
    r = lax.broadcasted_iota(jnp.int32, (LANES, 2 * RET_VAL_DIM), 0)
    lg = jnp.where(r < RET_KEY_DIM, lg_ref[2 * c], lg_ref[2 * c + 1])
    return jnp.where(_state_mask(), jnp.exp(lg * float(CHUNK)), 0.0)


def _row_index():
    return lax.broadcasted_iota(jnp.int32, (CHUNK, LANES), 0).astype(F32)


def _state_update(st_ref, c, k_rot, kdec, gdec, vpair):
    kd = (k_rot * kdec).astype(BF16)
    upd = lax.dot_general(kd, vpair, (((0,), (0,)), ((), ())),
                          preferred_element_type=F32)
    st_ref[c] = st_ref[c] * gdec + jnp.where(_state_mask(), upd, 0.0)


def _pair_cols(refs, rows, c, width):
    per_ref = CB // width
    return refs[c // per_ref][rows, width * (c % per_ref):width * (c % per_ref + 1)]


def _silu(g):
    return g / (1.0 + jnp.exp(-g))


def _col_scale(col):
    if OFF_QA <= col < OFF_QA + ATTN_WIDTH:
        return ATTN_HEAD_DIM ** -0.5 * LOG2E
    if OFF_KR <= col < OFF_KR + RET_QK_COLS:
        return RET_KEY_DIM ** -0.5
    return 1.0


WB = 256
W_SLOTS = 3


def _load_weight(whbm_ref, w_ref, stage_ref, sem_ref, col_scale=lambda col: 1.0):
    n_blk = w_ref.shape[1] // WB
    slots = stage_ref.shape[0]
    rows_per_iter = 256

    def copy(c):
        return pltpu.make_async_copy(whbm_ref.at[:, pl.ds(c * WB, WB)],
                                     stage_ref.at[c % slots], sem_ref.at[c % slots])

    for c in range(slots - 1):
        copy(c).start()
    for c in range(n_blk):
        if c + slots - 1 < n_blk:
            copy(c + slots - 1).start()
        copy(c).wait()
        scale = col_scale(c * WB)

        def body(r, carry, c=c, scale=scale):
            rs = pl.ds(pl.multiple_of(r * rows_per_iter, rows_per_iter), rows_per_iter)
            w = stage_ref[c % slots, rs, :]
            if scale != 1.0:
                w = w * scale
            w_ref[rs, c * WB:(c + 1) * WB] = w.astype(BF16)
            return carry

        lax.fori_loop(0, w_ref.shape[0] // rows_per_iter, body, 0)


def _inproj_kernel(lgb_ref, x_ref, nw_ref, whbm_ref, rowtab_ref, basetab_ref, o_ref, pb_ref,
                   w_ref, stage_ref, sem_ref, st_ref, kdb_ref, gb_ref, *, row_splits,
                   tiles_per_seq):
    tm = x_ref.shape[0]
    nw = nw_ref[...]
    step = pl.program_id(0)
    tile_blk = (tiles_per_seq - 1 - step % tiles_per_seq) * (tm // BLOCK)

    @pl.when(step == 0)
    def _():
        _load_weight(whbm_ref, w_ref, stage_ref, sem_ref, _col_scale)
        idx = _row_index()
        for c in range(N_PAIRS):
            kdb_ref[c] = _lane_decay(lgb_ref, c, idx)
            gb_ref[c] = _chunk_decay(lgb_ref, c)

    @pl.when(step % tiles_per_seq == 0)
    def _():
        st_ref[...] = jnp.zeros_like(st_ref)

    start = 0
    for rc in row_splits:
        rows = slice(start, start + rc)
        x = x_ref[rows, :]
        ms = jnp.mean(x * x, axis=-1, keepdims=True)
        h = (x * lax.rsqrt(ms + EPS) * nw).astype(BF16)
        ca, sa, sas = rowtab_ref[0, :rc, :], rowtab_ref[1, :rc, :], rowtab_ref[2, :rc, :]
        base = basetab_ref[tile_blk + start // BLOCK]
        cb, sb, sbs = base[0:1, :], base[1:2, :], base[2:3, :]
        cos = ca * cb - sa * sb
        sin_signed = sas * cb + ca * sbs
        lane = lax.broadcasted_iota(jnp.int32, (rc, LANES), 1)
        even = (lane % 2) == 0
        start += rc
        for col in range(0, IN_WIDTH, CB):
            res = jnp.dot(h, w_ref[:, col:col + CB], preferred_element_type=F32)
            if OFF_GA <= col < OFF_QR or col >= OFF_GR:
                res = _silu(res)
            elif OFF_QR <= col < OFF_VR:
                parts = []
                for c in range(CB // LANES):
                    v = res[:, LANES * c:LANES * (c + 1)]
                    partner = jnp.where(even, pltpu.roll(v, LANES - 1, axis=1),
                                        pltpu.roll(v, 1, axis=1))
                    parts.append(v * cos + partner * sin_signed)
                res = jnp.concatenate(parts, axis=1)
            o_ref[rows, col:col + CB] = res.astype(o_ref.dtype)

    for ch in reversed(range(tm // CHUNK)):
        rows = slice(CHUNK * ch, CHUNK * (ch + 1))
        for c in range(N_PAIRS):
            st = st_ref[c]
            pb_ref[ch, c] = (st[:PB_ROWS] + st[PB_ROWS:]).astype(BF16)
            k = o_ref[rows, OFF_KR + LANES * c:OFF_KR + LANES * (c + 1)].astype(F32)
            vpair = o_ref[rows, OFF_VR + 2 * RET_VAL_DIM * c:OFF_VR + 2 * RET_VAL_DIM * (c + 1)]
            _state_update(st_ref, c, k, kdb_ref[c], gb_ref[c], vpair)


def _inproj(x2, norm_w, w_in, rowtab, basetab, lg_b, seq, tm=512, row_splits=(256, 256)):
    m = x2.shape[0]
    assert sum(row_splits) == tm and max(row_splits) <= rowtab.shape[1]
    tps = seq // tm
    cpt = tm // CHUNK
    tile = lambda i: (i // tps) * tps + (tps - 1 - i % tps)
    return pl.pallas_call(
        functools.partial(_inproj_kernel, row_splits=row_splits, tiles_per_seq=tps),
        grid=(m // tm,),
        in_specs=[
            pl.BlockSpec(memory_space=pltpu.SMEM),
            pl.BlockSpec((tm, D_MODEL), lambda i: (tile(i), 0)),
            pl.BlockSpec((1, D_MODEL), lambda i: (0, 0)),
            pl.BlockSpec(memory_space=pl.ANY),
            pl.BlockSpec(rowtab.shape, lambda i: (0, 0, 0)),
            pl.BlockSpec(basetab.shape, lambda i: (0, 0, 0)),
        ],
        out_specs=[
            pl.BlockSpec((tm, IN_WIDTH), lambda i: (tile(i), 0)),
            pl.BlockSpec((cpt, N_PAIRS, PB_ROWS, 2 * RET_VAL_DIM), lambda i: (tile(i), 0, 0, 0)),
        ],
        out_shape=[
            jax.ShapeDtypeStruct((m, IN_WIDTH), BF16),
            jax.ShapeDtypeStruct((m // CHUNK, N_PAIRS, PB_ROWS, 2 * RET_VAL_DIM), BF16),
        ],
        scratch_shapes=[pltpu.VMEM((D_MODEL, IN_WIDTH), BF16),
                        pltpu.VMEM((W_SLOTS + 1, D_MODEL, WB), F32),
                        pltpu.SemaphoreType.DMA((W_SLOTS + 1,)),
                        pltpu.VMEM((N_PAIRS, LANES, 2 * RET_VAL_DIM), F32),
                        pltpu.VMEM((N_PAIRS, CHUNK, LANES), F32),
                        pltpu.VMEM((N_PAIRS, LANES, 2 * RET_VAL_DIM), F32)],
        compiler_params=pltpu.CompilerParams(
            dimension_semantics=("arbitrary",),
            vmem_limit_bytes=VMEM_LIMIT),
        name="inproj",
    )(lg_b, x2, norm_w, w_in, rowtab, basetab)


GROUP_ROWS = 2 * BLOCK
OUT_AFTER = 1


def _attention_stage(i, nt, q_ref, kvp_ref, kvc_ref, kvn_ref, bias_ref, cap_ref, sink_ref, a_scr):
    n_blk = q_ref.shape[0] // BLOCK
    hd = ATTN_HEAD_DIM
    kv = jnp.concatenate([kvp_ref[...], kvc_ref[...], kvn_ref[...]], axis=0)
    lane = lax.broadcasted_iota(jnp.int32, (kv.shape[0], LANES), 1)
    low = lane < hd

    tile4 = lambda c: jnp.concatenate([c] * ATTN_GROUP, axis=1)
    cap_l, cap_r = tile4(cap_ref[:BLOCK]), tile4(cap_ref[2 * BLOCK:])
    cap_first = jnp.where(i > 0, cap_l, NEG)
    cap_last = jnp.where(i < nt - 1, cap_r, NEG)

    k_side, v_ones = {}, {}
    for j in range(ATTN_KV_COLS // LANES):
        kj = kv[:, LANES * j:LANES * (j + 1)]
        vj = kv[:, ATTN_KV_COLS + LANES * j:ATTN_KV_COLS + LANES * (j + 1)]
        for hg in range(2):
            mine = low if hg == 0 else jnp.logical_not(low)
            k_here = jnp.where(mine, kj, jnp.zeros_like(kj))
            k_side[(2 * j + hg, hg)] = k_here
            k_side[(2 * j + hg, 1 - hg)] = _roll_half(k_here)
            v_ones[2 * j + hg] = jnp.where(mine, vj, jnp.ones_like(vj))

    def scores(item):
        blk, g = item
        win = slice(BLOCK * blk, BLOCK * blk + KW)
        qs = jnp.concatenate(
            [q_ref[BLOCK * blk:BLOCK * (blk + 1), LANES * c:LANES * (c + 1)]
             for c in (2 * g, 2 * g + 1)], axis=0)
        return [lax.dot_general(k_side[(g, half)][win], qs, (((1,), (1,)), ((), ())),
                                preferred_element_type=F32) for half in range(2)]

    def softmax(item, st_pair):
        blk, g = item
        c_l = cap_first if blk == 0 else cap_l
        c_r = cap_last if blk == n_blk - 1 else cap_r
        st = jnp.concatenate(st_pair, axis=1)
        b = jnp.concatenate([bias_ref[ATTN_GROUP * g + t] for t in range(ATTN_GROUP)], axis=1)
        t = jnp.concatenate([
            jnp.minimum(st[:BLOCK] + b[:BLOCK], c_l),
            st[BLOCK:2 * BLOCK] + b[BLOCK:2 * BLOCK],
            jnp.minimum(st[2 * BLOCK:] + b[2 * BLOCK:], c_r),
        ], axis=0)
        sk = sink_ref[g]
        m = jnp.maximum(jnp.max(t, axis=0, keepdims=True), sk)
        return jnp.exp2(t - m).astype(BF16), jnp.exp2(sk - m)

    def finish(item, probs):
        blk, g = item
        p, e_sink = probs
        win = slice(BLOCK * blk, BLOCK * blk + KW)
        ot = lax.dot_general(v_ones[g][win], p, (((0,), (0,)), ((), ())),
                             preferred_element_type=F32)
        hg = g % 2
        num = ot[hd * hg:hd * (hg + 1)]
        den = ot[hd * (1 - hg):hd * (1 - hg) + 1]
        out = num * (1.0 / (den + e_sink))
        out_t = jnp.concatenate([out[:, :2 * BLOCK], out[:, 2 * BLOCK:]], axis=0)
        for r in range(2):
            c = 2 * g + r
            a_scr[BLOCK * blk:BLOCK * (blk + 1), LANES * c:LANES * (c + 1)] = (
                out_t[:, BLOCK * r:BLOCK * (r + 1)].T.astype(a_scr.dtype))

    return scores, softmax, finish


def _retention_stage(q_ref, k_ref, v_refs, g_refs, pb_ref, gnw_ref, st_ref, d_ref, qdf_ref,
                     qdb_ref, kdf_ref, gf_ref, r_scr):
    lane = lax.broadcasted_iota(jnp.int32, (CHUNK, LANES), 1)
    low_half = lane < RET_KEY_DIM
    first_v = lax.broadcasted_iota(jnp.int32, (CHUNK, 2 * RET_VAL_DIM), 1) < RET_VAL_DIM
    first_s = lax.broadcasted_iota(jnp.int32, (PB_ROWS, 2 * RET_VAL_DIM), 1) < RET_VAL_DIM

    def scores(item):
        ch, c = item
        rows = slice(CHUNK * ch, CHUNK * (ch + 1))
        kb = k_ref[rows, LANES * c:LANES * (c + 1)]
        q = q_ref[rows, LANES * c:LANES * (c + 1)]
        out = []
        for half in range(2):
            sel = low_half if half == 0 else jnp.logical_not(low_half)
            qh = jnp.where(sel, q, jnp.zeros_like(q))
            out.append(lax.dot_general(qh, kb, (((1,), (1,)), ((), ())),
                                       preferred_element_type=F32))
        return out

    def decay(item, s_pair):
        ch, c = item
        rows = slice(CHUNK * ch, CHUNK * (ch + 1))
        q = q_ref[rows, LANES * c:LANES * (c + 1)].astype(F32)
        parts = [(s_pair[half] * d_ref[2 * c + half]).astype(BF16) for half in range(2)]
        parts.append((q * qdf_ref[c]).astype(BF16))
        parts.append((q * qdb_ref[c]).astype(BF16))
        return jnp.concatenate(parts, axis=1)

    def finish(item, lhs):
        ch, c = item
        rows = slice(CHUNK * ch, CHUNK * (ch + 1))
        k = k_ref[rows, LANES * c:LANES * (c + 1)].astype(F32)
        vpair = _pair_cols(v_refs, rows, c, 2 * RET_VAL_DIM)
        zero_v = jnp.zeros_like(vpair)
        pb = pb_ref[ch, c]
        zero_s = jnp.zeros_like(pb)
        rhs = jnp.concatenate([
            jnp.where(first_v, vpair, zero_v),
            jnp.where(first_v, zero_v, vpair),
            st_ref[c].astype(BF16),
            jnp.where(first_s, pb, zero_s),
            jnp.where(first_s, zero_s, pb),
        ], axis=0)
        o = jnp.dot(lhs, rhs, preferred_element_type=F32)
        _state_update(st_ref, c, k, kdf_ref[c], gf_ref[c], vpair)
        for half in range(2):
            head = 2 * c + half
            lo_c = RET_VAL_DIM * head
            oh = o[:, RET_VAL_DIM * half:RET_VAL_DIM * (half + 1)]
            mu = jnp.mean(oh, axis=-1, keepdims=True)
            dev = oh - mu
            var = jnp.mean(dev * dev, axis=-1, keepdims=True)
            y = dev * lax.rsqrt(var + EPS) * gnw_ref[:, lo_c:lo_c + RET_VAL_DIM]
            g = _pair_cols(g_refs, rows, head, RET_VAL_DIM).astype(F32)
            r_scr[rows, lo_c:lo_c + RET_VAL_DIM] = (y * g).astype(r_scr.dtype)

    return scores, decay, finish


def _mixer_kernel(lgf_ref, lgb_ref, qa_ref, kvp_ref, kvc_ref, kvn_ref, bias_ref, cap_ref, sink_ref,
                  ga0_ref, ga1_ref, qr_ref, kr_ref, v0_ref, v1_ref, gr0_ref, gr1_ref, pb_ref,
                  gnw_ref, x_ref, anw_ref, whbm_ref, fw_ref, o_ref,
                  w_ref, stage_ref, sem_ref,
                  a_scr, r_scr, st_ref, d_ref, qdf_ref, qdb_ref, kdf_ref, gf_ref):
    @pl.when(_first_step())
    def _():
        _load_weight(whbm_ref, w_ref, stage_ref, sem_ref)
        idx = _row_index()
        col = lax.broadcasted_iota(jnp.int32, (CHUNK, CHUNK), 1).astype(F32)
        diff = idx - col
        for c in range(N_PAIRS):
            qdf_ref[c] = _lane_decay(lgf_ref, c, idx + 1.0)
            qdb_ref[c] = _lane_decay(lgb_ref, c, float(CHUNK) - idx)
            kdf_ref[c] = _lane_decay(lgf_ref, c, float(CHUNK - 1) - idx)
            gf_ref[c] = _chunk_decay(lgf_ref, c)
        for hh in range(RET_HEADS):
            d_ref[hh] = jnp.exp(jnp.where(diff >= 0.0, lgf_ref[hh] * diff, lgb_ref[hh] * (-diff)))

    @pl.when(pl.program_id(1) == 0)
    def _():
        st_ref[...] = jnp.zeros_like(st_ref)

    tq = qa_ref.shape[0]
    stage = {
        "a": _attention_stage(pl.program_id(1), pl.num_programs(1), qa_ref, kvp_ref, kvc_ref,
                              kvn_ref, bias_ref, cap_ref, sink_ref, a_scr),
        "r": _retention_stage(qr_ref, kr_ref, (v0_ref, v1_ref), (gr0_ref, gr1_ref), pb_ref,
                              gnw_ref, st_ref, d_ref, qdf_ref, qdb_ref, kdf_ref, gf_ref, r_scr),
    }

    n_grp = tq // GROUP_ROWS

    def out_group(grp):
        rows = slice(GROUP_ROWS * grp, GROUP_ROWS * (grp + 1))
        a = a_scr[rows, :].astype(F32)
        ms = jnp.mean(a * a, axis=-1, keepdims=True)
        g = jnp.concatenate([ga0_ref[rows, :], ga1_ref[rows, :]], axis=1).astype(F32)
        gated = (a * lax.rsqrt(ms + EPS) * anw_ref[...] * g).astype(BF16)
        y = (x_ref[rows, :]
             + jnp.dot(r_scr[rows, :], w_ref[ATTN_WIDTH:, :], preferred_element_type=F32)
             + jnp.dot(gated, w_ref[:ATTN_WIDTH, :], preferred_element_type=F32))
        ms = jnp.mean(y * y, axis=-1, keepdims=True)
        o_ref[rows, :] = y * lax.rsqrt(ms + EPS) * fw_ref[...]

    per = GROUP_ROWS // BLOCK
    items = []
    for grp in range(n_grp):
        for blk in range(per * grp, per * (grp + 1)):
            for u in range(ATTN_KV_HEADS):
                items.append(("a", (blk, u), grp))
                items.append(("r", (blk, u), grp))
    per_group = len(items) // n_grp
    pending = stage[items[0][0]][0](items[0][1])
    closing = None
    for idx, (kind, item, grp) in enumerate(items):
        cur = pending
        if idx + 1 < len(items):
            nxt = items[idx + 1]
            pending = stage[nxt[0]][0](nxt[1])
        mid = stage[kind][1](item, cur)
        if closing is not None:
            stage[closing[0]][2](closing[1], closing[2])
        closing = (kind, item, mid)
        if grp > 0 and idx % per_group == OUT_AFTER:
            out_group(grp - 1)
    stage[closing[0]][2](closing[1], closing[2])
    out_group(n_grp - 1)


def _mixer(proj, pb, x2, bias_t, cap_t, sink_t, lg_f, lg_b, gn_w, attn_nw, w_o_p, final_w,
           batch, seq, tq=512):
    nt = seq // tq
    bpt = tq // BLOCK
    nb = seq // BLOCK
    row = lambda b, i: b * nt + i
    col = lambda off: (lambda b, i: (row(b, i), off // CB))
    prev_blk = lambda b, i: (b * nb + jnp.maximum(bpt * i - 1, 0), OFF_KA // CB)
    next_blk = lambda b, i: (b * nb + jnp.minimum(bpt * (i + 1), nb - 1), OFF_KA // CB)
    const2 = lambda b, i: (0, 0)
    const3 = lambda b, i: (0, 0, 0)
    smem = pl.BlockSpec(memory_space=pltpu.SMEM)
    state_shape = (N_PAIRS, LANES, 2 * RET_VAL_DIM)
    lane_tab = (N_PAIRS, CHUNK, LANES)
    return pl.pallas_call(
        _mixer_kernel,
        grid=(batch, nt),
        in_specs=[
            smem, smem,
            pl.BlockSpec((tq, ATTN_WIDTH), lambda b, i: (row(b, i), OFF_QA // ATTN_WIDTH)),
            pl.BlockSpec((BLOCK, CB), prev_blk),
            pl.BlockSpec((tq, CB), col(OFF_KA)),
            pl.BlockSpec((BLOCK, CB), next_blk),
            pl.BlockSpec((ATTN_HEADS, KW, BLOCK), const3),
            pl.BlockSpec((KW, BLOCK), const2),
            pl.BlockSpec((ATTN_KV_HEADS, 1, ATTN_GROUP * BLOCK), const3),
            pl.BlockSpec((tq, CB), col(OFF_GA)),
            pl.BlockSpec((tq, CB), col(OFF_GA + CB)),
            pl.BlockSpec((tq, CB), col(OFF_QR)),
            pl.BlockSpec((tq, CB), col(OFF_KR)),
            pl.BlockSpec((tq, CB), col(OFF_VR)),
            pl.BlockSpec((tq, CB), col(OFF_VR + CB)),
            pl.BlockSpec((tq, CB), col(OFF_GR)),
            pl.BlockSpec((tq, CB), col(OFF_GR + CB)),
            pl.BlockSpec((bpt, N_PAIRS, PB_ROWS, 2 * RET_VAL_DIM), lambda b, i: (row(b, i), 0, 0, 0)),
            pl.BlockSpec((1, RET_WIDTH), const2),
            pl.BlockSpec((tq, D_MODEL), lambda b, i: (row(b, i), 0)),
            pl.BlockSpec((1, ATTN_WIDTH), const2),
            pl.BlockSpec(memory_space=pl.ANY),
            pl.BlockSpec((1, D_MODEL), const2),
        ],
        out_specs=pl.BlockSpec((tq, D_MODEL), lambda b, i: (row(b, i), 0)),
        out_shape=jax.ShapeDtypeStruct((batch * seq, D_MODEL), F32),
        scratch_shapes=[pltpu.VMEM((ATTN_WIDTH + RET_WIDTH, D_MODEL), BF16),
                        pltpu.VMEM((W_SLOTS, ATTN_WIDTH + RET_WIDTH, WB), F32),
                        pltpu.SemaphoreType.DMA((W_SLOTS,)),
                        pltpu.VMEM((tq, ATTN_WIDTH), BF16),
                        pltpu.VMEM((tq, RET_WIDTH), BF16),
                        pltpu.VMEM(state_shape, F32),
                        pltpu.VMEM((RET_HEADS, CHUNK, CHUNK), F32),
                        pltpu.VMEM(lane_tab, F32),
                        pltpu.VMEM(lane_tab, F32),
                        pltpu.VMEM(lane_tab, F32),
                        pltpu.VMEM(state_shape, F32)],
        compiler_params=pltpu.CompilerParams(
            dimension_semantics=("arbitrary", "arbitrary"),
            vmem_limit_bytes=VMEM_LIMIT),
        name="mixer",
    )(lg_f, lg_b, proj, proj, proj, proj, bias_t, cap_t, sink_t, proj, proj, proj, proj, proj, proj,
      proj, proj, pb, gn_w, x2, attn_nw, w_o_p, final_w)


def _t5_bucket(rel):
    nb = N_BUCKETS // 2
    max_exact = nb // 2
    ret = jnp.where(rel > 0, nb, 0)
    n = jnp.abs(rel)
    nf = jnp.maximum(n, 1).astype(F32)
    large = max_exact + (jnp.log(nf / max_exact) / math.log(MAX_DISTANCE / max_exact)
                         * (nb - max_exact)).astype(jnp.int32)
    large = jnp.minimum(large, nb - 1)
    return ret + jnp.where(n < max_exact, n, large)


def kernel(x, norm_w, w_in, attn_sink, rel_bias, attn_out_norm_w, ret_decay_fwd,
           ret_decay_bwd, ret_gn_w, w_out, final_norm_w):
    batch, seq, _ = x.shape
    assert norm_w.shape[0] == 1 and seq % BLOCK == 0
    x2 = x.reshape(batch * seq, D_MODEL)

    w_o_p = w_out[0]

    qi = jnp.arange(BLOCK, dtype=jnp.int32)[None, :]
    kt = jnp.arange(KW, dtype=jnp.int32)[:, None]
    rel = kt - BLOCK - qi
    bucket = _t5_bucket(rel) & (N_BUCKETS - 1)
    onehot = (bucket[None] == jnp.arange(N_BUCKETS)[:, None, None]).astype(F32)
    rb = rel_bias.astype(F32).reshape(N_BUCKETS, ATTN_KV_HEADS, 2, 2)
    rb = jnp.swapaxes(rb, 2, 3).reshape(N_BUCKETS, ATTN_HEADS)
    bias_t = jnp.einsum("nh,nkq->hkq", rb, onehot,
                        precision=lax.Precision.HIGHEST)
    bias_t = bias_t * LOG2E
    cap_t = jnp.where(jnp.abs(rel) <= WINDOW, BIG, NEG).astype(F32)
    sink = (attn_sink[0].astype(F32) * LOG2E).reshape(ATTN_KV_HEADS, 1, 2, 2)
    sink = jnp.swapaxes(sink, 2, 3)[..., None]
    sink_t = jnp.broadcast_to(sink, (ATTN_KV_HEADS, 1, 2, 2, BLOCK)).reshape(
        ATTN_KV_HEADS, 1, ATTN_GROUP * BLOCK)

    inv = ROPE_BASE ** (-jnp.arange(0, RET_KEY_DIM, 2, dtype=F32) / RET_KEY_DIM)
    inv_full = jnp.tile(jnp.repeat(inv, 2), LANES // RET_KEY_DIM)[None, :]
    sgn = np.tile(np.array([-1.0, 1.0], np.float32), LANES // 2)[None, :]

    def trig_table(pos):
        ang = pos[:, None] * inv_full
        return jnp.stack([jnp.cos(ang), jnp.sin(ang), jnp.sin(ang) * sgn])

    rowtab = trig_table(jnp.arange(2 * BLOCK, dtype=F32))
    basetab = jnp.swapaxes(trig_table(jnp.arange(0, seq, BLOCK, dtype=F32)), 0, 1)

    lg_f = jax.nn.log_sigmoid(ret_decay_fwd[0].astype(F32))
    lg_b = jax.nn.log_sigmoid(ret_decay_bwd[0].astype(F32))

    proj, pb = _inproj(x2, norm_w[0].reshape(1, D_MODEL), w_in[0], rowtab, basetab, lg_b, seq)
    out = _mixer(proj, pb, x2, bias_t, cap_t, sink_t, lg_f, lg_b,
                 ret_gn_w[0].reshape(1, RET_WIDTH), attn_out_norm_w[0].reshape(1, ATTN_WIDTH),
                 w_o_p, final_norm_w.reshape(1, D_MODEL), batch, seq)
    return out.reshape(batch, seq, D_MODEL)
```

```python
import functools
import math

import numpy as np
import jax
import jax.numpy as jnp
from jax import lax
from jax.experimental import pallas as pl
from jax.experimental.pallas import tpu as pltpu

D_MODEL = 2048
ATTN_HEAD_DIM = 64
ATTN_WIDTH = 1024
ATTN_HEADS = 16
ATTN_KV_HEADS = 4
ATTN_GROUP = 4
ATTN_KV_COLS = 256
WINDOW = 128
BLOCK = 128
N_BUCKETS = 32
MAX_DISTANCE = 128
RET_WIDTH = 1024
RET_HEADS = 8
RET_VAL_DIM = 128
RET_KEY_DIM = 64
RET_QK_COLS = 512
CHUNK = 128
ROPE_BASE = 10000.0
EPS = 1e-6
NEG = -1e30
BIG = 3e38
LOG2E = math.log2(math.e)

LANES = 128
N_PAIRS = RET_HEADS // 2
KW = 3 * BLOCK
CB = 512

OFF_QA = 0
OFF_KA = OFF_QA + ATTN_WIDTH
OFF_GA = OFF_KA + 2 * ATTN_KV_COLS
OFF_QR = OFF_GA + ATTN_WIDTH
OFF_KR = OFF_QR + RET_QK_COLS
OFF_VR = OFF_KR + RET_QK_COLS
OFF_GR = OFF_VR + RET_WIDTH
IN_WIDTH = OFF_GR + RET_WIDTH

VMEM_LIMIT = 56 * 1024 * 1024

F32 = jnp.float32
BF16 = jnp.bfloat16


def _roll_half(x):
    return pltpu.roll(x.astype(F32), LANES // 2, axis=1).astype(x.dtype)


PB_ROWS = RET_KEY_DIM


def _first_step():
    return jnp.logical_and(pl.program_id(0) == 0, pl.program_id(1) == 0)


def _lane_decay(lg_ref, c, expo):
    lane = lax.broadcasted_iota(jnp.int32, (CHUNK, LANES), 1)
    lg = jnp.where(lane < RET_KEY_DIM, lg_ref[2 * c], lg_ref[2 * c + 1])
    return jnp.exp(lg * expo)


def _state_mask():
    r = lax.broadcasted_iota(jnp.int32, (LANES, 2 * RET_VAL_DIM), 0)
    m = lax.broadcasted_iota(jnp.int32, (LANES, 2 * RET_VAL_DIM), 1)
    return (r // RET_KEY_DIM) == (m // RET_VAL_DIM)


def _chunk_decay(lg_ref, c):
    r = lax.broadcasted_iota(jnp.int32, (LANES, 2 * RET_VAL_DIM), 0)
    lg = jnp.where(r < RET_KEY_DIM, lg_ref[2 * c], lg_ref[2 * c + 1])
    return jnp.where(_state_mask(), jnp.exp(lg * float(CHUNK)), 0.0)


def _row_index():
    return lax.broadcasted_iota(jnp.int32, (CHUNK, LANES), 0).astype(F32)


def _state_update(st_ref, c, k_rot, kdec, gdec, vpair):
    kd = (k_rot * kdec).astype(BF16)
    upd = lax.dot_general(kd, vpair, (((0,), (0,)), ((), ())),
                          preferred_element_type=F32)
    st_ref[c] = st_ref[c] * gdec + jnp.where(_state_mask(), upd, 0.0)


def _pair_cols(refs, rows, c, width):
    per_ref = CB // width
    return refs[c // per_ref][rows, width * (c % per_ref):width * (c % per_ref + 1)]


def _silu(g):
    return g / (1.0 + jnp.exp(-g))


def _col_scale(col):
    if OFF_QA <= col < OFF_QA + ATTN_WIDTH:
        return ATTN_HEAD_DIM ** -0.5 * LOG2E
    if OFF_KR <= col < OFF_KR + RET_QK_COLS:
        return RET_KEY_DIM ** -0.5
    return 1.0


WB = 256
W_SLOTS = 3


def _load_weight(whbm_ref, w_ref, stage_ref, sem_ref, col_scale=lambda col: 1.0):
    n_blk = w_ref.shape[1] // WB
    slots = stage_ref.shape[0]
    rows_per_iter = 256

    def copy(c):
        return pltpu.make_async_copy(whbm_ref.at[:, pl.ds(c * WB, WB)],
                                     stage_ref.at[c % slots], sem_ref.at[c % slots])

    for c in range(slots - 1):
        copy(c).start(priority=c % 2)
    for c in range(n_blk):
        if c + slots - 1 < n_blk:
            copy(c + slots - 1).start(priority=(c + slots - 1) % 2)
        copy(c).wait()
        scale = col_scale(c * WB)

        def body(r, carry, c=c, scale=scale):
            rs = pl.ds(pl.multiple_of(r * rows_per_iter, rows_per_iter), rows_per_iter)
            w = stage_ref[c % slots, rs, :]
            if scale != 1.0:
                w = w * scale
            w_ref[rs, c * WB:(c + 1) * WB] = w.astype(BF16)
            return carry

        lax.fori_loop(0, w_ref.shape[0] // rows_per_iter, body, 0)


def _inproj_kernel(lgb_ref, x_ref, nw_ref, whbm_ref, rowtab_ref, basetab_ref, o_ref, pb_ref,
                   w_ref, stage_ref, sem_ref, st_ref, kdb_ref, gb_ref, *, row_splits,
                   tiles_per_seq):
    tm = x_ref.shape[0]
    nw = nw_ref[...]
    step = pl.program_id(0)
    tile_blk = (tiles_per_seq - 1 - step % tiles_per_seq) * (tm // BLOCK)

    @pl.when(step == 0)
    def _():
        _load_weight(whbm_ref, w_ref, stage_ref, sem_ref, _col_scale)
        idx = _row_index()
        for c in range(N_PAIRS):
            kdb_ref[c] = _lane_decay(lgb_ref, c, idx)
            gb_ref[c] = _chunk_decay(lgb_ref, c)

    @pl.when(step % tiles_per_seq == 0)
    def _():
        st_ref[...] = jnp.zeros_like(st_ref)

    start = 0
    for rc in row_splits:
        rows = slice(start, start + rc)
        x = x_ref[rows, :]
        ms = jnp.mean(x * x, axis=-1, keepdims=True)
        h = (x * lax.rsqrt(ms + EPS) * nw).astype(BF16)
        ca, sa, sas = rowtab_ref[0, :rc, :], rowtab_ref[1, :rc, :], rowtab_ref[2, :rc, :]
        base = basetab_ref[tile_blk + start // BLOCK]
        cb, sb, sbs = base[0:1, :], base[1:2, :], base[2:3, :]
        cos = ca * cb - sa * sb
        sin_signed = sas * cb + ca * sbs
        lane = lax.broadcasted_iota(jnp.int32, (rc, LANES), 1)
        even = (lane % 2) == 0
        start += rc
        for col in range(0, IN_WIDTH, CB):
            res = jnp.dot(h, w_ref[:, col:col + CB], preferred_element_type=F32)
            if OFF_GA <= col < OFF_QR or col >= OFF_GR:
                res = _silu(res)
            elif OFF_QR <= col < OFF_VR:
                parts = []
                for c in range(CB // LANES):
                    v = res[:, LANES * c:LANES * (c + 1)]
                    partner = jnp.where(even, pltpu.roll(v, LANES - 1, axis=1),
                                        pltpu.roll(v, 1, axis=1))
                    parts.append(v * cos + partner * sin_signed)
                res = jnp.concatenate(parts, axis=1)
            o_ref[rows, col:col + CB] = res.astype(o_ref.dtype)

    for ch in reversed(range(tm // CHUNK)):
        rows = slice(CHUNK * ch, CHUNK * (ch + 1))
        for c in range(N_PAIRS):
            st = st_ref[c]
            pb_ref[ch, c] = (st[:PB_ROWS] + st[PB_ROWS:]).astype(BF16)
            k = o_ref[rows, OFF_KR + LANES * c:OFF_KR + LANES * (c + 1)].astype(F32)
            vpair = o_ref[rows, OFF_VR + 2 * RET_VAL_DIM * c:OFF_VR + 2 * RET_VAL_DIM * (c + 1)]
            _state_update(st_ref, c, k, kdb_ref[c], gb_ref[c], vpair)


def _inproj(x2, norm_w, w_in, rowtab, basetab, lg_b, seq, tm=512, row_splits=(256, 256)):
    m = x2.shape[0]
    assert sum(row_splits) == tm and max(row_splits) <= rowtab.shape[1]
    tps = seq // tm
    cpt = tm // CHUNK
    tile = lambda i: (i // tps) * tps + (tps - 1 - i % tps)
    return pl.pallas_call(
        functools.partial(_inproj_kernel, row_splits=row_splits, tiles_per_seq=tps),
        grid=(m // tm,),
        in_specs=[
            pl.BlockSpec(memory_space=pltpu.SMEM),
            pl.BlockSpec((tm, D_MODEL), lambda i: (tile(i), 0)),
            pl.BlockSpec((1, D_MODEL), lambda i: (0, 0)),
            pl.BlockSpec(memory_space=pl.ANY),
            pl.BlockSpec(rowtab.shape, lambda i: (0, 0, 0)),
            pl.BlockSpec(basetab.shape, lambda i: (0, 0, 0)),
        ],
        out_specs=[
            pl.BlockSpec((tm, IN_WIDTH), lambda i: (tile(i), 0)),
            pl.BlockSpec((cpt, N_PAIRS, PB_ROWS, 2 * RET_VAL_DIM), lambda i: (tile(i), 0, 0, 0)),
        ],
        out_shape=[
            jax.ShapeDtypeStruct((m, IN_WIDTH), BF16),
            jax.ShapeDtypeStruct((m // CHUNK, N_PAIRS, PB_ROWS, 2 * RET_VAL_DIM), BF16),
        ],
        scratch_shapes=[pltpu.VMEM((D_MODEL, IN_WIDTH), BF16),
                        pltpu.VMEM((W_SLOTS, D_MODEL, WB), F32),
                        pltpu.SemaphoreType.DMA((W_SLOTS,)),
                        pltpu.VMEM((N_PAIRS, LANES, 2 * RET_VAL_DIM), F32),
                        pltpu.VMEM((N_PAIRS, CHUNK, LANES), F32),
                        pltpu.VMEM((N_PAIRS, LANES, 2 * RET_VAL_DIM), F32)],
        compiler_params=pltpu.CompilerParams(
            dimension_semantics=("arbitrary",),
            vmem_limit_bytes=VMEM_LIMIT),
        name="inproj",
    )(lg_b, x2, norm_w, w_in, rowtab, basetab)


GROUP_ROWS = 2 * BLOCK
OUT_AFTER = 1


def _attention_stage(i, nt, q_ref, kvp_ref, kvc_ref, kvn_ref, bias_ref, cap_ref, sink_ref, a_scr):
    n_blk = q_ref.shape[0] // BLOCK
    hd = ATTN_HEAD_DIM
    kv = jnp.concatenate([kvp_ref[...], kvc_ref[...], kvn_ref[...]], axis=0)
    lane = lax.broadcasted_iota(jnp.int32, (kv.shape[0], LANES), 1)
    low = lane < hd

    tile4 = lambda c: jnp.concatenate([c] * ATTN_GROUP, axis=1)
    cap_l, cap_r = tile4(cap_ref[:BLOCK]), tile4(cap_ref[2 * BLOCK:])
    cap_first = jnp.where(i > 0, cap_l, NEG)
    cap_last = jnp.where(i < nt - 1, cap_r, NEG)

    k_side, v_ones = {}, {}
    for j in range(ATTN_KV_COLS // LANES):
        kj = kv[:, LANES * j:LANES * (j + 1)]
        vj = kv[:, ATTN_KV_COLS + LANES * j:ATTN_KV_COLS + LANES * (j + 1)]
        for hg in range(2):
            mine = low if hg == 0 else jnp.logical_not(low)
            k_here = jnp.where(mine, kj, jnp.zeros_like(kj))
            k_side[(2 * j + hg, hg)] = k_here
            k_side[(2 * j + hg, 1 - hg)] = _roll_half(k_here)
            v_ones[2 * j + hg] = jnp.where(mine, vj, jnp.ones_like(vj))

    def scores(item):
        blk, g = item
        win = slice(BLOCK * blk, BLOCK * blk + KW)
        qs = jnp.concatenate(
            [q_ref[BLOCK * blk:BLOCK * (blk + 1), LANES * c:LANES * (c + 1)]
             for c in (2 * g, 2 * g + 1)], axis=0)
        return [lax.dot_general(k_side[(g, half)][win], qs, (((1,), (1,)), ((), ())),
                                preferred_element_type=F32) for half in range(2)]

    def softmax(item, st_pair):
        blk, g = item
        c_l = cap_first if blk == 0 else cap_l
        c_r = cap_last if blk == n_blk - 1 else cap_r
        st = jnp.concatenate(st_pair, axis=1)
        b = jnp.concatenate([bias_ref[ATTN_GROUP * g + t] for t in range(ATTN_GROUP)], axis=1)
        t = jnp.concatenate([
            jnp.minimum(st[:BLOCK] + b[:BLOCK], c_l),
            st[BLOCK:2 * BLOCK] + b[BLOCK:2 * BLOCK],
            jnp.minimum(st[2 * BLOCK:] + b[2 * BLOCK:], c_r),
        ], axis=0)
        sk = sink_ref[g]
        m = jnp.maximum(jnp.max(t, axis=0, keepdims=True), sk)
        return jnp.exp2(t - m).astype(BF16), jnp.exp2(sk - m)

    def finish(item, probs):
        blk, g = item
        p, e_sink = probs
        win = slice(BLOCK * blk, BLOCK * blk + KW)
        ot = lax.dot_general(v_ones[g][win], p, (((0,), (0,)), ((), ())),
                             preferred_element_type=F32)
        hg = g % 2
        num = ot[hd * hg:hd * (hg + 1)]
        den = ot[hd * (1 - hg):hd * (1 - hg) + 1]
        out = num * (1.0 / (den + e_sink))
        out_t = jnp.concatenate([out[:, :2 * BLOCK], out[:, 2 * BLOCK:]], axis=0)
        for r in range(2):
            c = 2 * g + r
            a_scr[BLOCK * blk:BLOCK * (blk + 1), LANES * c:LANES * (c + 1)] = (
                out_t[:, BLOCK * r:BLOCK * (r + 1)].T.astype(a_scr.dtype))

    return scores, softmax, finish


def _retention_stage(q_ref, k_ref, v_refs, g_refs, pb_ref, gnw_ref, st_ref, d_ref, qdf_ref,
                     qdb_ref, kdf_ref, gf_ref, r_scr):
    lane = lax.broadcasted_iota(jnp.int32, (CHUNK, LANES), 1)
    low_half = lane < RET_KEY_DIM
    first_v = lax.broadcasted_iota(jnp.int32, (CHUNK, 2 * RET_VAL_DIM), 1) < RET_VAL_DIM
    first_s = lax.broadcasted_iota(jnp.int32, (PB_ROWS, 2 * RET_VAL_DIM), 1) < RET_VAL_DIM

    def scores(item):
        ch, c = item
        rows = slice(CHUNK * ch, CHUNK * (ch + 1))
        kb = k_ref[rows, LANES * c:LANES * (c + 1)]
        q = q_ref[rows, LANES * c:LANES * (c + 1)]
        out = []
        for half in range(2):
            sel = low_half if half == 0 else jnp.logical_not(low_half)
            qh = jnp.where(sel, q, jnp.zeros_like(q))
            out.append(lax.dot_general(qh, kb, (((1,), (1,)), ((), ())),
                                       preferred_element_type=F32))
        return out

    def decay(item, s_pair):
        ch, c = item
        rows = slice(CHUNK * ch, CHUNK * (ch + 1))
        q = q_ref[rows, LANES * c:LANES * (c + 1)].astype(F32)
        parts = [(s_pair[half] * d_ref[2 * c + half]).astype(BF16) for half in range(2)]
        parts.append((q * qdf_ref[c]).astype(BF16))
        parts.append((q * qdb_ref[c]).astype(BF16))
        return jnp.concatenate(parts, axis=1)

    def finish(item, lhs):
        ch, c = item
        rows = slice(CHUNK * ch, CHUNK * (ch + 1))
        k = k_ref[rows, LANES * c:LANES * (c + 1)].astype(F32)
        vpair = _pair_cols(v_refs, rows, c, 2 * RET_VAL_DIM)
        zero_v = jnp.zeros_like(vpair)
        pb = pb_ref[ch, c]
        zero_s = jnp.zeros_like(pb)
        rhs = jnp.concatenate([
            jnp.where(first_v, vpair, zero_v),
            jnp.where(first_v, zero_v, vpair),
            st_ref[c].astype(BF16),
            jnp.where(first_s, pb, zero_s),
            jnp.where(first_s, zero_s, pb),
        ], axis=0)
        o = jnp.dot(lhs, rhs, preferred_element_type=F32)
        _state_update(st_ref, c, k, kdf_ref[c], gf_ref[c], vpair)
        for half in range(2):
            head = 2 * c + half
            lo_c = RET_VAL_DIM * head
            oh = o[:, RET_VAL_DIM * half:RET_VAL_DIM * (half + 1)]
            mu = jnp.mean(oh, axis=-1, keepdims=True)
            dev = oh - mu
            var = jnp.mean(dev * dev, axis=-1, keepdims=True)
            y = dev * lax.rsqrt(var + EPS) * gnw_ref[:, lo_c:lo_c + RET_VAL_DIM]
            g = _pair_cols(g_refs, rows, head, RET_VAL_DIM).astype(F32)
            r_scr[rows, lo_c:lo_c + RET_VAL_DIM] = (y * g).astype(r_scr.dtype)

    return scores, decay, finish


def _mixer_kernel(lgf_ref, lgb_ref, qa_ref, kvp_ref, kvc_ref, kvn_ref, bias_ref, cap_ref, sink_ref,
                  ga0_ref, ga1_ref, qr_ref, kr_ref, v0_ref, v1_ref, gr0_ref, gr1_ref, pb_ref,
                  gnw_ref, x_ref, anw_ref, whbm_ref, fw_ref, o_ref,
                  w_ref, stage_ref, sem_ref,
                  a_scr, r_scr, st_ref, d_ref, qdf_ref, qdb_ref, kdf_ref, gf_ref):
    @pl.when(_first_step())
    def _():
        _load_weight(whbm_ref, w_ref, stage_ref, sem_ref)
        idx = _row_index()
        col = lax.broadcasted_iota(jnp.int32, (CHUNK, CHUNK), 1).astype(F32)
        diff = idx - col
        for c in range(N_PAIRS):
            qdf_ref[c] = _lane_decay(lgf_ref, c, idx + 1.0)
            qdb_ref[c] = _lane_decay(lgb_ref, c, float(CHUNK) - idx)
            kdf_ref[c] = _lane_decay(lgf_ref, c, float(CHUNK - 1) - idx)
            gf_ref[c] = _chunk_decay(lgf_ref, c)
        for hh in range(RET_HEADS):
            d_ref[hh] = jnp.exp(jnp.where(diff >= 0.0, lgf_ref[hh] * diff, lgb_ref[hh] * (-diff)))

    @pl.when(pl.program_id(1) == 0)
    def _():
        st_ref[...] = jnp.zeros_like(st_ref)

    tq = qa_ref.shape[0]
    stage = {
        "a": _attention_stage(pl.program_id(1), pl.num_programs(1), qa_ref, kvp_ref, kvc_ref,
                              kvn_ref, bias_ref, cap_ref, sink_ref, a_scr),
        "r": _retention_stage(qr_ref, kr_ref, (v0_ref, v1_ref), (gr0_ref, gr1_ref), pb_ref,
                              gnw_ref, st_ref, d_ref, qdf_ref, qdb_ref, kdf_ref, gf_ref, r_scr),
    }

    n_grp = tq // GROUP_ROWS

    def out_group(grp):
        rows = slice(GROUP_ROWS * grp, GROUP_ROWS * (grp + 1))
        a = a_scr[rows, :].astype(F32)
        ms = jnp.mean(a * a, axis=-1, keepdims=True)
        g = jnp.concatenate([ga0_ref[rows, :], ga1_ref[rows, :]], axis=1).astype(F32)
        gated = (a * lax.rsqrt(ms + EPS) * anw_ref[...] * g).astype(BF16)
        y = (x_ref[rows, :]
             + jnp.dot(r_scr[rows, :], w_ref[ATTN_WIDTH:, :], preferred_element_type=F32)
             + jnp.dot(gated, w_ref[:ATTN_WIDTH, :], preferred_element_type=F32))
        ms = jnp.mean(y * y, axis=-1, keepdims=True)
        o_ref[rows, :] = y * lax.rsqrt(ms + EPS) * fw_ref[...]

    per = GROUP_ROWS // BLOCK
    items = []
    for grp in range(n_grp):
        for blk in range(per * grp, per * (grp + 1)):
            for u in range(ATTN_KV_HEADS):
                items.append(("a", (blk, u), grp))
                items.append(("r", (blk, u), grp))
    per_group = len(items) // n_grp
    pending = stage[items[0][0]][0](items[0][1])
    closing = None
    for idx, (kind, item, grp) in enumerate(items):
        cur = pending
        if idx + 1 < len(items):
            nxt = items[idx + 1]
            pending = stage[nxt[0]][0](nxt[1])
        mid = stage[kind][1](item, cur)
        if closing is not None:
            stage[closing[0]][2](closing[1], closing[2])
        closing = (kind, item, mid)
        if grp > 0 and idx % per_group == OUT_AFTER:
            out_group(grp - 1)
    stage[closing[0]][2](closing[1], closing[2])
    out_group(n_grp - 1)


def _mixer(proj, pb, x2, bias_t, cap_t, sink_t, lg_f, lg_b, gn_w, attn_nw, w_o_p, final_w,
           batch, seq, tq=512):
    nt = seq // tq
    bpt = tq // BLOCK
    nb = seq // BLOCK
    row = lambda b, i: b * nt + i
    col = lambda off: (lambda b, i: (row(b, i), off // CB))
    prev_blk = lambda b, i: (b * nb + jnp.maximum(bpt * i - 1, 0), OFF_KA // CB)
    next_blk = lambda b, i: (b * nb + jnp.minimum(bpt * (i + 1), nb - 1), OFF_KA // CB)
    const2 = lambda b, i: (0, 0)
    const3 = lambda b, i: (0, 0, 0)
    smem = pl.BlockSpec(memory_space=pltpu.SMEM)
    state_shape = (N_PAIRS, LANES, 2 * RET_VAL_DIM)
    lane_tab = (N_PAIRS, CHUNK, LANES)
    return pl.pallas_call(
        _mixer_kernel,
        grid=(batch, nt),
        in_specs=[
            smem, smem,
            pl.BlockSpec((tq, ATTN_WIDTH), lambda b, i: (row(b, i), OFF_QA // ATTN_WIDTH)),
            pl.BlockSpec((BLOCK, CB), prev_blk),
            pl.BlockSpec((tq, CB), col(OFF_KA)),
            pl.BlockSpec((BLOCK, CB), next_blk),
            pl.BlockSpec((ATTN_HEADS, KW, BLOCK), const3),
            pl.BlockSpec((KW, BLOCK), const2),
            pl.BlockSpec((ATTN_KV_HEADS, 1, ATTN_GROUP * BLOCK), const3),
            pl.BlockSpec((tq, CB), col(OFF_GA)),
            pl.BlockSpec((tq, CB), col(OFF_GA + CB)),
            pl.BlockSpec((tq, CB), col(OFF_QR)),
            pl.BlockSpec((tq, CB), col(OFF_KR)),
            pl.BlockSpec((tq, CB), col(OFF_VR)),
            pl.BlockSpec((tq, CB), col(OFF_VR + CB)),
            pl.BlockSpec((tq, CB), col(OFF_GR)),
            pl.BlockSpec((tq, CB), col(OFF_GR + CB)),
            pl.BlockSpec((bpt, N_PAIRS, PB_ROWS, 2 * RET_VAL_DIM), lambda b, i: (row(b, i), 0, 0, 0)),
            pl.BlockSpec((1, RET_WIDTH), const2),
            pl.BlockSpec((tq, D_MODEL), lambda b, i: (row(b, i), 0)),
            pl.BlockSpec((1, ATTN_WIDTH), const2),
            pl.BlockSpec(memory_space=pl.ANY),
            pl.BlockSpec((1, D_MODEL), const2),
        ],
        out_specs=pl.BlockSpec((tq, D_MODEL), lambda b, i: (row(b, i), 0)),
        out_shape=jax.ShapeDtypeStruct((batch * seq, D_MODEL), F32),
        scratch_shapes=[pltpu.VMEM((ATTN_WIDTH + RET_WIDTH, D_MODEL), BF16),
                        pltpu.VMEM((W_SLOTS, ATTN_WIDTH + RET_WIDTH, WB), F32),
                        pltpu.SemaphoreType.DMA((W_SLOTS,)),
                        pltpu.VMEM((tq, ATTN_WIDTH), BF16),
                        pltpu.VMEM((tq, RET_WIDTH), BF16),
                        pltpu.VMEM(state_shape, F32),
                        pltpu.VMEM((RET_HEADS, CHUNK, CHUNK), F32),
                        pltpu.VMEM(lane_tab, F32),
                        pltpu.VMEM(lane_tab, F32),
                        pltpu.VMEM(lane_tab, F32),
                        pltpu.VMEM(state_shape, F32)],
        compiler_params=pltpu.CompilerParams(
            dimension_semantics=("arbitrary", "arbitrary"),
            vmem_limit_bytes=VMEM_LIMIT),
        name="mixer",
    )(lg_f, lg_b, proj, proj, proj, proj, bias_t, cap_t, sink_t, proj, proj, proj, proj, proj, proj,
      proj, proj, pb, gn_w, x2, attn_nw, w_o_p, final_w)


def _t5_bucket(rel):
    nb = N_BUCKETS // 2
    max_exact = nb // 2
    ret = jnp.where(rel > 0, nb, 0)
    n = jnp.abs(rel)
    nf = jnp.maximum(n, 1).astype(F32)
    large = max_exact + (jnp.log(nf / max_exact) / math.log(MAX_DISTANCE / max_exact)
                         * (nb - max_exact)).astype(jnp.int32)
    large = jnp.minimum(large, nb - 1)
    return ret + jnp.where(n < max_exact, n, large)


def kernel(x, norm_w, w_in, attn_sink, rel_bias, attn_out_norm_w, ret_decay_fwd,
           ret_decay_bwd, ret_gn_w, w_out, final_norm_w):
    batch, seq, _ = x.shape
    assert norm_w.shape[0] == 1 and seq % BLOCK == 0
    x2 = x.reshape(batch * seq, D_MODEL)

    w_o_p = w_out[0]

    qi = jnp.arange(BLOCK, dtype=jnp.int32)[None, :]
    kt = jnp.arange(KW, dtype=jnp.int32)[:, None]
    rel = kt - BLOCK - qi
    bucket = _t5_bucket(rel) & (N_BUCKETS - 1)
    onehot = (bucket[None] == jnp.arange(N_BUCKETS)[:, None, None]).astype(F32)
    rb = rel_bias.astype(F32).reshape(N_BUCKETS, ATTN_KV_HEADS, 2, 2)
    rb = jnp.swapaxes(rb, 2, 3).reshape(N_BUCKETS, ATTN_HEADS)
    bias_t = jnp.einsum("nh,nkq->hkq", rb, onehot,
                        precision=lax.Precision.HIGHEST)
    bias_t = bias_t * LOG2E
    cap_t = jnp.where(jnp.abs(rel) <= WINDOW, BIG, NEG).astype(F32)
    sink = (attn_sink[0].astype(F32) * LOG2E).reshape(ATTN_KV_HEADS, 1, 2, 2)
    sink = jnp.swapaxes(sink, 2, 3)[..., None]
    sink_t = jnp.broadcast_to(sink, (ATTN_KV_HEADS, 1, 2, 2, BLOCK)).reshape(
        ATTN_KV_HEADS, 1, ATTN_GROUP * BLOCK)

    inv = ROPE_BASE ** (-jnp.arange(0, RET_KEY_DIM, 2, dtype=F32) / RET_KEY_DIM)
    inv_full = jnp.tile(jnp.repeat(inv, 2), LANES // RET_KEY_DIM)[None, :]
    sgn = np.tile(np.array([-1.0, 1.0], np.float32), LANES // 2)[None, :]

    def trig_table(pos):
        ang = pos[:, None] * inv_full
        return jnp.stack([jnp.cos(ang), jnp.sin(ang), jnp.sin(ang) * sgn])

    rowtab = trig_table(jnp.arange(2 * BLOCK, dtype=F32))
    basetab = jnp.swapaxes(trig_table(jnp.arange(0, seq, BLOCK, dtype=F32)), 0, 1)

    lg_f = jax.nn.log_sigmoid(ret_decay_fwd[0].astype(F32))
    lg_b = jax.nn.log_sigmoid(ret_decay_bwd[0].astype(F32))

    proj, pb = _inproj(x2, norm_w[0].reshape(1, D_MODEL), w_in[0], rowtab, basetab, lg_b, seq)
    out = _mixer(proj, pb, x2, bias_t, cap_t, sink_t, lg_f, lg_b,
                 ret_gn_w[0].reshape(1, RET_WIDTH), attn_out_norm_w[0].reshape(1, ATTN_WIDTH),
                 w_o_p, final_norm_w.reshape(1, D_MODEL), batch, seq)
    return out.reshape(batch, seq, D_MODEL)
```
